```python
import math
import jax, jax.numpy as jnp
from jax import lax
import numpy as np

D_MODEL = 2048
BATCH = 2
SEQ = 4096
DEPTH = 4
DEC_BATCH = 32
DEC_SEQ = 1
PAST_LEN = 16384
PAGE_SIZE = 128

N_HEADS_A = 16
N_KV_A = 4
HD_A = 64
GROUP_A = N_HEADS_A // N_KV_A
WINDOW = 128
BLOCK_A = 128
ROPE_THETA = 10000.0
D_RNN = 1024
N_BLOCKS_B = 16
BS_B = D_RNN // N_BLOCKS_B
CONV_B = 4
LRU_C = 8.0
N_HEADS_C = 8
DK_C = 128
DV_C = 256
CHUNK_C = 64
D_FF = 6144
CONV_F = 3
EPS = 1e-6

N_A_LAYERS = (DEPTH + 1) // 2
N_C_LAYERS = DEPTH // 2
QA = N_HEADS_A * HD_A
KA = N_KV_A * HD_A
IN_A = QA + 2 * KA + 2 * D_RNN
QC = N_HEADS_C * DK_C
VC = N_HEADS_C * DV_C
IN_C = 2 * QC + 2 * VC + 2 * N_HEADS_C

kernel_name = 'hybrid_swa_rglru_mlstm_convffn_step'


def rmsnorm(x, g):
    xf = x.astype(jnp.float32)
    y = xf * lax.rsqrt(jnp.mean(xf * xf, axis=-1, keepdims=True) + EPS)
    return (y * g.astype(jnp.float32)).astype(x.dtype)


def rope(x, pos0):
    T = x.shape[1]
    half = x.shape[-1] // 2
    pos = (pos0 + jnp.arange(T)).astype(jnp.float32)
    inv = ROPE_THETA ** (-jnp.arange(half, dtype=jnp.float32) / half)
    ang = pos[:, None] * inv[None, :]
    cos = jnp.cos(ang)[None, :, None, :]
    sin = jnp.sin(ang)[None, :, None, :]
    xf = x.astype(jnp.float32)
    x1, x2 = xf[..., :half], xf[..., half:]
    return jnp.concatenate([x1 * cos - x2 * sin, x2 * cos + x1 * sin], axis=-1).astype(x.dtype)


def causal_dwconv(x, past, w, b):
    width = w.shape[0]
    T = x.shape[1]
    xp = jnp.concatenate([past.astype(x.dtype), x], axis=1)
    y = xp[:, 0:T] * w[0]
    for j in range(1, width):
        y = y + xp[:, j:j + T] * w[j]
    return y + b, xp[:, -(width - 1):]


def sink_probs(s, valid, sinks):
    sk = sinks.astype(jnp.float32).reshape(N_KV_A, GROUP_A, 1, 1)
    s = jnp.where(valid, s, -jnp.inf)
    m = jnp.maximum(jnp.max(s, axis=-1, keepdims=True), sk)
    p = jnp.exp(s - m)
    return p / (jnp.sum(p, axis=-1, keepdims=True) + jnp.exp(sk - m))


def swa_prompt(q, k, v, sinks):
    N, S = q.shape[:2]
    nb = S // BLOCK_A
    qb = q.reshape(N, nb, BLOCK_A, N_KV_A, GROUP_A, HD_A)
    kb = k.reshape(N, nb, BLOCK_A, N_KV_A, HD_A)
    vb = v.reshape(N, nb, BLOCK_A, N_KV_A, HD_A)
    pad = ((0, 0), (1, 0), (0, 0), (0, 0), (0, 0))
    kband = jnp.concatenate([jnp.pad(kb, pad)[:, :-1], kb], axis=2)
    vband = jnp.concatenate([jnp.pad(vb, pad)[:, :-1], vb], axis=2)
    qpos = jnp.arange(nb)[:, None] * BLOCK_A + jnp.arange(BLOCK_A)[None, :]
    kpos = (jnp.arange(nb)[:, None] - 1) * BLOCK_A + jnp.arange(2 * BLOCK_A)[None, :]
    diff = qpos[:, :, None] - kpos[:, None, :]
    valid = (kpos[:, None, :] >= 0) & (diff >= 0) & (diff <= WINDOW)
    s = jnp.einsum('zcqhgd,zcshd->zchgqs', qb, kband, preferred_element_type=jnp.float32) * (HD_A ** -0.5)
    p = sink_probs(s, valid[None, :, None, None], sinks)
    o = jnp.einsum('zchgqs,zcshd->zcqhgd', p.astype(v.dtype), vband)
    return o.reshape(N, S, QA)


def swa_sample(q, k, v, k_cache, v_cache, sinks):
    N, T = q.shape[:2]
    W = k_cache.shape[1]
    kall = jnp.concatenate([k_cache.astype(k.dtype), k], axis=1)
    vall = jnp.concatenate([v_cache.astype(v.dtype), v], axis=1)
    qpos = PAST_LEN + jnp.arange(T)
    kpos = jnp.concatenate([PAST_LEN - W + jnp.arange(W), qpos])
    diff = qpos[:, None] - kpos[None, :]
    valid = (diff >= 0) & (diff <= WINDOW)
    qg = q.reshape(N, T, N_KV_A, GROUP_A, HD_A)
    s = jnp.einsum('ztkgd,zskd->zkgts', qg, kall, preferred_element_type=jnp.float32) * (HD_A ** -0.5)
    p = sink_probs(s, valid, sinks)
    o = jnp.einsum('zkgts,zskd->ztkgd', p.astype(v.dtype), vall)
    return o.reshape(N, T, QA), kall[:, -W:], vall[:, -W:]


def rglru(xb, h0, wa, ba, wx, bx, lam):
    N, T = xb.shape[:2]
    xblk = xb.reshape(N, T, N_BLOCKS_B, BS_B)
    r = jax.nn.sigmoid((jnp.einsum('ztni,nij->ztnj', xblk, wa).reshape(N, T, D_RNN) + ba).astype(jnp.float32))
    i = jax.nn.sigmoid((jnp.einsum('ztni,nij->ztnj', xblk, wx).reshape(N, T, D_RNN) + bx).astype(jnp.float32))
    log_a = LRU_C * r * jax.nn.log_sigmoid(lam.astype(jnp.float32))
    a = jnp.exp(log_a)
    bterm = jnp.sqrt(-jnp.expm1(2.0 * log_a)) * (i * xb.astype(jnp.float32))
    bterm = bterm.at[:, 0].add(a[:, 0] * h0.astype(jnp.float32))

    def comb(left, right):
        a_l, b_l = left
        a_r, b_r = right
        return a_l * a_r, a_r * b_l + b_r

    _, h = lax.associative_scan(comb, (a, bterm), axis=1)
    return h.astype(xb.dtype), h[:, -1].astype(h0.dtype)


def swa_lru_mixer(hn, pos0, k_cache, v_cache, lru_h0, lru_conv0, w_in, sinks, conv_w, conv_b,
                  wa, ba, wx, bx, lam, w_out):
    N, T, _ = hn.shape
    proj = hn @ w_in
    q, k, v, xr, gr = jnp.split(proj, [QA, QA + KA, QA + 2 * KA, QA + 2 * KA + D_RNN], axis=-1)
    q = rope(q.reshape(N, T, N_HEADS_A, HD_A), pos0)
    k = rope(k.reshape(N, T, N_KV_A, HD_A), pos0)
    v = v.reshape(N, T, N_KV_A, HD_A)
    if k_cache is None:
        att = swa_prompt(q, k, v, sinks)
        cw = min(WINDOW, T)
        k_new, v_new = k[:, -cw:], v[:, -cw:]
    else:
        att, k_new, v_new = swa_sample(q, k, v, k_cache, v_cache, sinks)
        k_new, v_new = k_new.astype(k_cache.dtype), v_new.astype(v_cache.dtype)
    xc, conv_new = causal_dwconv(xr, lru_conv0, conv_w, conv_b)
    hr, h_last = rglru(xc, lru_h0, wa, ba, wx, bx, lam)
    rec = hr * jax.nn.gelu(gr, approximate=True)
    out = jnp.concatenate([att, rec], axis=-1) @ w_out
    return out, k_new, v_new, h_last, conv_new.astype(lru_conv0.dtype)


def mlstm_scan(q, k, v, i_pre, logf, C0, n0, m0):
    N, T = q.shape[:2]
    L = math.gcd(T, CHUNK_C)
    nc = T // L
    f32 = jnp.float32

    def chunks(a):
        return jnp.moveaxis(a.astype(f32).reshape((N, nc, L) + a.shape[2:]), 1, 0)

    causal = jnp.tril(jnp.ones((L, L), dtype=bool))

    def step(carry, xs):
        C, n, m = carry
        qc, kc, vc, ic, fc = xs
        b = jnp.cumsum(fc, axis=1).transpose(0, 2, 1)
        ih = ic.transpose(0, 2, 1)
        g = b + m[..., None]
        D = jnp.where(causal, b[..., :, None] - b[..., None, :] + ih[..., None, :], -jnp.inf)
        mt = jnp.maximum(g, jnp.max(D, axis=-1))
        inter = jnp.exp(g - mt)
        A = jnp.exp(D - mt[..., None]) * jnp.einsum('zthd,zshd->zhts', qc, kc)
        num = inter[..., None] * jnp.einsum('zthd,zhde->zhte', qc, C) + jnp.einsum('zhts,zshe->zhte', A, vc)
        den = inter * jnp.einsum('zthd,zhd->zht', qc, n) + jnp.sum(A, axis=-1)
        h = num / jnp.maximum(jnp.abs(den), jnp.exp(-mt))[..., None]
        m_last = mt[..., -1]
        decay = jnp.exp(g[..., -1] - m_last)
        w_last = jnp.exp(b[..., -1:] - b + ih - m_last[..., None])
        C = decay[..., None, None] * C + jnp.einsum('zhs,zshd,zshe->zhde', w_last, kc, vc)
        n = decay[..., None] * n + jnp.einsum('zhs,zshd->zhd', w_last, kc)
        return (C, n, m_last), h.transpose(0, 2, 1, 3)

    (C, n, m), hs = lax.scan(step, (C0.astype(f32), n0.astype(f32), m0.astype(f32)),
                             (chunks(q), chunks(k), chunks(v), chunks(i_pre), chunks(logf)))
    h = jnp.moveaxis(hs, 0, 1).reshape(N, T, N_HEADS_C, DV_C)
    return h, C.astype(C0.dtype), n.astype(n0.dtype), m.astype(m0.dtype)


def mlstm_mixer(hn, C0, n0, m0, w_in, b_i, b_f, b_o, g_norm, w_out):
    N, T, _ = hn.shape
    proj = hn @ w_in
    q, k, v, o, ip, fp = jnp.split(proj, [QC, 2 * QC, 2 * QC + VC, 2 * QC + 2 * VC, 2 * QC + 2 * VC + N_HEADS_C], axis=-1)
    q = q.reshape(N, T, N_HEADS_C, DK_C)
    k = k.reshape(N, T, N_HEADS_C, DK_C) * (DK_C ** -0.5)
    v = v.reshape(N, T, N_HEADS_C, DV_C)
    i_pre = ip.astype(jnp.float32) + b_i.astype(jnp.float32)
    logf = jax.nn.log_sigmoid(fp.astype(jnp.float32) + b_f.astype(jnp.float32))
    h, C, n, m = mlstm_scan(q, k, v, i_pre, logf, C0, n0, m0)
    h = rmsnorm(h, g_norm.reshape(N_HEADS_C, DV_C)).reshape(N, T, VC)
    og = jax.nn.sigmoid(o.astype(jnp.float32) + b_o.astype(jnp.float32))
    out = (og * h.astype(jnp.float32)).astype(hn.dtype) @ w_out
    return out, C, n, m


def conv_ffn(hn, conv0, w_up, conv_w, conv_b, w_down):
    u, conv_new = causal_dwconv(hn @ w_up, conv0, conv_w, conv_b)
    g, val = jnp.split(u, 2, axis=-1)
    return (jax.nn.gelu(g, approximate=True) * val) @ w_down, conv_new.astype(conv0.dtype)


def run_trunk(x, pos0, k_cache, v_cache, lru_h, lru_conv, m_C, m_n, m_m, f_conv, w):
    outs = {'k': [], 'v': [], 'h': [], 'conv': [], 'C': [], 'n': [], 'm': [], 'f': []}
    for l in range(DEPTH):
        j = l // 2
        hn = rmsnorm(x, w['norm_mix_pre'][l])
        if l % 2 == 0:
            mix, k_new, v_new, h_new, c_new = swa_lru_mixer(
                hn, pos0,
                None if k_cache is None else k_cache[j],
                None if v_cache is None else v_cache[j],
                lru_h[j], lru_conv[j],
                w['a_w_in'][j], w['a_sinks'][j], w['a_conv_w'][j], w['a_conv_b'][j],
                w['a_lru_wa'][j], w['a_lru_ba'][j], w['a_lru_wx'][j], w['a_lru_bx'][j],
                w['a_lru_lambda'][j], w['a_w_out'][j])
            outs['k'].append(k_new)
            outs['v'].append(v_new)
            outs['h'].append(h_new)
            outs['conv'].append(c_new)
        else:
            mix, C_new, n_new, m_new = mlstm_mixer(
                hn, m_C[j], m_n[j], m_m[j],
                w['c_w_in'][j], w['c_b_i'][j], w['c_b_f'][j], w['c_b_o'][j],
                w['c_norm'][j], w['c_w_out'][j])
            outs['C'].append(C_new)
            outs['n'].append(n_new)
            outs['m'].append(m_new)
        x = x + rmsnorm(mix, w['norm_mix_post'][l])
        ff, f_new = conv_ffn(rmsnorm(x, w['norm_ffn_pre'][l]), f_conv[l],
                             w['ffn_w_up'][l], w['ffn_conv_w'][l], w['ffn_conv_b'][l], w['ffn_w_down'][l])
        outs['f'].append(f_new)
        x = x + rmsnorm(ff, w['norm_ffn_post'][l])
    st = {name: jnp.stack(vals) for name, vals in outs.items()}
    return x, st


def setup_inputs(seed: int = 0) -> dict:
    key = jax.random.key(seed)
    ks = iter(jax.random.split(key, 48))

    def nrm(shape, scale):
        return scale * jax.random.normal(next(ks), shape, jnp.float32)

    cw = min(WINDOW, PAST_LEN)
    a0 = jax.random.uniform(next(ks), (N_A_LAYERS, D_RNN), jnp.float32, minval=0.9, maxval=0.999)
    sa = a0 ** (1.0 / LRU_C)
    lam = jnp.log(sa) - jnp.log1p(-sa)
    b_f = jnp.linspace(3.0, 6.0, N_HEADS_C, dtype=jnp.float32)[None, :] + nrm((N_C_LAYERS, N_HEADS_C), 0.1)
    return {
        'x_prompt': nrm((BATCH, SEQ, D_MODEL), 1.0),
        'x_sample': nrm((DEC_BATCH, DEC_SEQ, D_MODEL), 1.0),
        'cache_k': nrm((N_A_LAYERS, DEC_BATCH, cw, N_KV_A, HD_A), 1.0),
        'cache_v': nrm((N_A_LAYERS, DEC_BATCH, cw, N_KV_A, HD_A), 1.0),
        'state_lru_h': nrm((N_A_LAYERS, DEC_BATCH, D_RNN), 0.5),
        'state_lru_conv': nrm((N_A_LAYERS, DEC_BATCH, CONV_B - 1, D_RNN), 1.0),
        'state_mlstm_C': nrm((N_C_LAYERS, DEC_BATCH, N_HEADS_C, DK_C, DV_C), 0.3),
        'state_mlstm_n': nrm((N_C_LAYERS, DEC_BATCH, N_HEADS_C, DK_C), 0.3),
        'state_mlstm_m': nrm((N_C_LAYERS, DEC_BATCH, N_HEADS_C), 0.5),
        'state_ffn_conv': nrm((DEPTH, DEC_BATCH, CONV_F - 1, 2 * D_FF), 1.0),
        'norm_mix_pre': 1.0 + nrm((DEPTH, D_MODEL), 0.01),
        'norm_mix_post': 1.0 + nrm((DEPTH, D_MODEL), 0.01),
        'norm_ffn_pre': 1.0 + nrm((DEPTH, D_MODEL), 0.01),
        'norm_ffn_post': 1.0 + nrm((DEPTH, D_MODEL), 0.01),
        'a_w_in': nrm((N_A_LAYERS, D_MODEL, IN_A), D_MODEL ** -0.5),
        'a_sinks': nrm((N_A_LAYERS, N_HEADS_A), 0.5),
        'a_conv_w': nrm((N_A_LAYERS, CONV_B, D_RNN), CONV_B ** -0.5),
        'a_conv_b': nrm((N_A_LAYERS, D_RNN), 0.01),
        'a_lru_wa': nrm((N_A_LAYERS, N_BLOCKS_B, BS_B, BS_B), BS_B ** -0.5),
        'a_lru_ba': nrm((N_A_LAYERS, D_RNN), 0.01),
        'a_lru_wx': nrm((N_A_LAYERS, N_BLOCKS_B, BS_B, BS_B), BS_B ** -0.5),
        'a_lru_bx': nrm((N_A_LAYERS, D_RNN), 0.01),
        'a_lru_lambda': lam,
        'a_w_out': nrm((N_A_LAYERS, QA + D_RNN, D_MODEL), (QA + D_RNN) ** -0.5),
        'c_w_in': nrm((N_C_LAYERS, D_MODEL, IN_C), D_MODEL ** -0.5),
        'c_b_i': nrm((N_C_LAYERS, N_HEADS_C), 0.1),
        'c_b_f': b_f,
        'c_b_o': nrm((N_C_LAYERS, VC), 0.1),
        'c_norm': 1.0 + nrm((N_C_LAYERS, VC), 0.01),
        'c_w_out': nrm((N_C_LAYERS, VC, D_MODEL), VC ** -0.5),
        'ffn_w_up': nrm((DEPTH, D_MODEL, 2 * D_FF), D_MODEL ** -0.5),
        'ffn_conv_w': nrm((DEPTH, CONV_F, 2 * D_FF), CONV_F ** -0.5),
        'ffn_conv_b': nrm((DEPTH, 2 * D_FF), 0.01),
        'ffn_w_down': nrm((DEPTH, D_FF, D_MODEL), D_FF ** -0.5),
    }


def reference(x_prompt, x_sample, cache_k, cache_v, state_lru_h, state_lru_conv, state_mlstm_C,
              state_mlstm_n, state_mlstm_m, state_ffn_conv, norm_mix_pre, norm_mix_post, norm_ffn_pre,
              norm_ffn_post, a_w_in, a_sinks, a_conv_w, a_conv_b, a_lru_wa, a_lru_ba, a_lru_wx, a_lru_bx,
              a_lru_lambda, a_w_out, c_w_in, c_b_i, c_b_f, c_b_o, c_norm, c_w_out, ffn_w_up, ffn_conv_w,
              ffn_conv_b, ffn_w_down):
    w = {
        'norm_mix_pre': norm_mix_pre, 'norm_mix_post': norm_mix_post,
        'norm_ffn_pre': norm_ffn_pre, 'norm_ffn_post': norm_ffn_post,
        'a_w_in': a_w_in, 'a_sinks': a_sinks, 'a_conv_w': a_conv_w, 'a_conv_b': a_conv_b,
        'a_lru_wa': a_lru_wa, 'a_lru_ba': a_lru_ba, 'a_lru_wx': a_lru_wx, 'a_lru_bx': a_lru_bx,
        'a_lru_lambda': a_lru_lambda, 'a_w_out': a_w_out,
        'c_w_in': c_w_in, 'c_b_i': c_b_i, 'c_b_f': c_b_f, 'c_b_o': c_b_o, 'c_norm': c_norm, 'c_w_out': c_w_out,
        'ffn_w_up': ffn_w_up, 'ffn_conv_w': ffn_conv_w, 'ffn_conv_b': ffn_conv_b, 'ffn_w_down': ffn_w_down,
    }
    dt = x_prompt.dtype
    Bp = x_prompt.shape[0]
    y_prompt, sp = run_trunk(
        x_prompt, 0, None, None,
        jnp.zeros((N_A_LAYERS, Bp, D_RNN), dt),
        jnp.zeros((N_A_LAYERS, Bp, CONV_B - 1, D_RNN), dt),
        jnp.zeros((N_C_LAYERS, Bp, N_HEADS_C, DK_C, DV_C), dt),
        jnp.zeros((N_C_LAYERS, Bp, N_HEADS_C, DK_C), dt),
        jnp.zeros((N_C_LAYERS, Bp, N_HEADS_C), dt),
        jnp.zeros((DEPTH, Bp, CONV_F - 1, 2 * D_FF), dt),
        w)
    y_sample, ss = run_trunk(
        x_sample, PAST_LEN, cache_k, cache_v, state_lru_h, state_lru_conv,
        state_mlstm_C, state_mlstm_n, state_mlstm_m, state_ffn_conv, w)
    return (y_prompt, y_sample, sp['k'], ss['k'], sp['v'], ss['v'], sp['h'], ss['h'],
            sp['conv'], ss['conv'], sp['C'], ss['C'], sp['n'], ss['n'], sp['m'], ss['m'],
            sp['f'], ss['f'])
```

```python
import functools
import math

import jax
import jax.numpy as jnp
from jax import lax
from jax.experimental import pallas as pl
from jax.experimental.pallas import tpu as pltpu

F32 = jnp.float32
BF16 = jnp.bfloat16

D_MODEL = 2048
DEPTH = 4
PAST_LEN = 16384
N_HEADS_A = 16
N_KV_A = 4
HD_A = 64
GROUP_A = N_HEADS_A // N_KV_A
WINDOW = 128
BLOCK_A = 128
ROPE_THETA = 10000.0
D_RNN = 1024
N_BLOCKS_B = 16
BS_B = D_RNN // N_BLOCKS_B
CONV_B = 4
LRU_C = 8.0
N_HEADS_C = 8
DK_C = 128
DV_C = 256
D_FF = 6144
CONV_F = 3
EPS = 1e-6
QA = N_HEADS_A * HD_A
KA = N_KV_A * HD_A
IN_A = QA + 2 * KA + 2 * D_RNN
QC = N_HEADS_C * DK_C
VC = N_HEADS_C * DV_C
IN_C_MAIN = 2 * QC + 2 * VC

LANES = 128
SUBLANES = 8
VMEM_LIMIT_BYTES = 56 * 1024 * 1024

TM = 512
TN_PROJ = 512
TF = 512
LRU_TT = 256
MLSTM_L = 256
LRU_GROUP = 256
GATE_PAD = LANES


def _cparams(n_axes):
    return pltpu.CompilerParams(dimension_semantics=("arbitrary",) * n_axes,
                                vmem_limit_bytes=VMEM_LIMIT_BYTES)


def _rms(x, g):
    ms = jnp.mean(x * x, axis=-1, keepdims=True)
    return x * lax.rsqrt(ms + EPS) * g


def _gelu_tanh(x):
    c = math.sqrt(2.0 / math.pi)
    return 0.5 * x * (1.0 + jnp.tanh(c * (x + 0.044715 * (x * x * x))))


def _sigmoid(x):
    return 1.0 / (1.0 + jnp.exp(-x))


def _log_sigmoid(x):
    return jnp.minimum(x, 0.0) - jnp.log1p(jnp.exp(-jnp.abs(x)))


def _norm_matmul_kernel(x_ref, g_ref, w_ref, o_ref, hn_ref):
    @pl.when(pl.program_id(1) == 0)
    def _():
        hn_ref[...] = _rms(x_ref[...], g_ref[...]).astype(BF16)

    o_ref[...] = jnp.dot(hn_ref[...], w_ref[...], preferred_element_type=F32)


def norm_matmul(x, g, w, tn):
    m, k = x.shape
    n = w.shape[1]
    tm = min(TM, m)
    return pl.pallas_call(
        _norm_matmul_kernel,
        grid=(m // tm, n // tn),
        in_specs=[pl.BlockSpec((tm, k), lambda i, j: (i, 0)),
                  pl.BlockSpec((1, k), lambda i, j: (0, 0)),
                  pl.BlockSpec((k, tn), lambda i, j: (0, j))],
        out_specs=pl.BlockSpec((tm, tn), lambda i, j: (i, j)),
        out_shape=jax.ShapeDtypeStruct((m, n), F32),
        scratch_shapes=[pltpu.VMEM((tm, k), BF16)],
        compiler_params=_cparams(2),
        name="norm_matmul",
    )(x, g.reshape(1, k), w)


def _out_proj_kernel(*refs, n_in):
    a_refs = refs[:n_in]
    w_ref, x_ref, g_ref, o_ref = refs[n_in:]
    acc = None
    row = 0
    for a_ref in a_refs:
        ka = a_ref.shape[1]
        part = jnp.dot(a_ref[...], w_ref[row:row + ka, :], preferred_element_type=F32)
        acc = part if acc is None else acc + part
        row += ka
    o_ref[...] = x_ref[...] + _rms(acc, g_ref[...])


def out_proj(inputs, w, x, g):
    m, d = x.shape
    tm = min(TM, m)
    in_specs = [pl.BlockSpec((tm, a.shape[1]), lambda i: (i, 0)) for a in inputs]
    in_specs += [pl.BlockSpec(w.shape, lambda i: (0, 0)),
                 pl.BlockSpec((tm, d), lambda i: (i, 0)),
                 pl.BlockSpec((1, d), lambda i: (0, 0))]
    return pl.pallas_call(
        functools.partial(_out_proj_kernel, n_in=len(inputs)),
        grid=(m // tm,),
        in_specs=in_specs,
        out_specs=pl.BlockSpec((tm, d), lambda i: (i, 0)),
        out_shape=jax.ShapeDtypeStruct((m, d), F32),
        compiler_params=_cparams(1),
        name="out_proj",
    )(*inputs, w, x, g.reshape(1, d))


def _ffn_conv(u, prev_ref, cw_ref, cb_ref, ubuf_ref):
    tm = u.shape[0]
    ubuf_ref[0:SUBLANES, :] = prev_ref[...]
    ubuf_ref[SUBLANES:SUBLANES + tm, :] = u
    cw = cw_ref[...]
    y = ubuf_ref[SUBLANES - 2:SUBLANES - 2 + tm, :] * cw[0:1, :]
    y = y + ubuf_ref[SUBLANES - 1:SUBLANES - 1 + tm, :] * cw[1:2, :]
    y = y + u * cw[2:3, :]
    return y + cb_ref[...]


def _ffn_seq_kernel(x_ref, gpre_ref, wg_ref, wv_ref, cwg_ref, cwv_ref, cbg_ref, cbv_ref, wd_ref, gpost_ref,
                    o_ref, ng_ref, nv_ref,
                    hn_ref, acc_ref, carry_ref, ubuf_ref, *, tiles_per_seq):
    i = pl.program_id(0)
    f = pl.program_id(1)
    nf = pl.num_programs(1)
    tm = x_ref.shape[0]

    @pl.when(f == 0)
    def _():
        hn_ref[...] = _rms(x_ref[...], gpre_ref[...]).astype(BF16)
        acc_ref[...] = jnp.zeros_like(acc_ref)

    @pl.when(i % tiles_per_seq == 0)
    def _():
        carry_ref[f] = jnp.zeros(carry_ref.shape[1:], F32)

    hn = hn_ref[...]
    ug = jnp.dot(hn, wg_ref[...], preferred_element_type=F32)
    uv = jnp.dot(hn, wv_ref[...], preferred_element_type=F32)
    cg = _ffn_conv(ug, carry_ref.at[f, 0], cwg_ref, cbg_ref, ubuf_ref.at[0])
    cv = _ffn_conv(uv, carry_ref.at[f, 1], cwv_ref, cbv_ref, ubuf_ref.at[1])
    carry_ref[f, 0] = ug[tm - SUBLANES:tm, :]
    carry_ref[f, 1] = uv[tm - SUBLANES:tm, :]
    ng_ref[...] = ug[tm - (CONV_F - 1):tm, :]
    nv_ref[...] = uv[tm - (CONV_F - 1):tm, :]
    act = (_gelu_tanh(cg) * cv).astype(BF16)
    acc_ref[...] += jnp.dot(act, wd_ref[...], preferred_element_type=F32)

    @pl.when(f == nf - 1)
    def _():
        o_ref[...] = x_ref[...] + _rms(acc_ref[...], gpost_ref[...])


def ffn_seq(x, seq_len, gpre, wup, convw, convb, wdown, gpost):
    m, d = x.shape
    tm = min(TM, seq_len)
    tps = seq_len // tm
    nb = m // seq_len
    nf = D_FF // TF
    kern = functools.partial(_ffn_seq_kernel, tiles_per_seq=tps)
    y, ng, nv = pl.pallas_call(
        kern,
        grid=(m // tm, nf),
        in_specs=[pl.BlockSpec((tm, d), lambda i, f: (i, 0)),
                  pl.BlockSpec((1, d), lambda i, f: (0, 0)),
                  pl.BlockSpec((d, TF), lambda i, f: (0, f)),
                  pl.BlockSpec((d, TF), lambda i, f: (0, f + nf)),
                  pl.BlockSpec((CONV_F, TF), lambda i, f: (0, f)),
                  pl.BlockSpec((CONV_F, TF), lambda i, f: (0, f + nf)),
                  pl.BlockSpec((1, TF), lambda i, f: (0, f)),
                  pl.BlockSpec((1, TF), lambda i, f: (0, f + nf)),
                  pl.BlockSpec((TF, d), lambda i, f: (f, 0)),
                  pl.BlockSpec((1, d), lambda i, f: (0, 0))],
        out_specs=[pl.BlockSpec((tm, d), lambda i, f: (i, 0)),
                   pl.BlockSpec((None, CONV_F - 1, TF), lambda i, f: (i, 0, f)),
                   pl.BlockSpec((None, CONV_F - 1, TF), lambda i, f: (i, 0, f))],
        out_shape=[jax.ShapeDtypeStruct((m, d), F32),
                   jax.ShapeDtypeStruct((m // tm, CONV_F - 1, D_FF), F32),
                   jax.ShapeDtypeStruct((m // tm, CONV_F - 1, D_FF), F32)],
        scratch_shapes=[pltpu.VMEM((tm, d), BF16),
                        pltpu.VMEM((tm, d), F32),
                        pltpu.VMEM((nf, 2, SUBLANES, TF), F32),
                        pltpu.VMEM((2, tm + SUBLANES, TF), F32)],
        compiler_params=_cparams(2),
        name="ffn_seq",
    )(x, gpre.reshape(1, d), wup, wup, convw, convw, convb.reshape(1, -1), convb.reshape(1, -1),
      wdown, gpost.reshape(1, d))
    last = jnp.concatenate([ng, nv], axis=-1).reshape(nb, tps, CONV_F - 1, 2 * D_FF)
    return y, last[:, tps - 1]


def _ffn_step_kernel(x_ref, gpre_ref, wg_ref, wv_ref, cwg_ref, cwv_ref, cbg_ref, cbv_ref, wd_ref, gpost_ref,
                     p0g_ref, p0v_ref, p1g_ref, p1v_ref,
                     o_ref, ug_ref, uv_ref, hn_ref, acc_ref):
    f = pl.program_id(0)
    nf = pl.num_programs(0)

    @pl.when(f == 0)
    def _():
        hn_ref[...] = _rms(x_ref[...], gpre_ref[...]).astype(BF16)
        acc_ref[...] = jnp.zeros_like(acc_ref)

    hn = hn_ref[...]
    ug = jnp.dot(hn, wg_ref[...], preferred_element_type=F32)
    uv = jnp.dot(hn, wv_ref[...], preferred_element_type=F32)
    ug_ref[...] = ug
    uv_ref[...] = uv
    cwg = cwg_ref[...]
    cwv = cwv_ref[...]
    cg = p0g_ref[...] * cwg[0:1, :] + p1g_ref[...] * cwg[1:2, :] + ug * cwg[2:3, :] + cbg_ref[...]
    cv = p0v_ref[...] * cwv[0:1, :] + p1v_ref[...] * cwv[1:2, :] + uv * cwv[2:3, :] + cbv_ref[...]
    act = (_gelu_tanh(cg) * cv).astype(BF16)
    acc_ref[...] += jnp.dot(act, wd_ref[...], preferred_element_type=F32)

    @pl.when(f == nf - 1)
    def _():
        o_ref[...] = x_ref[...] + _rms(acc_ref[...], gpost_ref[...])


def ffn_step(x, past, gpre, wup, convw, convb, wdown, gpost):
    n, d = x.shape
    nf = D_FF // TF
    past2 = past.reshape(n, (CONV_F - 1) * 2 * D_FF)
    y, ug, uv = pl.pallas_call(
        _ffn_step_kernel,
        grid=(nf,),
        in_specs=[pl.BlockSpec((n, d), lambda f: (0, 0)),
                  pl.BlockSpec((1, d), lambda f: (0, 0)),
                  pl.BlockSpec((d, TF), lambda f: (0, f)),
                  pl.BlockSpec((d, TF), lambda f: (0, f + nf)),
                  pl.BlockSpec((CONV_F, TF), lambda f: (0, f)),
                  pl.BlockSpec((CONV_F, TF), lambda f: (0, f + nf)),
                  pl.BlockSpec((1, TF), lambda f: (0, f)),
                  pl.BlockSpec((1, TF), lambda f: (0, f + nf)),
                  pl.BlockSpec((TF, d), lambda f: (f, 0)),
                  pl.BlockSpec((1, d), lambda f: (0, 0)),
                  pl.BlockSpec((n, TF), lambda f: (0, f)),
                  pl.BlockSpec((n, TF), lambda f: (0, f + nf)),
                  pl.BlockSpec((n, TF), lambda f: (0, f + 2 * nf)),
                  pl.BlockSpec((n, TF), lambda f: (0, f + 3 * nf))],
        out_specs=[pl.BlockSpec((n, d), lambda f: (0, 0)),
                   pl.BlockSpec((n, TF), lambda f: (0, f)),
                   pl.BlockSpec((n, TF), lambda f: (0, f))],
        out_shape=[jax.ShapeDtypeStruct((n, d), F32),
                   jax.ShapeDtypeStruct((n, D_FF), F32),
                   jax.ShapeDtypeStruct((n, D_FF), F32)],
        scratch_shapes=[pltpu.VMEM((n, d), BF16), pltpu.VMEM((n, d), F32)],
        compiler_params=_cparams(1),
        name="ffn_step",
    )(x, gpre.reshape(1, d), wup, wup, convw, convw, convb.reshape(1, -1), convb.reshape(1, -1),
      wdown, gpost.reshape(1, d), past2, past2, past2, past2)
    u_new = jnp.concatenate([ug, uv], axis=-1)
    return y, jnp.concatenate([past[:, 1:], u_new[:, None, :]], axis=1)


def _rope_tables(pos0, t):
    half = HD_A // 2
    pos = (pos0 + jnp.arange(t)).astype(F32)
    inv = ROPE_THETA ** (-jnp.arange(half, dtype=F32) / half)
    ang = pos[:, None] * inv[None, :]
    cos, sin = jnp.cos(ang), jnp.sin(ang)
    zero = jnp.zeros_like(sin)
    cos_h = jnp.concatenate([cos, cos], axis=-1)
    sin_up = jnp.concatenate([-sin, zero], axis=-1)
    sin_dn = jnp.concatenate([zero, sin], axis=-1)
    two = lambda a: jnp.concatenate([a, a], axis=-1)
    return two(cos_h), two(sin_up), two(sin_dn)


def _rope(x, cos, sin_up, sin_dn):
    w = x.shape[1]
    reps = w // LANES
    tile = lambda a: jnp.concatenate([a] * reps, axis=1) if reps > 1 else a
    x_up = pltpu.roll(x, w - HD_A // 2, axis=1)
    x_dn = pltpu.roll(x, HD_A // 2, axis=1)
    return x * tile(cos) + x_up * tile(sin_up) + x_dn * tile(sin_dn)


def _dup_half(a, half):
    lane = lax.broadcasted_iota(jnp.int32, a.shape, 1)
    sw = pltpu.roll(a, HD_A, axis=1)
    lo = lane < HD_A
    return jnp.where(lo, a, sw) if half == 0 else jnp.where(lo, sw, a)


def _sink_softmax(s, sk_rows):
    m = jnp.maximum(jnp.max(s, axis=-1, keepdims=True), sk_rows)
    p = jnp.exp(s - m)
    return p, jnp.sum(p, axis=-1, keepdims=True) + jnp.exp(sk_rows - m)


def _swa_prompt_kernel(sinks_ref, q_ref, kc_ref, kp_ref, vc_ref, vp_ref,
                       cc_ref, suc_ref, sdc_ref, cp_ref, sup_ref, sdp_ref,
                       att_ref, knew_ref, vnew_ref):
    c = pl.program_id(1)
    nb = pl.num_programs(1)
    bq = q_ref.shape[0]
    q = _rope(q_ref[...], cc_ref[...], suc_ref[...], sdc_ref[...])
    kc = _rope(kc_ref[...], cc_ref[...], suc_ref[...], sdc_ref[...])
    kp = _rope(kp_ref[...], cp_ref[...], sup_ref[...], sdp_ref[...])
    vc = vc_ref[...]
    vp = vp_ref[...]

    @pl.when(c == nb - 1)
    def _():
        knew_ref[...] = kc
        vnew_ref[...] = vc

    rows = GROUP_A * bq
    r = lax.broadcasted_iota(jnp.int32, (rows, 2 * bq), 0) % bq
    col = lax.broadcasted_iota(jnp.int32, (rows, 2 * bq), 1)
    valid = (col >= r) & (col <= r + WINDOW) & ((c > 0) | (col >= bq))
    lane = lax.broadcasted_iota(jnp.int32, (bq, LANES), 1)
    lo = lane < HD_A
    row_head = lax.broadcasted_iota(jnp.int32, (rows, 1), 0) // bq

    for h in range(N_KV_A):
        t, half = divmod(h, 2)
        ksl = slice(t * LANES, (t + 1) * LANES)
        kd = _dup_half(jnp.concatenate([kp[:, ksl], kc[:, ksl]], axis=0), half).astype(BF16)
        vd = _dup_half(jnp.concatenate([vp[:, ksl], vc[:, ksl]], axis=0), half).astype(BF16)
        parts = []
        for p in range(2):
            q2 = q[:, (2 * h + p) * LANES:(2 * h + p + 1) * LANES]
            parts.append(jnp.where(lo, q2, 0.0))
            parts.append(jnp.where(lo, 0.0, q2))
        qs = jnp.concatenate(parts, axis=0).astype(BF16)
        s = lax.dot_general(qs, kd, (((1,), (1,)), ((), ())), preferred_element_type=F32) * (HD_A ** -0.5)
        s = jnp.where(valid, s, -jnp.inf)
        sk = jnp.zeros((rows, 1), F32)
        for g in range(GROUP_A):
            sk = jnp.where(row_head == g, sinks_ref[h * GROUP_A + g], sk)
        pr, den = _sink_softmax(s, sk)
        pr = pr / den
        o = jnp.dot(pr.astype(BF16), vd, preferred_element_type=F32)
        for p in range(2):
            o_lo = o[(2 * p) * bq:(2 * p + 1) * bq, :]
            o_hi = o[(2 * p + 1) * bq:(2 * p + 2) * bq, :]
            att_ref[:, (2 * h + p) * LANES:(2 * h + p + 1) * LANES] = jnp.where(lo, o_lo, o_hi).astype(BF16)


def swa_prompt(proj, sinks, nbatch, seq_len):
    nb = seq_len // BLOCK_A
    p3 = proj.reshape(nbatch, seq_len, IN_A)
    kcol = (QA + 2 * D_RNN) // KA
    cos, sup, sdn = _rope_tables(0, seq_len)
    prev = lambda c: jnp.maximum(c - 1, 0)
    tab_c = pl.BlockSpec((BLOCK_A, LANES), lambda b, c: (c, 0))
    tab_p = pl.BlockSpec((BLOCK_A, LANES), lambda b, c: (prev(c), 0))
    att, knew, vnew = pl.pallas_call(
        _swa_prompt_kernel,
        grid=(nbatch, nb),
        in_specs=[pl.BlockSpec(memory_space=pltpu.SMEM),
                  pl.BlockSpec((None, BLOCK_A, QA), lambda b, c: (b, c, 0)),
                  pl.BlockSpec((None, BLOCK_A, KA), lambda b, c: (b, c, kcol)),
                  pl.BlockSpec((None, BLOCK_A, KA), lambda b, c: (b, prev(c), kcol)),
                  pl.BlockSpec((None, BLOCK_A, KA), lambda b, c: (b, c, kcol + 1)),
                  pl.BlockSpec((None, BLOCK_A, KA), lambda b, c: (b, prev(c), kcol + 1)),
                  tab_c, tab_c, tab_c, tab_p, tab_p, tab_p],
        out_specs=[pl.BlockSpec((None, BLOCK_A, QA), lambda b, c: (b, c, 0)),
                   pl.BlockSpec((None, BLOCK_A, KA), lambda b, c: (b, 0, 0)),
                   pl.BlockSpec((None, BLOCK_A, KA), lambda b, c: (b, 0, 0))],
        out_shape=[jax.ShapeDtypeStruct((nbatch, seq_len, QA), BF16),
                   jax.ShapeDtypeStruct((nbatch, BLOCK_A, KA), F32),
                   jax.ShapeDtypeStruct((nbatch, BLOCK_A, KA), F32)],
        compiler_params=_cparams(2),
        name="swa_prompt",
    )(sinks, p3, p3, p3, p3, p3, cos, sup, sdn, cos, sup, sdn)
    return att.reshape(nbatch * seq_len, QA), knew, vnew


def _swa_sample_kernel(sinks_ref, q_ref, k_ref, v_ref, kc_ref, vc_ref, cos_ref, sup_ref, sdn_ref,
                       att_ref, knew_ref, vnew_ref):
    w = kc_ref.shape[0]
    q = _rope(q_ref[...], cos_ref[...], sup_ref[...], sdn_ref[...])
    k = _rope(k_ref[...], cos_ref[...], sup_ref[...], sdn_ref[...])
    v = v_ref[...]
    kcache = kc_ref[...]
    vcache = vc_ref[...]
    knew_ref[0:w - 1, :] = kc_ref[1:w, :]
    knew_ref[w - 1:w, :] = k
    vnew_ref[0:w - 1, :] = vc_ref[1:w, :]
    vnew_ref[w - 1:w, :] = v

    lane = lax.broadcasted_iota(jnp.int32, (1, LANES), 1)
    lo = lane < HD_A
    row = lax.broadcasted_iota(jnp.int32, (SUBLANES, 1), 0)
    for h in range(N_KV_A):
        t, half = divmod(h, 2)
        ksl = slice(t * LANES, (t + 1) * LANES)
        kd = _dup_half(kcache[:, ksl], half)
        vd = _dup_half(vcache[:, ksl], half)
        kd_new = _dup_half(k[:, ksl], half)
        vd_new = _dup_half(v[:, ksl], half)
        parts = []
        for p in range(2):
            q2 = q[:, (2 * h + p) * LANES:(2 * h + p + 1) * LANES]
            parts.append(jnp.where(lo, q2, 0.0))
            parts.append(jnp.where(lo, 0.0, q2))
        parts.append(jnp.zeros((SUBLANES - GROUP_A, LANES), F32))
        qs = jnp.concatenate(parts, axis=0)
        s = lax.dot_general(qs.astype(BF16), kd.astype(BF16), (((1,), (1,)), ((), ())),
                            preferred_element_type=F32) * (HD_A ** -0.5)
        s_new = jnp.sum(qs * kd_new, axis=-1, keepdims=True) * (HD_A ** -0.5)
        sk = jnp.zeros((SUBLANES, 1), F32)
        for g in range(GROUP_A):
            sk = jnp.where(row == g, sinks_ref[h * GROUP_A + g], sk)
        m = jnp.maximum(jnp.maximum(jnp.max(s, axis=-1, keepdims=True), s_new), sk)
        pr = jnp.exp(s - m)
        p_new = jnp.exp(s_new - m)
        den = jnp.sum(pr, axis=-1, keepdims=True) + p_new + jnp.exp(sk - m)
        pr = pr / den
        p_new = p_new / den
        o = jnp.dot(pr.astype(BF16), vd.astype(BF16), preferred_element_type=F32) + p_new * vd_new
        for p in range(2):
            att_ref[:, (2 * h + p) * LANES:(2 * h + p + 1) * LANES] = jnp.where(
                lo, o[2 * p:2 * p + 1, :], o[2 * p + 1:2 * p + 2, :]).astype(BF16)


def swa_sample(proj, cache_k, cache_v, sinks):
    n = proj.shape[0]
    w = cache_k.shape[1]
    p3 = proj.reshape(n, 1, IN_A)
    kcol = (QA + 2 * D_RNN) // KA
    ck = cache_k.reshape(n, w, KA)
    cv = cache_v.reshape(n, w, KA)
    cos, sup, sdn = _rope_tables(PAST_LEN, 1)
    tab = pl.BlockSpec((1, LANES), lambda z: (0, 0))
    att, knew, vnew = pl.pallas_call(
        _swa_sample_kernel,
        grid=(n,),
        in_specs=[pl.BlockSpec(memory_space=pltpu.SMEM),
                  pl.BlockSpec((None, 1, QA), lambda z: (z, 0, 0)),
                  pl.BlockSpec((None, 1, KA), lambda z: (z, 0, kcol)),
                  pl.BlockSpec((None, 1, KA), lambda z: (z, 0, kcol + 1)),
                  pl.BlockSpec((None, w, KA), lambda z: (z, 0, 0)),
                  pl.BlockSpec((None, w, KA), lambda z: (z, 0, 0)),
                  tab, tab, tab],
        out_specs=[pl.BlockSpec((None, 1, QA), lambda z: (z, 0, 0)),
                   pl.BlockSpec((None, w, KA), lambda z: (z, 0, 0)),
                   pl.BlockSpec((None, w, KA), lambda z: (z, 0, 0))],
        out_shape=[jax.ShapeDtypeStruct((n, 1, QA), BF16),
                   jax.ShapeDtypeStruct((n, w, KA), F32),
                   jax.ShapeDtypeStruct((n, w, KA), F32)],
        compiler_params=_cparams(1),
        name="swa_sample",
    )(sinks, p3, p3, p3, ck, cv, cos, sup, sdn)
    return att.reshape(n, QA), knew, vnew


def _lru_gates(xc, wa_ref, wx_ref, ba_ref, bx_ref, lam_ref):
    rs, is_ = [], []
    for c in range(D_RNN // LRU_GROUP):
        xg = xc[:, c * LRU_GROUP:(c + 1) * LRU_GROUP].astype(BF16)
        rs.append(jnp.dot(xg, wa_ref[c], preferred_element_type=F32))
        is_.append(jnp.dot(xg, wx_ref[c], preferred_element_type=F32))
    r = _sigmoid(jnp.concatenate(rs, axis=1) + ba_ref[...])
    i = _sigmoid(jnp.concatenate(is_, axis=1) + bx_ref[...])
    log_a = LRU_C * r * _log_sigmoid(lam_ref[...])
    a = jnp.exp(log_a)
    b = jnp.sqrt(-jnp.tanh(log_a) * (a * a + 1.0)) * (i * xc)
    return a, b


def _lru_seq_kernel(xr_ref, gr_ref, cw_ref, cb_ref, wa_ref, wx_ref, ba_ref, bx_ref, lam_ref,
                    rec_ref, hlast_ref, cnew_ref,
                    xbuf_ref, a_ref, b_ref, h_ref, hc_ref):
    t = pl.program_id(1)
    tt = xr_ref.shape[0]

    @pl.when(t == 0)
    def _():
        xbuf_ref[0:SUBLANES, :] = jnp.zeros((SUBLANES, D_RNN), F32)
        hc_ref[...] = jnp.zeros_like(hc_ref)

    xr = xr_ref[...]
    xbuf_ref[SUBLANES:SUBLANES + tt, :] = xr
    cw = cw_ref[...]
    xc = xr * cw[CONV_B - 1:CONV_B, :] + cb_ref[...]
    for j in range(CONV_B - 1):
        off = SUBLANES - (CONV_B - 1) + j
        xc = xc + xbuf_ref[off:off + tt, :] * cw[j:j + 1, :]
    xbuf_ref[0:SUBLANES, :] = xr[tt - SUBLANES:tt, :]
    cnew_ref[...] = xr[tt - (CONV_B - 1):tt, :]

    a, b = _lru_gates(xc, wa_ref, wx_ref, ba_ref, bx_ref, lam_ref)
    a_ref[...] = a
    b_ref[...] = b
    row = lax.broadcasted_iota(jnp.int32, (SUBLANES, D_RNN), 0)

    def body(g, h):
        base = pl.multiple_of(g * SUBLANES, SUBLANES)
        a8 = a_ref[pl.ds(base, SUBLANES), :]
        b8 = b_ref[pl.ds(base, SUBLANES), :]
        for s in (1, 2, 4):
            a_sh = pltpu.roll(a8, s, axis=0)
            b_sh = pltpu.roll(b8, s, axis=0)
            keep = row >= s
            b8 = jnp.where(keep, a8 * b_sh + b8, b8)
            a8 = jnp.where(keep, a8 * a_sh, a8)
        h8 = b8 + a8 * h
        h_ref[pl.ds(base, SUBLANES), :] = h8
        return h8[SUBLANES - 1:SUBLANES, :]

    h_last = lax.fori_loop(0, tt // SUBLANES, body, hc_ref[...])
    hc_ref[...] = h_last
    hlast_ref[...] = h_last
    rec_ref[...] = (h_ref[...] * _gelu_tanh(gr_ref[...])).astype(BF16)


def _lru_blockdiag(w):
    per = LRU_GROUP // BS_B
    w4 = w.reshape(D_RNN // LRU_GROUP, per, BS_B, BS_B)
    eye = jnp.eye(per, dtype=w.dtype)
    return jnp.einsum('cpij,pq->cpiqj', w4, eye).reshape(D_RNN // LRU_GROUP, LRU_GROUP, LRU_GROUP).astype(BF16)


def lru_seq(proj, nbatch, seq_len, cw, cb, wa_bd, wx_bd, ba, bx, lam):
    p3 = proj.reshape(nbatch, seq_len, IN_A)
    tt = min(LRU_TT, seq_len)
    vec = lambda: pl.BlockSpec((1, D_RNN), lambda b, t: (0, 0))
    wspec = lambda: pl.BlockSpec(wa_bd.shape, lambda b, t: (0, 0, 0))
    rec, hlast, cnew = pl.pallas_call(
        _lru_seq_kernel,
        grid=(nbatch, seq_len // tt),
        in_specs=[pl.BlockSpec((None, tt, D_RNN), lambda b, t: (b, t, QA // D_RNN)),
                  pl.BlockSpec((None, tt, D_RNN), lambda b, t: (b, t, QA // D_RNN + 1)),
                  pl.BlockSpec((CONV_B, D_RNN), lambda b, t: (0, 0)),
                  vec(), wspec(), wspec(), vec(), vec(), vec()],
        out_specs=[pl.BlockSpec((None, tt, D_RNN), lambda b, t: (b, t, 0)),
                   pl.BlockSpec((None, 1, D_RNN), lambda b, t: (b, 0, 0)),
                   pl.BlockSpec((None, CONV_B - 1, D_RNN), lambda b, t: (b, 0, 0))],
        out_shape=[jax.ShapeDtypeStruct((nbatch, seq_len, D_RNN), BF16),
                   jax.ShapeDtypeStruct((nbatch, 1, D_RNN), F32),
                   jax.ShapeDtypeStruct((nbatch, CONV_B - 1, D_RNN), F32)],
        scratch_shapes=[pltpu.VMEM((tt + SUBLANES, D_RNN), F32),
                        pltpu.VMEM((tt, D_RNN), F32),
                        pltpu.VMEM((tt, D_RNN), F32),
                        pltpu.VMEM((tt, D_RNN), F32),
                        pltpu.VMEM((1, D_RNN), F32)],
        compiler_params=_cparams(2),
        name="lru_seq",
    )(p3, p3, cw, cb.reshape(1, -1), wa_bd, wx_bd, ba.reshape(1, -1), bx.reshape(1, -1), lam.reshape(1, -1))
    return rec.reshape(nbatch * seq_len, D_RNN), hlast.reshape(nbatch, D_RNN), cnew


def _lru_step_kernel(xr_ref, gr_ref, p0_ref, p1_ref, p2_ref, h0_ref, cw_ref, cb_ref,
                     wa_ref, wx_ref, ba_ref, bx_ref, lam_ref, rec_ref, h_ref):
    cw = cw_ref[...]
    xr = xr_ref[...]
    xc = (p0_ref[...] * cw[0:1, :] + p1_ref[...] * cw[1:2, :] + p2_ref[...] * cw[2:3, :]
          + xr * cw[3:4, :] + cb_ref[...])
    a, b = _lru_gates(xc, wa_ref, wx_ref, ba_ref, bx_ref, lam_ref)
    h = b + a * h0_ref[...]
    h_ref[...] = h
    rec_ref[...] = (h * _gelu_tanh(gr_ref[...])).astype(BF16)


def lru_step(proj, h0, conv0, cw, cb, wa_bd, wx_bd, ba, bx, lam):
    n = proj.shape[0]
    past = conv0.reshape(n, (CONV_B - 1) * D_RNN)
    col = lambda j: pl.BlockSpec((n, D_RNN), lambda i: (0, j))
    vec = lambda: pl.BlockSpec((1, D_RNN), lambda i: (0, 0))
    wspec = lambda: pl.BlockSpec(wa_bd.shape, lambda i: (0, 0, 0))
    rec, h = pl.pallas_call(
        _lru_step_kernel,
        grid=(1,),
        in_specs=[col(QA // D_RNN), col(QA // D_RNN + 1), col(0), col(1), col(2), col(0),
                  pl.BlockSpec((CONV_B, D_RNN), lambda i: (0, 0)),
                  vec(), wspec(), wspec(), vec(), vec(), vec()],
        out_specs=[col(0), col(0)],
        out_shape=[jax.ShapeDtypeStruct((n, D_RNN), BF16), jax.ShapeDtypeStruct((n, D_RNN), F32)],
        compiler_params=_cparams(1),
        name="lru_step",
    )(proj, proj, past, past, past, h0, cw, cb.reshape(1, -1), wa_bd, wx_bd,
      ba.reshape(1, -1), bx.reshape(1, -1), lam.reshape(1, -1))
    xr = proj[:, QA:QA + D_RNN]
    conv_new = jnp.concatenate([conv0[:, 1:], xr[:, None, :]], axis=1)
    return rec, h, conv_new


def _cumsum_lanes(x):
    n = x.shape[1]
    lane = lax.broadcasted_iota(jnp.int32, x.shape, 1)
    s = 1
    while s < n:
        x = x + jnp.where(lane >= s, pltpu.roll(x, s, axis=1), 0.0)
        s *= 2
    return x


def _mlstm_seq_kernel(q_ref, k_ref, v_ref, o_ref, gate_ref, gbias_ref, bo_ref, gn_ref,
                      out_ref, c_ref, n_ref, m_ref):
    ci = pl.program_id(1)
    ln = q_ref.shape[0]

    @pl.when(ci == 0)
    def _():
        c_ref[...] = jnp.zeros_like(c_ref)
        n_ref[...] = jnp.zeros_like(n_ref)
        m_ref[...] = jnp.zeros_like(m_ref)

    gate = gate_ref[...] + gbias_ref[...]
    gate_t = gate.T
    b_t = _cumsum_lanes(_log_sigmoid(gate_t))
    b_cols = b_t.T
    ri = lax.broadcasted_iota(jnp.int32, (ln, ln), 0)
    cj = lax.broadcasted_iota(jnp.int32, (ln, ln), 1)
    causal = cj <= ri

    for h in range(N_HEADS_C):
        fcol = N_HEADS_C + h
        b_col = b_cols[:, fcol:fcol + 1]
        b_row = b_t[fcol:fcol + 1, :]
        i_col = gate[:, h:h + 1]
        i_row = gate_t[h:h + 1, :]
        m_prev = m_ref[h:h + 1, 0:1]
        qh = q_ref[:, h * DK_C:(h + 1) * DK_C]
        kh = k_ref[:, h * DK_C:(h + 1) * DK_C] * (DK_C ** -0.5)
        vh = v_ref[:, h * DV_C:(h + 1) * DV_C].astype(BF16)
        qb = qh.astype(BF16)

        g_col = b_col + m_prev
        dmat = jnp.where(causal, b_col - b_row + i_row, -jnp.inf)
        mt = jnp.maximum(g_col, jnp.max(dmat, axis=1, keepdims=True))
        inter = jnp.exp(g_col - mt)
        s = lax.dot_general(qb, kh.astype(BF16), (((1,), (1,)), ((), ())), preferred_element_type=F32)
        amat = jnp.exp(dmat - mt) * s
        c_h = c_ref[h]
        n_h = n_ref[h:h + 1, :]
        num = inter * jnp.dot(qb, c_h.astype(BF16), preferred_element_type=F32) \
            + jnp.dot(amat.astype(BF16), vh, preferred_element_type=F32)
        den = inter * jnp.sum(qh * n_h, axis=1, keepdims=True) + jnp.sum(amat, axis=1, keepdims=True)
        hh = num / jnp.maximum(jnp.abs(den), jnp.exp(-mt))

        m_last = mt[ln - 1:ln, :]
        decay = jnp.exp(g_col[ln - 1:ln, :] - m_last)
        w_col = jnp.exp(b_col[ln - 1:ln, :] - b_col + i_col - m_last)
        kw = kh * w_col
        c_ref[h] = decay * c_h + lax.dot_general(kw.astype(BF16), vh, (((0,), (0,)), ((), ())),
                                                 preferred_element_type=F32)
        n_ref[h:h + 1, :] = decay * n_h + jnp.sum(kw, axis=0, keepdims=True)
        m_ref[h:h + 1, :] = jnp.broadcast_to(m_last, (1, m_ref.shape[1]))

        vsl = slice(h * DV_C, (h + 1) * DV_C)
        hn = _rms(hh, gn_ref[:, vsl])
        og = _sigmoid(o_ref[:, vsl] + bo_ref[:, vsl])
        out_ref[:, vsl] = (og * hn).astype(BF16)


def mlstm_seq(proj, gates, nbatch, seq_len, gbias, b_o, g_norm):
    ln = min(MLSTM_L, seq_len)
    p3 = proj.reshape(nbatch, seq_len, IN_C_MAIN)
    g3 = gates.reshape(nbatch, seq_len, GATE_PAD)
    out, c, n, m = pl.pallas_call(
        _mlstm_seq_kernel,
        grid=(nbatch, seq_len // ln),
        in_specs=[pl.BlockSpec((None, ln, QC), lambda b, c: (b, c, 0)),
                  pl.BlockSpec((None, ln, QC), lambda b, c: (b, c, 1)),
                  pl.BlockSpec((None, ln, VC), lambda b, c: (b, c, 2 * QC // VC)),
                  pl.BlockSpec((None, ln, VC), lambda b, c: (b, c, 2 * QC // VC + 1)),
                  pl.BlockSpec((None, ln, GATE_PAD), lambda b, c: (b, c, 0)),
                  pl.BlockSpec((1, GATE_PAD), lambda b, c: (0, 0)),
                  pl.BlockSpec((1, VC), lambda b, c: (0, 0)),
                  pl.BlockSpec((1, VC), lambda b, c: (0, 0))],
        out_specs=[pl.BlockSpec((None, ln, VC), lambda b, c: (b, c, 0)),
                   pl.BlockSpec((None, N_HEADS_C, DK_C, DV_C), lambda b, c: (b, 0, 0, 0)),
                   pl.BlockSpec((None, N_HEADS_C, DK_C), lambda b, c: (b, 0, 0)),
                   pl.BlockSpec((None, N_HEADS_C, LANES), lambda b, c: (b, 0, 0))],
        out_shape=[jax.ShapeDtypeStruct((nbatch, seq_len, VC), BF16),
                   jax.ShapeDtypeStruct((nbatch, N_HEADS_C, DK_C, DV_C), F32),
                   jax.ShapeDtypeStruct((nbatch, N_HEADS_C, DK_C), F32),
                   jax.ShapeDtypeStruct((nbatch, N_HEADS_C, LANES), F32)],
        compiler_params=_cparams(2),
        name="mlstm_seq",
    )(p3, p3, p3, p3, g3, gbias, b_o.reshape(1, VC), g_norm.reshape(1, VC))
    return out.reshape(nbatch * seq_len, VC), c, n, m[:, :, 0]


def _mlstm_step_kernel(q_ref, k_ref, v_ref, o_ref, gate_ref, gbias_ref, bo_ref, gn_ref, c0_ref, n0_ref, m0_ref,
                       out_ref, c_ref, n_ref, m_ref):
    q8 = q_ref[...]
    k8 = k_ref[...] * (DK_C ** -0.5)
    v8 = v_ref[...]
    gate = gate_ref[...] + gbias_ref[...]
    i8 = gate[:, 0:1]
    lf8 = _log_sigmoid(gate[:, 1:2])
    g8 = lf8 + m0_ref[...]
    mt = jnp.maximum(g8, i8)
    inter = jnp.exp(g8 - mt)
    wl = jnp.exp(i8 - mt)
    a8 = wl * jnp.sum(q8 * k8, axis=1, keepdims=True)
    row = lax.broadcasted_iota(jnp.int32, (N_HEADS_C, 1), 0)
    qb = q8.astype(BF16)
    kw = k8 * wl
    qc = jnp.zeros((N_HEADS_C, DV_C), F32)
    for h in range(N_HEADS_C):
        c_h = c0_ref[h]
        qc = qc + jnp.where(row == h, jnp.dot(qb, c_h.astype(BF16), preferred_element_type=F32), 0.0)
        kz = jnp.where(row == h, kw, 0.0).astype(BF16)
        c_ref[h] = inter[h:h + 1, :] * c_h + lax.dot_general(kz, v8.astype(BF16), (((0,), (0,)), ((), ())),
                                                             preferred_element_type=F32)
    n0 = n0_ref[...]
    num = inter * qc + a8 * v8
    den = inter * jnp.sum(q8 * n0, axis=1, keepdims=True) + a8
    hh = num / jnp.maximum(jnp.abs(den), jnp.exp(-mt))
    n_ref[...] = inter * n0 + kw
    m_ref[...] = mt
    hn = _rms(hh, gn_ref[...])
    og = _sigmoid(o_ref[...] + bo_ref[...])
    out_ref[...] = (og * hn).astype(BF16)


def mlstm_step(proj, gates, gbias, b_o, g_norm, c0, n0, m0):
    n = proj.shape[0]
    h = N_HEADS_C
    q = proj[:, :QC].reshape(n, h, DK_C)
    k = proj[:, QC:2 * QC].reshape(n, h, DK_C)
    v = proj[:, 2 * QC:2 * QC + VC].reshape(n, h, DV_C)
    o = proj[:, 2 * QC + VC:].reshape(n, h, DV_C)
    gcols = jnp.swapaxes(gates[:, :2 * h].reshape(n, 2, h), 1, 2)
    gb = jnp.swapaxes(gbias[0, :2 * h].reshape(2, h), 0, 1)
    per = lambda *tail: pl.BlockSpec((None,) + tail, lambda z: (z,) + (0,) * len(tail))
    shared = lambda *shape: pl.BlockSpec(shape, lambda z: (0,) * len(shape))
    out, c, nn, m = pl.pallas_call(
        _mlstm_step_kernel,
        grid=(n,),
        in_specs=[per(h, DK_C), per(h, DK_C), per(h, DV_C), per(h, DV_C), per(h, 2),
                  shared(h, 2), shared(h, DV_C), shared(h, DV_C),
                  per(h, DK_C, DV_C), per(h, DK_C), per(h, 1)],
        out_specs=[per(h, DV_C), per(h, DK_C, DV_C), per(h, DK_C), per(h, 1)],
        out_shape=[jax.ShapeDtypeStruct((n, h, DV_C), BF16),
                   jax.ShapeDtypeStruct((n, h, DK_C, DV_C), F32),
                   jax.ShapeDtypeStruct((n, h, DK_C), F32),
                   jax.ShapeDtypeStruct((n, h, 1), F32)],
        compiler_params=_cparams(1),
        name="mlstm_step",
    )(q, k, v, o, gcols, gb, b_o.reshape(h, DV_C), g_norm.reshape(h, DV_C), c0, n0, m0.reshape(n, h, 1))
    return out.reshape(n, VC), c, nn, m.reshape(n, h)


def _prep_weights(w):
    a_in = w['a_w_in']
    q, k, v, xr, gr = jnp.split(a_in, [QA, QA + KA, QA + 2 * KA, QA + 2 * KA + D_RNN], axis=-1)
    c_in = w['c_w_in']
    gate_w = jnp.pad(c_in[..., IN_C_MAIN:], ((0, 0), (0, 0), (0, GATE_PAD - 2 * N_HEADS_C)))
    nc = c_in.shape[0]
    gbias = jnp.concatenate([w['c_b_i'], w['c_b_f'], jnp.zeros((nc, GATE_PAD - 2 * N_HEADS_C), F32)], axis=-1)
    return {
        'a_w_in': jnp.concatenate([q, xr, gr, k, v], axis=-1).astype(BF16),
        'a_w_out': w['a_w_out'].astype(BF16),
        'c_w_in': c_in[..., :IN_C_MAIN].astype(BF16),
        'c_w_gate': gate_w.astype(BF16),
        'c_gbias': gbias.reshape(nc, 1, GATE_PAD),
        'c_w_out': w['c_w_out'].astype(BF16),
        'ffn_w_up': w['ffn_w_up'].astype(BF16),
        'ffn_w_down': w['ffn_w_down'].astype(BF16),
        'lru_wa': jax.vmap(_lru_blockdiag)(w['a_lru_wa']),
        'lru_wx': jax.vmap(_lru_blockdiag)(w['a_lru_wx']),
    }


def _trunk(x, seq_len, state, w, pw):
    m = x.shape[0]
    nbatch = m // seq_len
    outs = {'k': [], 'v': [], 'h': [], 'conv': [], 'C': [], 'n': [], 'm': [], 'f': []}
    for l in range(DEPTH):
        j = l // 2
        if l % 2 == 0:
            proj = norm_matmul(x, w['norm_mix_pre'][l], pw['a_w_in'][j], TN_PROJ)
            lru_args = (w['a_conv_w'][j], w['a_conv_b'][j], pw['lru_wa'][j], pw['lru_wx'][j],
                        w['a_lru_ba'][j], w['a_lru_bx'][j], w['a_lru_lambda'][j])
            if state is None:
                att, k_new, v_new = swa_prompt(proj, w['a_sinks'][j], nbatch, seq_len)
                rec, h_new, c_new = lru_seq(proj, nbatch, seq_len, *lru_args)
            else:
                att, k_new, v_new = swa_sample(proj, state['cache_k'][j], state['cache_v'][j], w['a_sinks'][j])
                rec, h_new, c_new = lru_step(proj, state['lru_h'][j], state['lru_conv'][j], *lru_args)
            outs['k'].append(k_new.reshape(nbatch, -1, N_KV_A, HD_A))
            outs['v'].append(v_new.reshape(nbatch, -1, N_KV_A, HD_A))
            outs['h'].append(h_new)
            outs['conv'].append(c_new)
            x = out_proj([att, rec], pw['a_w_out'][j], x, w['norm_mix_post'][l])
        else:
            proj = norm_matmul(x, w['norm_mix_pre'][l], pw['c_w_in'][j], TN_PROJ)
            gates = norm_matmul(x, w['norm_mix_pre'][l], pw['c_w_gate'][j], GATE_PAD)
            if state is None:
                hg, c_new, n_new, m_new = mlstm_seq(proj, gates, nbatch, seq_len, pw['c_gbias'][j],
                                                    w['c_b_o'][j], w['c_norm'][j])
            else:
                hg, c_new, n_new, m_new = mlstm_step(proj, gates, pw['c_gbias'][j], w['c_b_o'][j], w['c_norm'][j],
                                                     state['mlstm_C'][j], state['mlstm_n'][j], state['mlstm_m'][j])
            outs['C'].append(c_new)
            outs['n'].append(n_new)
            outs['m'].append(m_new)
            x = out_proj([hg], pw['c_w_out'][j], x, w['norm_mix_post'][l])
        ffn_args = (w['norm_ffn_pre'][l], pw['ffn_w_up'][l], w['ffn_conv_w'][l], w['ffn_conv_b'][l],
                    pw['ffn_w_down'][l], w['norm_ffn_post'][l])
        if state is None:
            x, f_new = ffn_seq(x, seq_len, *ffn_args)
        else:
            x, f_new = ffn_step(x, state['ffn_conv'][l], *ffn_args)
        outs['f'].append(f_new)
    return x, {name: jnp.stack(vals) for name, vals in outs.items()}


def kernel(x_prompt, x_sample, cache_k, cache_v, state_lru_h, state_lru_conv, state_mlstm_C, state_mlstm_n, state_mlstm_m, state_ffn_conv, norm_mix_pre, norm_mix_post, norm_ffn_pre, norm_ffn_post, a_w_in, a_sinks, a_conv_w, a_conv_b, a_lru_wa, a_lru_ba, a_lru_wx, a_lru_bx, a_lru_lambda, a_w_out, c_w_in, c_b_i, c_b_f, c_b_o, c_norm, c_w_out, ffn_w_up, ffn_conv_w, ffn_conv_b, ffn_w_down):
    w = {
        'norm_mix_pre': norm_mix_pre, 'norm_mix_post': norm_mix_post,
        'norm_ffn_pre': norm_ffn_pre, 'norm_ffn_post': norm_ffn_post,
        'a_w_in': a_w_in, 'a_sinks': a_sinks, 'a_conv_w': a_conv_w, 'a_conv_b': a_conv_b,
        'a_lru_wa': a_lru_wa, 'a_lru_ba': a_lru_ba, 'a_lru_wx': a_lru_wx, 'a_lru_bx': a_lru_bx,
        'a_lru_lambda': a_lru_lambda, 'a_w_out': a_w_out,
        'c_w_in': c_w_in, 'c_b_i': c_b_i, 'c_b_f': c_b_f, 'c_b_o': c_b_o, 'c_norm': c_norm, 'c_w_out': c_w_out,
        'ffn_w_up': ffn_w_up, 'ffn_conv_w': ffn_conv_w, 'ffn_conv_b': ffn_conv_b, 'ffn_w_down': ffn_w_down,
    }
    pw = _prep_weights(w)
    bp, sp, d = x_prompt.shape
    bs, ss, _ = x_sample.shape
    assert ss == 1, "the sample group advances one token per step"
    y_p, st_p = _trunk(x_prompt.reshape(bp * sp, d), sp, None, w, pw)
    state = {'cache_k': cache_k, 'cache_v': cache_v, 'lru_h': state_lru_h, 'lru_conv': state_lru_conv,
             'mlstm_C': state_mlstm_C, 'mlstm_n': state_mlstm_n, 'mlstm_m': state_mlstm_m,
             'ffn_conv': state_ffn_conv}
    y_s, st_s = _trunk(x_sample.reshape(bs * ss, d), 1, state, w, pw)
    return (y_p.reshape(bp, sp, d), y_s.reshape(bs, ss, d),
            st_p['k'], st_s['k'], st_p['v'], st_s['v'], st_p['h'], st_s['h'],
            st_p['conv'], st_s['conv'], st_p['C'], st_s['C'], st_p['n'], st_s['n'],
            st_p['m'], st_s['m'], st_p['f'], st_s['f'])
```

```python
import functools
import math

import jax
import jax.numpy as jnp
from jax import lax
from jax.experimental import pallas as pl
from jax.experimental.pallas import tpu as pltpu

F32 = jnp.float32
BF16 = jnp.bfloat16

D_MODEL = 2048
DEPTH = 4
PAST_LEN = 16384
N_HEADS_A = 16
N_KV_A = 4
HD_A = 64
GROUP_A = N_HEADS_A // N_KV_A
WINDOW = 128
BLOCK_A = 128
ROPE_THETA = 10000.0
D_RNN = 1024
N_BLOCKS_B = 16
BS_B = D_RNN // N_BLOCKS_B
CONV_B = 4
LRU_C = 8.0
N_HEADS_C = 8
DK_C = 128
DV_C = 256
D_FF = 6144
CONV_F = 3
EPS = 1e-6
QA = N_HEADS_A * HD_A
KA = N_KV_A * HD_A
IN_A = QA + 2 * KA + 2 * D_RNN
QC = N_HEADS_C * DK_C
VC = N_HEADS_C * DV_C
IN_C_MAIN = 2 * QC + 2 * VC

LANES = 128
SUBLANES = 8
VMEM_LIMIT_BYTES = 60 * 1024 * 1024

TM = 512
TM_PROJ = 1024
TF = 1024
LRU_TT = 256
MLSTM_L = 256
LRU_GROUP = 256
GATE_PAD = LANES
SAMPLES_PER_STEP = 4


def _cparams(n_axes):
    return pltpu.CompilerParams(dimension_semantics=("arbitrary",) * n_axes,
                                vmem_limit_bytes=VMEM_LIMIT_BYTES)


def _rms(x, g):
    ms = jnp.mean(x * x, axis=-1, keepdims=True)
    return x * lax.rsqrt(ms + EPS) * g


def _gelu_tanh(x):
    c = math.sqrt(2.0 / math.pi)
    hx = 0.5 * x
    return hx + hx * jnp.tanh(x * (c + (c * 0.044715) * (x * x)))


def _sigmoid(x):
    return 1.0 / (1.0 + jnp.exp(-x))


def _log_sigmoid(x):
    return jnp.minimum(x, 0.0) - jnp.log1p(jnp.exp(-jnp.abs(x)))


def _norm_matmul_kernel(x_ref, g_ref, w_ref, *rest, with_side):
    o_ref, hn_ref = rest[1 if with_side else 0], rest[-1]

    @pl.when(pl.program_id(1) == 0)
    def _():
        hn_ref[...] = _rms(x_ref[...], g_ref[...]).astype(BF16)
        if with_side:
            rest[2][...] = jnp.dot(hn_ref[...], rest[0][...], preferred_element_type=F32)

    o_ref[...] = jnp.dot(hn_ref[...], w_ref[...], preferred_element_type=F32)


def _proj_tn(m, n):
    budget = (8 if m <= LANES else 4) * 1024 * 1024
    best = LANES
    for tn in range(LANES, n + 1, LANES):
        if n % tn == 0 and D_MODEL * tn * 2 <= budget:
            best = tn
    return best


def norm_matmul(x, g, w, tn, w_side=None):
    m, k = x.shape
    n = w.shape[1]
    tm = min(TM_PROJ, m)
    in_specs = [pl.BlockSpec((tm, k), lambda i, j: (i, 0)),
                pl.BlockSpec((1, k), lambda i, j: (0, 0)),
                pl.BlockSpec((k, tn), lambda i, j: (0, j))]
    out_specs = [pl.BlockSpec((tm, tn), lambda i, j: (i, j))]
    out_shape = [jax.ShapeDtypeStruct((m, n), F32)]
    args = [x, g.reshape(1, k), w]
    if w_side is not None:
        ns = w_side.shape[1]
        in_specs.append(pl.BlockSpec((k, ns), lambda i, j: (0, 0)))
        out_specs.append(pl.BlockSpec((tm, ns), lambda i, j: (i, 0)))
        out_shape.append(jax.ShapeDtypeStruct((m, ns), F32))
        args.append(w_side)
    outs = pl.pallas_call(
        functools.partial(_norm_matmul_kernel, with_side=w_side is not None),
        grid=(m // tm, n // tn),
        in_specs=in_specs,
        out_specs=out_specs,
        out_shape=out_shape,
        scratch_shapes=[pltpu.VMEM((tm, k), BF16)],
        compiler_params=_cparams(2),
        name="norm_matmul",
    )(*args)
    return outs[0] if w_side is None else outs


def _out_proj_kernel(*refs, n_in):
    a_refs = refs[:n_in]
    w_ref, x_ref, g_ref, o_ref = refs[n_in:]
    acc = None
    row = 0
    for a_ref in a_refs:
        ka = a_ref.shape[1]
        part = jnp.dot(a_ref[...], w_ref[row:row + ka, :], preferred_element_type=F32)
        acc = part if acc is None else acc + part
        row += ka
    o_ref[...] = x_ref[...] + _rms(acc, g_ref[...])


def out_proj(inputs, w, x, g):
    m, d = x.shape
    tm = min(TM, m)
    in_specs = [pl.BlockSpec((tm, a.shape[1]), lambda i: (i, 0)) for a in inputs]
    in_specs += [pl.BlockSpec(w.shape, lambda i: (0, 0)),
                 pl.BlockSpec((tm, d), lambda i: (i, 0)),
                 pl.BlockSpec((1, d), lambda i: (0, 0))]
    return pl.pallas_call(
        functools.partial(_out_proj_kernel, n_in=len(inputs)),
        grid=(m // tm,),
        in_specs=in_specs,
        out_specs=pl.BlockSpec((tm, d), lambda i: (i, 0)),
        out_shape=jax.ShapeDtypeStruct((m, d), F32),
        compiler_params=_cparams(1),
        name="out_proj",
    )(*inputs, w, x, g.reshape(1, d))


def _ffn_conv(u, prev_ref, cw_ref, cb_ref, ubuf_ref):
    tm = u.shape[0]
    ubuf_ref[0:SUBLANES, :] = prev_ref[...]
    ubuf_ref[SUBLANES:SUBLANES + tm, :] = u
    cw = cw_ref[...]
    y = ubuf_ref[SUBLANES - 2:SUBLANES - 2 + tm, :] * cw[0:1, :]
    y = y + ubuf_ref[SUBLANES - 1:SUBLANES - 1 + tm, :] * cw[1:2, :]
    y = y + u * cw[2:3, :]
    return y + cb_ref[...]


def _ffn_seq_kernel(x_ref, gpre_ref, wg_ref, wv_ref, cwg_ref, cwv_ref, cbg_ref, cbv_ref, wd_ref, gpost_ref,
                    *rest, tiles_per_seq, cast_next):
    if cast_next:
        wupn_ref, wdn_ref, o_ref, ng_ref, nv_ref, wupn_out, wdn_out, hn_ref, carry_ref, ubuf_ref = rest
        wupn_out[...] = wupn_ref[...].astype(BF16)
        wdn_out[...] = wdn_ref[...].astype(BF16)
    else:
        o_ref, ng_ref, nv_ref, hn_ref, carry_ref, ubuf_ref = rest
    i = pl.program_id(0)
    f = pl.program_id(1)
    nf = pl.num_programs(1)
    tm = x_ref.shape[0]

    @pl.when(f == 0)
    def _():
        hn_ref[...] = _rms(x_ref[...], gpre_ref[...]).astype(BF16)
        o_ref[...] = jnp.zeros_like(o_ref)

    @pl.when(i % tiles_per_seq == 0)
    def _():
        carry_ref[f] = jnp.zeros(carry_ref.shape[1:], F32)

    hn = hn_ref[...]
    ug = jnp.dot(hn, wg_ref[...], preferred_element_type=F32)
    uv = jnp.dot(hn, wv_ref[...], preferred_element_type=F32)
    cg = _ffn_conv(ug, carry_ref.at[f, 0], cwg_ref, cbg_ref, ubuf_ref.at[0])
    cv = _ffn_conv(uv, carry_ref.at[f, 1], cwv_ref, cbv_ref, ubuf_ref.at[1])
    carry_ref[f, 0] = ug[tm - SUBLANES:tm, :]
    carry_ref[f, 1] = uv[tm - SUBLANES:tm, :]
    ng_ref[...] = ug[tm - (CONV_F - 1):tm, :]
    nv_ref[...] = uv[tm - (CONV_F - 1):tm, :]
    act = (_gelu_tanh(cg) * cv).astype(BF16)
    o_ref[...] += jnp.dot(act, wd_ref[...], preferred_element_type=F32)

    @pl.when(f == nf - 1)
    def _():
        o_ref[...] = x_ref[...] + _rms(o_ref[...], gpost_ref[...])


def ffn_seq(x, seq_len, gpre, wup, convw, convb, wdown, gpost, next_f32=None):
    m, d = x.shape
    tm = min(TM, seq_len)
    tps = seq_len // tm
    nb = m // seq_len
    nf = D_FF // TF
    nsteps = (m // tm) * nf
    cast_next = next_f32 is not None
    kern = functools.partial(_ffn_seq_kernel, tiles_per_seq=tps, cast_next=cast_next)
    in_specs = [pl.BlockSpec((tm, d), lambda i, f: (i, 0)),
                pl.BlockSpec((1, d), lambda i, f: (0, 0)),
                pl.BlockSpec((d, TF), lambda i, f: (0, f)),
                pl.BlockSpec((d, TF), lambda i, f: (0, f + nf)),
                pl.BlockSpec((CONV_F, TF), lambda i, f: (0, f)),
                pl.BlockSpec((CONV_F, TF), lambda i, f: (0, f + nf)),
                pl.BlockSpec((1, TF), lambda i, f: (0, f)),
                pl.BlockSpec((1, TF), lambda i, f: (0, f + nf)),
                pl.BlockSpec((TF, d), lambda i, f: (f, 0)),
                pl.BlockSpec((1, d), lambda i, f: (0, 0))]
    out_specs = [pl.BlockSpec((tm, d), lambda i, f: (i, 0)),
                 pl.BlockSpec((None, CONV_F - 1, TF), lambda i, f: (i, 0, f)),
                 pl.BlockSpec((None, CONV_F - 1, TF), lambda i, f: (i, 0, f))]
    out_shape = [jax.ShapeDtypeStruct((m, d), F32),
                 jax.ShapeDtypeStruct((m // tm, CONV_F - 1, D_FF), F32),
                 jax.ShapeDtypeStruct((m // tm, CONV_F - 1, D_FF), F32)]
    args = [x, gpre.reshape(1, d), wup, wup, convw, convw, convb.reshape(1, -1), convb.reshape(1, -1),
            wdown, gpost.reshape(1, d)]
    if cast_next:
        cw, rem_c = divmod(2 * D_FF, nsteps)
        rw, rem_r = divmod(D_FF, nsteps)
        assert rem_c == 0 and rem_r == 0 and cw % LANES == 0 and rw % (2 * SUBLANES) == 0, (nsteps, cw, rw)
        up_spec = pl.BlockSpec((d, cw), lambda i, f: (0, i * nf + f))
        dn_spec = pl.BlockSpec((rw, d), lambda i, f: (i * nf + f, 0))
        in_specs += [up_spec, dn_spec]
        out_specs += [up_spec, dn_spec]
        out_shape += [jax.ShapeDtypeStruct((d, 2 * D_FF), BF16), jax.ShapeDtypeStruct((D_FF, d), BF16)]
        args += list(next_f32)
    outs = pl.pallas_call(
        kern,
        grid=(m // tm, nf),
        in_specs=in_specs,
        out_specs=out_specs,
        out_shape=out_shape,
        scratch_shapes=[pltpu.VMEM((tm, d), BF16),
                        pltpu.VMEM((nf, 2, SUBLANES, TF), F32),
                        pltpu.VMEM((2, tm + SUBLANES, TF), F32)],
        compiler_params=_cparams(2),
        name="ffn_seq",
    )(*args)
    y, ng, nv = outs[:3]
    last = jnp.concatenate([ng, nv], axis=-1).reshape(nb, tps, CONV_F - 1, 2 * D_FF)
    return y, last[:, tps - 1], tuple(outs[3:])


def _ffn_step_kernel(x_ref, gpre_ref, wg_ref, wv_ref, cwg_ref, cwv_ref, cbg_ref, cbv_ref, wd_ref, gpost_ref,
                     p0g_ref, p0v_ref, p1g_ref, p1v_ref,
                     o_ref, ug_ref, uv_ref, hn_ref, acc_ref):
    f = pl.program_id(0)
    nf = pl.num_programs(0)

    @pl.when(f == 0)
    def _():
        hn_ref[...] = _rms(x_ref[...], gpre_ref[...]).astype(BF16)
        acc_ref[...] = jnp.zeros_like(acc_ref)

    hn = hn_ref[...]
    ug = jnp.dot(hn, wg_ref[...], preferred_element_type=F32)
    uv = jnp.dot(hn, wv_ref[...], preferred_element_type=F32)
    ug_ref[...] = ug
    uv_ref[...] = uv
    cwg = cwg_ref[...]
    cwv = cwv_ref[...]
    cg = p0g_ref[...] * cwg[0:1, :] + p1g_ref[...] * cwg[1:2, :] + ug * cwg[2:3, :] + cbg_ref[...]
    cv = p0v_ref[...] * cwv[0:1, :] + p1v_ref[...] * cwv[1:2, :] + uv * cwv[2:3, :] + cbv_ref[...]
    act = (_gelu_tanh(cg) * cv).astype(BF16)
    acc_ref[...] += jnp.dot(act, wd_ref[...], preferred_element_type=F32)

    @pl.when(f == nf - 1)
    def _():
        o_ref[...] = x_ref[...] + _rms(acc_ref[...], gpost_ref[...])


def ffn_step(x, past, gpre, wup, convw, convb, wdown, gpost):
    n, d = x.shape
    nf = D_FF // TF
    past2 = past.reshape(n, (CONV_F - 1) * 2 * D_FF)
    y, ug, uv = pl.pallas_call(
        _ffn_step_kernel,
        grid=(nf,),
        in_specs=[pl.BlockSpec((n, d), lambda f: (0, 0)),
                  pl.BlockSpec((1, d), lambda f: (0, 0)),
                  pl.BlockSpec((d, TF), lambda f: (0, f)),
                  pl.BlockSpec((d, TF), lambda f: (0, f + nf)),
                  pl.BlockSpec((CONV_F, TF), lambda f: (0, f)),
                  pl.BlockSpec((CONV_F, TF), lambda f: (0, f + nf)),
                  pl.BlockSpec((1, TF), lambda f: (0, f)),
                  pl.BlockSpec((1, TF), lambda f: (0, f + nf)),
                  pl.BlockSpec((TF, d), lambda f: (f, 0)),
                  pl.BlockSpec((1, d), lambda f: (0, 0)),
                  pl.BlockSpec((n, TF), lambda f: (0, f)),
                  pl.BlockSpec((n, TF), lambda f: (0, f + nf)),
                  pl.BlockSpec((n, TF), lambda f: (0, f + 2 * nf)),
                  pl.BlockSpec((n, TF), lambda f: (0, f + 3 * nf))],
        out_specs=[pl.BlockSpec((n, d), lambda f: (0, 0)),
                   pl.BlockSpec((n, TF), lambda f: (0, f)),
                   pl.BlockSpec((n, TF), lambda f: (0, f))],
        out_shape=[jax.ShapeDtypeStruct((n, d), F32),
                   jax.ShapeDtypeStruct((n, D_FF), F32),
                   jax.ShapeDtypeStruct((n, D_FF), F32)],
        scratch_shapes=[pltpu.VMEM((n, d), BF16), pltpu.VMEM((n, d), F32)],
        compiler_params=_cparams(1),
        name="ffn_step",
    )(x, gpre.reshape(1, d), wup, wup, convw, convw, convb.reshape(1, -1), convb.reshape(1, -1),
      wdown, gpost.reshape(1, d), past2, past2, past2, past2)
    u_new = jnp.concatenate([ug, uv], axis=-1)
    return y, jnp.concatenate([past[:, 1:], u_new[:, None, :]], axis=1)


def _rope_tables(pos0, t):
    half = HD_A // 2
    pos = (pos0 + jnp.arange(t)).astype(F32)
    inv = ROPE_THETA ** (-jnp.arange(half, dtype=F32) / half)
    ang = pos[:, None] * inv[None, :]
    cos, sin = jnp.cos(ang), jnp.sin(ang)
    zero = jnp.zeros_like(sin)
    cos_h = jnp.concatenate([cos, cos], axis=-1)
    sin_up = jnp.concatenate([-sin, zero], axis=-1)
    sin_dn = jnp.concatenate([zero, sin], axis=-1)
    two = lambda a: jnp.concatenate([a, a], axis=-1)
    return two(cos_h), two(sin_up), two(sin_dn)


def _rope(x, cos, sin_up, sin_dn):
    w = x.shape[1]
    reps = w // LANES
    tile = lambda a: jnp.concatenate([a] * reps, axis=1) if reps > 1 else a
    x_up = pltpu.roll(x, w - HD_A // 2, axis=1)
    x_dn = pltpu.roll(x, HD_A // 2, axis=1)
    return x * tile(cos) + x_up * tile(sin_up) + x_dn * tile(sin_dn)


def _dup_half(a, half):
    lane = lax.broadcasted_iota(jnp.int32, a.shape, 1)
    sw = pltpu.roll(a, HD_A, axis=1)
    lo = lane < HD_A
    return jnp.where(lo, a, sw) if half == 0 else jnp.where(lo, sw, a)


def _sink_softmax(s, sk_rows):
    m = jnp.maximum(jnp.max(s, axis=-1, keepdims=True), sk_rows)
    p = jnp.exp(s - m)
    return p, jnp.sum(p, axis=-1, keepdims=True) + jnp.exp(sk_rows - m)


def _swa_prompt_kernel(sinks_ref, q_ref, kc_ref, kp_ref, vc_ref, vp_ref,
                       cc_ref, suc_ref, sdc_ref, cp_ref, sup_ref, sdp_ref,
                       att_ref, knew_ref, vnew_ref):
    c = pl.program_id(1)
    nb = pl.num_programs(1)
    bq = q_ref.shape[0]
    q = _rope(q_ref[...], cc_ref[...], suc_ref[...], sdc_ref[...])
    kc = _rope(kc_ref[...], cc_ref[...], suc_ref[...], sdc_ref[...])
    kp = _rope(kp_ref[...], cp_ref[...], sup_ref[...], sdp_ref[...])
    vc = vc_ref[...]
    vp = vp_ref[...]

    @pl.when(c == nb - 1)
    def _():
        knew_ref[...] = kc
        vnew_ref[...] = vc

    rows = GROUP_A * bq
    r = lax.broadcasted_iota(jnp.int32, (rows, 2 * bq), 0) % bq
    col = lax.broadcasted_iota(jnp.int32, (rows, 2 * bq), 1)
    valid = (col >= r) & (col <= r + WINDOW) & ((c > 0) | (col >= bq))
    lane = lax.broadcasted_iota(jnp.int32, (bq, LANES), 1)
    lo = lane < HD_A
    row_head = lax.broadcasted_iota(jnp.int32, (rows, 1), 0) // bq

    for h in range(N_KV_A):
        t, half = divmod(h, 2)
        ksl = slice(t * LANES, (t + 1) * LANES)
        kd = _dup_half(jnp.concatenate([kp[:, ksl], kc[:, ksl]], axis=0), half).astype(BF16)
        vd = _dup_half(jnp.concatenate([vp[:, ksl], vc[:, ksl]], axis=0), half).astype(BF16)
        parts = []
        for p in range(2):
            q2 = q[:, (2 * h + p) * LANES:(2 * h + p + 1) * LANES]
            parts.append(jnp.where(lo, q2, 0.0))
            parts.append(jnp.where(lo, 0.0, q2))
        qs = jnp.concatenate(parts, axis=0).astype(BF16)
        s = lax.dot_general(qs, kd, (((1,), (1,)), ((), ())), preferred_element_type=F32) * (HD_A ** -0.5)
        s = jnp.where(valid, s, -jnp.inf)
        sk = jnp.zeros((rows, 1), F32)
        for g in range(GROUP_A):
            sk = jnp.where(row_head == g, sinks_ref[h * GROUP_A + g], sk)
        pr, den = _sink_softmax(s, sk)
        o = jnp.dot(pr.astype(BF16), vd, preferred_element_type=F32) / den
        for p in range(2):
            o_lo = o[(2 * p) * bq:(2 * p + 1) * bq, :]
            o_hi = o[(2 * p + 1) * bq:(2 * p + 2) * bq, :]
            att_ref[:, (2 * h + p) * LANES:(2 * h + p + 1) * LANES] = jnp.where(lo, o_lo, o_hi).astype(BF16)


def swa_prompt(proj, sinks, nbatch, seq_len):
    nb = seq_len // BLOCK_A
    p3 = proj.reshape(nbatch, seq_len, IN_A)
    kcol = (QA + 2 * D_RNN) // KA
    cos, sup, sdn = _rope_tables(0, seq_len)
    prev = lambda c: jnp.maximum(c - 1, 0)
    tab_c = pl.BlockSpec((BLOCK_A, LANES), lambda b, c: (c, 0))
    tab_p = pl.BlockSpec((BLOCK_A, LANES), lambda b, c: (prev(c), 0))
    att, knew, vnew = pl.pallas_call(
        _swa_prompt_kernel,
        grid=(nbatch, nb),
        in_specs=[pl.BlockSpec(memory_space=pltpu.SMEM),
                  pl.BlockSpec((None, BLOCK_A, QA), lambda b, c: (b, c, 0)),
                  pl.BlockSpec((None, BLOCK_A, KA), lambda b, c: (b, c, kcol)),
                  pl.BlockSpec((None, BLOCK_A, KA), lambda b, c: (b, prev(c), kcol)),
                  pl.BlockSpec((None, BLOCK_A, KA), lambda b, c: (b, c, kcol + 1)),
                  pl.BlockSpec((None, BLOCK_A, KA), lambda b, c: (b, prev(c), kcol + 1)),
                  tab_c, tab_c, tab_c, tab_p, tab_p, tab_p],
        out_specs=[pl.BlockSpec((None, BLOCK_A, QA), lambda b, c: (b, c, 0)),
                   pl.BlockSpec((None, BLOCK_A, KA), lambda b, c: (b, 0, 0)),
                   pl.BlockSpec((None, BLOCK_A, KA), lambda b, c: (b, 0, 0))],
        out_shape=[jax.ShapeDtypeStruct((nbatch, seq_len, QA), BF16),
                   jax.ShapeDtypeStruct((nbatch, BLOCK_A, KA), F32),
                   jax.ShapeDtypeStruct((nbatch, BLOCK_A, KA), F32)],
        compiler_params=_cparams(2),
        name="swa_prompt",
    )(sinks, p3, p3, p3, p3, p3, cos, sup, sdn, cos, sup, sdn)
    return att.reshape(nbatch * seq_len, QA), knew, vnew


def _swa_sample_kernel(sinks_ref, q_ref, k_ref, v_ref, kc_ref, vc_ref, cos_ref, sup_ref, sdn_ref,
                       att_ref, knew_ref, vnew_ref):
    for z in range(q_ref.shape[0]):
        _swa_sample_one(sinks_ref, q_ref.at[z], k_ref.at[z], v_ref.at[z], kc_ref.at[z], vc_ref.at[z],
                        cos_ref, sup_ref, sdn_ref, att_ref.at[z], knew_ref.at[z], vnew_ref.at[z])


def _swa_sample_one(sinks_ref, q_ref, k_ref, v_ref, kc_ref, vc_ref, cos_ref, sup_ref, sdn_ref,
                    att_ref, knew_ref, vnew_ref):
    w = kc_ref.shape[0]
    q = _rope(q_ref[...], cos_ref[...], sup_ref[...], sdn_ref[...])
    k = _rope(k_ref[...], cos_ref[...], sup_ref[...], sdn_ref[...])
    v = v_ref[...]
    kcache = kc_ref[...]
    vcache = vc_ref[...]
    knew_ref[0:w - 1, :] = kc_ref[1:w, :]
    knew_ref[w - 1:w, :] = k
    vnew_ref[0:w - 1, :] = vc_ref[1:w, :]
    vnew_ref[w - 1:w, :] = v

    lane = lax.broadcasted_iota(jnp.int32, (1, LANES), 1)
    lo = lane < HD_A
    row = lax.broadcasted_iota(jnp.int32, (SUBLANES, 1), 0)
    for h in range(N_KV_A):
        t, half = divmod(h, 2)
        ksl = slice(t * LANES, (t + 1) * LANES)
        kd = _dup_half(kcache[:, ksl], half)
        vd = _dup_half(vcache[:, ksl], half)
        kd_new = _dup_half(k[:, ksl], half)
        vd_new = _dup_half(v[:, ksl], half)
        parts = []
        for p in range(2):
            q2 = q[:, (2 * h + p) * LANES:(2 * h + p + 1) * LANES]
            parts.append(jnp.where(lo, q2, 0.0))
            parts.append(jnp.where(lo, 0.0, q2))
        parts.append(jnp.zeros((SUBLANES - GROUP_A, LANES), F32))
        qs = jnp.concatenate(parts, axis=0)
        s = lax.dot_general(qs.astype(BF16), kd.astype(BF16), (((1,), (1,)), ((), ())),
                            preferred_element_type=F32) * (HD_A ** -0.5)
        s_new = jnp.sum(qs * kd_new, axis=-1, keepdims=True) * (HD_A ** -0.5)
        sk = jnp.zeros((SUBLANES, 1), F32)
        for g in range(GROUP_A):
            sk = jnp.where(row == g, sinks_ref[h * GROUP_A + g], sk)
        m = jnp.maximum(jnp.maximum(jnp.max(s, axis=-1, keepdims=True), s_new), sk)
        pr = jnp.exp(s - m)
        p_new = jnp.exp(s_new - m)
        den = jnp.sum(pr, axis=-1, keepdims=True) + p_new + jnp.exp(sk - m)
        pr = pr / den
        p_new = p_new / den
        o = jnp.dot(pr.astype(BF16), vd.astype(BF16), preferred_element_type=F32) + p_new * vd_new
        for p in range(2):
            att_ref[:, (2 * h + p) * LANES:(2 * h + p + 1) * LANES] = jnp.where(
                lo, o[2 * p:2 * p + 1, :], o[2 * p + 1:2 * p + 2, :]).astype(BF16)


def swa_sample(proj, cache_k, cache_v, sinks):
    n = proj.shape[0]
    w = cache_k.shape[1]
    p3 = proj.reshape(n, 1, IN_A)
    kcol = (QA + 2 * D_RNN) // KA
    ck = cache_k.reshape(n, w, KA)
    cv = cache_v.reshape(n, w, KA)
    cos, sup, sdn = _rope_tables(PAST_LEN, 1)
    tab = pl.BlockSpec((1, LANES), lambda z: (0, 0))
    gs = math.gcd(n, SAMPLES_PER_STEP)
    att, knew, vnew = pl.pallas_call(
        _swa_sample_kernel,
        grid=(n // gs,),
        in_specs=[pl.BlockSpec(memory_space=pltpu.SMEM),
                  pl.BlockSpec((gs, 1, QA), lambda z: (z, 0, 0)),
                  pl.BlockSpec((gs, 1, KA), lambda z: (z, 0, kcol)),
                  pl.BlockSpec((gs, 1, KA), lambda z: (z, 0, kcol + 1)),
                  pl.BlockSpec((gs, w, KA), lambda z: (z, 0, 0)),
                  pl.BlockSpec((gs, w, KA), lambda z: (z, 0, 0)),
                  tab, tab, tab],
        out_specs=[pl.BlockSpec((gs, 1, QA), lambda z: (z, 0, 0)),
                   pl.BlockSpec((gs, w, KA), lambda z: (z, 0, 0)),
                   pl.BlockSpec((gs, w, KA), lambda z: (z, 0, 0))],
        out_shape=[jax.ShapeDtypeStruct((n, 1, QA), BF16),
                   jax.ShapeDtypeStruct((n, w, KA), F32),
                   jax.ShapeDtypeStruct((n, w, KA), F32)],
        compiler_params=_cparams(1),
        name="swa_sample",
    )(sinks, p3, p3, p3, ck, cv, cos, sup, sdn)
    return att.reshape(n, QA), knew, vnew


def _lru_gates(xc, wa_ref, wx_ref, ba_ref, bx_ref, lam_ref):
    rs, is_ = [], []
    for c in range(D_RNN // LRU_GROUP):
        xg = xc[:, c * LRU_GROUP:(c + 1) * LRU_GROUP].astype(BF16)
        rs.append(jnp.dot(xg, wa_ref[c], preferred_element_type=F32))
        is_.append(jnp.dot(xg, wx_ref[c], preferred_element_type=F32))
    r = _sigmoid(jnp.concatenate(rs, axis=1) + ba_ref[...])
    i = _sigmoid(jnp.concatenate(is_, axis=1) + bx_ref[...])
    log_a = LRU_C * r * _log_sigmoid(lam_ref[...])
    a = jnp.exp(log_a)
    b = jnp.sqrt(-jnp.tanh(log_a) * (a * a + 1.0)) * (i * xc)
    return a, b


def _lru_seq_kernel(xr_ref, gr_ref, cw_ref, cb_ref, wa_ref, wx_ref, ba_ref, bx_ref, lam_ref,
                    rec_ref, hlast_ref, cnew_ref,
                    xbuf_ref, a_ref, b_ref, h_ref, hc_ref):
    t = pl.program_id(1)
    tt = xr_ref.shape[0]

    @pl.when(t == 0)
    def _():
        xbuf_ref[0:SUBLANES, :] = jnp.zeros((SUBLANES, D_RNN), F32)
        hc_ref[...] = jnp.zeros_like(hc_ref)

    xr = xr_ref[...]
    xbuf_ref[SUBLANES:SUBLANES + tt, :] = xr
    cw = cw_ref[...]
    xc = xr * cw[CONV_B - 1:CONV_B, :] + cb_ref[...]
    for j in range(CONV_B - 1):
        off = SUBLANES - (CONV_B - 1) + j
        xc = xc + xbuf_ref[off:off + tt, :] * cw[j:j + 1, :]
    xbuf_ref[0:SUBLANES, :] = xr[tt - SUBLANES:tt, :]
    cnew_ref[...] = xr[tt - (CONV_B - 1):tt, :]

    a, b = _lru_gates(xc, wa_ref, wx_ref, ba_ref, bx_ref, lam_ref)
    a_ref[...] = a
    b_ref[...] = b
    row = lax.broadcasted_iota(jnp.int32, (SUBLANES, D_RNN), 0)

    def body(g, h):
        base = pl.multiple_of(g * SUBLANES, SUBLANES)
        a8 = a_ref[pl.ds(base, SUBLANES), :]
        b8 = b_ref[pl.ds(base, SUBLANES), :]
        for s in (1, 2, 4):
            a_sh = pltpu.roll(a8, s, axis=0)
            b_sh = pltpu.roll(b8, s, axis=0)
            keep = row >= s
            b8 = jnp.where(keep, a8 * b_sh + b8, b8)
            a8 = jnp.where(keep, a8 * a_sh, a8)
        h8 = b8 + a8 * h
        h_ref[pl.ds(base, SUBLANES), :] = h8
        return h8[SUBLANES - 1:SUBLANES, :]

    h_last = lax.fori_loop(0, tt // SUBLANES, body, hc_ref[...])
    hc_ref[...] = h_last
    hlast_ref[...] = h_last
    rec_ref[...] = (h_ref[...] * _gelu_tanh(gr_ref[...])).astype(BF16)


def _lru_blockdiag(w):
    per = LRU_GROUP // BS_B
    w4 = w.reshape(D_RNN // LRU_GROUP, per, BS_B, BS_B)
    eye = jnp.eye(per, dtype=w.dtype)
    return jnp.einsum('cpij,pq->cpiqj', w4, eye).reshape(D_RNN // LRU_GROUP, LRU_GROUP, LRU_GROUP).astype(BF16)


def lru_seq(proj, nbatch, seq_len, cw, cb, wa_bd, wx_bd, ba, bx, lam):
    p3 = proj.reshape(nbatch, seq_len, IN_A)
    tt = min(LRU_TT, seq_len)
    vec = lambda: pl.BlockSpec((1, D_RNN), lambda b, t: (0, 0))
    wspec = lambda: pl.BlockSpec(wa_bd.shape, lambda b, t: (0, 0, 0))
    rec, hlast, cnew = pl.pallas_call(
        _lru_seq_kernel,
        grid=(nbatch, seq_len // tt),
        in_specs=[pl.BlockSpec((None, tt, D_RNN), lambda b, t: (b, t, QA // D_RNN)),
                  pl.BlockSpec((None, tt, D_RNN), lambda b, t: (b, t, QA // D_RNN + 1)),
                  pl.BlockSpec((CONV_B, D_RNN), lambda b, t: (0, 0)),
                  vec(), wspec(), wspec(), vec(), vec(), vec()],
        out_specs=[pl.BlockSpec((None, tt, D_RNN), lambda b, t: (b, t, 0)),
                   pl.BlockSpec((None, 1, D_RNN), lambda b, t: (b, 0, 0)),
                   pl.BlockSpec((None, CONV_B - 1, D_RNN), lambda b, t: (b, 0, 0))],
        out_shape=[jax.ShapeDtypeStruct((nbatch, seq_len, D_RNN), BF16),
                   jax.ShapeDtypeStruct((nbatch, 1, D_RNN), F32),
                   jax.ShapeDtypeStruct((nbatch, CONV_B - 1, D_RNN), F32)],
        scratch_shapes=[pltpu.VMEM((tt + SUBLANES, D_RNN), F32),
                        pltpu.VMEM((tt, D_RNN), F32),
                        pltpu.VMEM((tt, D_RNN), F32),
                        pltpu.VMEM((tt, D_RNN), F32),
                        pltpu.VMEM((1, D_RNN), F32)],
        compiler_params=_cparams(2),
        name="lru_seq",
    )(p3, p3, cw, cb.reshape(1, -1), wa_bd, wx_bd, ba.reshape(1, -1), bx.reshape(1, -1), lam.reshape(1, -1))
    return rec.reshape(nbatch * seq_len, D_RNN), hlast.reshape(nbatch, D_RNN), cnew


def _lru_step_kernel(xr_ref, gr_ref, p0_ref, p1_ref, p2_ref, h0_ref, cw_ref, cb_ref,
                     wa_ref, wx_ref, ba_ref, bx_ref, lam_ref, rec_ref, h_ref):
    cw = cw_ref[...]
    xr = xr_ref[...]
    xc = (p0_ref[...] * cw[0:1, :] + p1_ref[...] * cw[1:2, :] + p2_ref[...] * cw[2:3, :]
          + xr * cw[3:4, :] + cb_ref[...])
    a, b = _lru_gates(xc, wa_ref, wx_ref, ba_ref, bx_ref, lam_ref)
    h = b + a * h0_ref[...]
    h_ref[...] = h
    rec_ref[...] = (h * _gelu_tanh(gr_ref[...])).astype(BF16)


def lru_step(proj, h0, conv0, cw, cb, wa_bd, wx_bd, ba, bx, lam):
    n = proj.shape[0]
    past = conv0.reshape(n, (CONV_B - 1) * D_RNN)
    col = lambda j: pl.BlockSpec((n, D_RNN), lambda i: (0, j))
    vec = lambda: pl.BlockSpec((1, D_RNN), lambda i: (0, 0))
    wspec = lambda: pl.BlockSpec(wa_bd.shape, lambda i: (0, 0, 0))
    rec, h = pl.pallas_call(
        _lru_step_kernel,
        grid=(1,),
        in_specs=[col(QA // D_RNN), col(QA // D_RNN + 1), col(0), col(1), col(2), col(0),
                  pl.BlockSpec((CONV_B, D_RNN), lambda i: (0, 0)),
                  vec(), wspec(), wspec(), vec(), vec(), vec()],
        out_specs=[col(0), col(0)],
        out_shape=[jax.ShapeDtypeStruct((n, D_RNN), BF16), jax.ShapeDtypeStruct((n, D_RNN), F32)],
        compiler_params=_cparams(1),
        name="lru_step",
    )(proj, proj, past, past, past, h0, cw, cb.reshape(1, -1), wa_bd, wx_bd,
      ba.reshape(1, -1), bx.reshape(1, -1), lam.reshape(1, -1))
    xr = proj[:, QA:QA + D_RNN]
    conv_new = jnp.concatenate([conv0[:, 1:], xr[:, None, :]], axis=1)
    return rec, h, conv_new


def _cumsum_lanes(x):
    n = x.shape[1]
    lane = lax.broadcasted_iota(jnp.int32, x.shape, 1)
    s = 1
    while s < n:
        x = x + jnp.where(lane >= s, pltpu.roll(x, s, axis=1), 0.0)
        s *= 2
    return x


def _mlstm_seq_kernel(q_ref, k_ref, v_ref, o_ref, gate_ref, gbias_ref, bo_ref, gn_ref,
                      out_ref, c_ref, n_ref, m_ref):
    ci = pl.program_id(1)
    ln = q_ref.shape[0]

    @pl.when(ci == 0)
    def _():
        c_ref[...] = jnp.zeros_like(c_ref)
        n_ref[...] = jnp.zeros_like(n_ref)
        m_ref[...] = jnp.zeros_like(m_ref)

    gate = gate_ref[...] + gbias_ref[...]
    gate_t = gate.T
    b_t = _cumsum_lanes(_log_sigmoid(gate_t))
    b_cols = b_t.T
    ri = lax.broadcasted_iota(jnp.int32, (ln, ln), 0)
    cj = lax.broadcasted_iota(jnp.int32, (ln, ln), 1)
    causal = cj <= ri

    for h in range(N_HEADS_C):
        fcol = N_HEADS_C + h
        b_col = b_cols[:, fcol:fcol + 1]
        b_row = b_t[fcol:fcol + 1, :]
        i_col = gate[:, h:h + 1]
        i_row = gate_t[h:h + 1, :]
        m_prev = m_ref[h:h + 1, 0:1]
        qh = q_ref[:, h * DK_C:(h + 1) * DK_C]
        kh = k_ref[:, h * DK_C:(h + 1) * DK_C] * (DK_C ** -0.5)
        vh = v_ref[:, h * DV_C:(h + 1) * DV_C].astype(BF16)
        qb = qh.astype(BF16)

        g_col = b_col + m_prev
        dmat = jnp.where(causal, b_col - b_row + i_row, -jnp.inf)
        mt = jnp.maximum(g_col, jnp.max(dmat, axis=1, keepdims=True))
        inter = jnp.exp(g_col - mt)
        s = lax.dot_general(qb, kh.astype(BF16), (((1,), (1,)), ((), ())), preferred_element_type=F32)
        amat = jnp.exp(dmat - mt) * s
        c_h = c_ref[h]
        n_h = n_ref[h:h + 1, :]
        num = inter * jnp.dot(qb, c_h.astype(BF16), preferred_element_type=F32) \
            + jnp.dot(amat.astype(BF16), vh, preferred_element_type=F32)
        den = inter * jnp.sum(qh * n_h, axis=1, keepdims=True) + jnp.sum(amat, axis=1, keepdims=True)
        hh = num / jnp.maximum(jnp.abs(den), jnp.exp(-mt))

        m_last = mt[ln - 1:ln, :]
        decay = jnp.exp(g_col[ln - 1:ln, :] - m_last)
        w_col = jnp.exp(b_col[ln - 1:ln, :] - b_col + i_col - m_last)
        kw = kh * w_col
        c_ref[h] = decay * c_h + lax.dot_general(kw.astype(BF16), vh, (((0,), (0,)), ((), ())),
                                                 preferred_element_type=F32)
        n_ref[h:h + 1, :] = decay * n_h + jnp.sum(kw, axis=0, keepdims=True)
        m_ref[h:h + 1, :] = jnp.broadcast_to(m_last, (1, m_ref.shape[1]))

        vsl = slice(h * DV_C, (h + 1) * DV_C)
        hn = _rms(hh, gn_ref[:, vsl])
        og = _sigmoid(o_ref[:, vsl] + bo_ref[:, vsl])
        out_ref[:, vsl] = (og * hn).astype(BF16)


def mlstm_seq(proj, gates, nbatch, seq_len, gbias, b_o, g_norm):
    ln = min(MLSTM_L, seq_len)
    p3 = proj.reshape(nbatch, seq_len, IN_C_MAIN)
    g3 = gates.reshape(nbatch, seq_len, GATE_PAD)
    out, c, n, m = pl.pallas_call(
        _mlstm_seq_kernel,
        grid=(nbatch, seq_len // ln),
        in_specs=[pl.BlockSpec((None, ln, QC), lambda b, c: (b, c, 0)),
                  pl.BlockSpec((None, ln, QC), lambda b, c: (b, c, 1)),
                  pl.BlockSpec((None, ln, VC), lambda b, c: (b, c, 2 * QC // VC)),
                  pl.BlockSpec((None, ln, VC), lambda b, c: (b, c, 2 * QC // VC + 1)),
                  pl.BlockSpec((None, ln, GATE_PAD), lambda b, c: (b, c, 0)),
                  pl.BlockSpec((1, GATE_PAD), lambda b, c: (0, 0)),
                  pl.BlockSpec((1, VC), lambda b, c: (0, 0)),
                  pl.BlockSpec((1, VC), lambda b, c: (0, 0))],
        out_specs=[pl.BlockSpec((None, ln, VC), lambda b, c: (b, c, 0)),
                   pl.BlockSpec((None, N_HEADS_C, DK_C, DV_C), lambda b, c: (b, 0, 0, 0)),
                   pl.BlockSpec((None, N_HEADS_C, DK_C), lambda b, c: (b, 0, 0)),
                   pl.BlockSpec((None, N_HEADS_C, LANES), lambda b, c: (b, 0, 0))],
        out_shape=[jax.ShapeDtypeStruct((nbatch, seq_len, VC), BF16),
                   jax.ShapeDtypeStruct((nbatch, N_HEADS_C, DK_C, DV_C), F32),
                   jax.ShapeDtypeStruct((nbatch, N_HEADS_C, DK_C), F32),
                   jax.ShapeDtypeStruct((nbatch, N_HEADS_C, LANES), F32)],
        compiler_params=_cparams(2),
        name="mlstm_seq",
    )(p3, p3, p3, p3, g3, gbias, b_o.reshape(1, VC), g_norm.reshape(1, VC))
    return out.reshape(nbatch * seq_len, VC), c, n, m[:, :, 0]


def _mlstm_step_kernel(q_ref, k_ref, v_ref, o_ref, gate_ref, gbias_ref, bo_ref, gn_ref, c0_ref, n0_ref, m0_ref,
                       out_ref, c_ref, n_ref, m_ref):
    q8 = q_ref[...]
    k8 = k_ref[...] * (DK_C ** -0.5)
    v8 = v_ref[...]
    gate = gate_ref[...] + gbias_ref[...]
    i8 = gate[:, 0:1]
    lf8 = _log_sigmoid(gate[:, 1:2])
    g8 = lf8 + m0_ref[...]
    mt = jnp.maximum(g8, i8)
    inter = jnp.exp(g8 - mt)
    wl = jnp.exp(i8 - mt)
    a8 = wl * jnp.sum(q8 * k8, axis=1, keepdims=True)
    row = lax.broadcasted_iota(jnp.int32, (N_HEADS_C, 1), 0)
    qb = q8.astype(BF16)
    kw = k8 * wl
    qc = jnp.zeros((N_HEADS_C, DV_C), F32)
    for h in range(N_HEADS_C):
        c_h = c0_ref[h]
        qc = qc + jnp.where(row == h, jnp.dot(qb, c_h.astype(BF16), preferred_element_type=F32), 0.0)
        kz = jnp.where(row == h, kw, 0.0).astype(BF16)
        c_ref[h] = inter[h:h + 1, :] * c_h + lax.dot_general(kz, v8.astype(BF16), (((0,), (0,)), ((), ())),
                                                             preferred_element_type=F32)
    n0 = n0_ref[...]
    num = inter * qc + a8 * v8
    den = inter * jnp.sum(q8 * n0, axis=1, keepdims=True) + a8
    hh = num / jnp.maximum(jnp.abs(den), jnp.exp(-mt))
    n_ref[...] = inter * n0 + kw
    m_ref[...] = mt
    hn = _rms(hh, gn_ref[...])
    og = _sigmoid(o_ref[...] + bo_ref[...])
    out_ref[...] = (og * hn).astype(BF16)


def mlstm_step(proj, gates, gbias, b_o, g_norm, c0, n0, m0):
    n = proj.shape[0]
    h = N_HEADS_C
    q = proj[:, :QC].reshape(n, h, DK_C)
    k = proj[:, QC:2 * QC].reshape(n, h, DK_C)
    v = proj[:, 2 * QC:2 * QC + VC].reshape(n, h, DV_C)
    o = proj[:, 2 * QC + VC:].reshape(n, h, DV_C)
    gcols = jnp.swapaxes(gates[:, :2 * h].reshape(n, 2, h), 1, 2)
    gb = jnp.swapaxes(gbias[0, :2 * h].reshape(2, h), 0, 1)
    per = lambda *tail: pl.BlockSpec((None,) + tail, lambda z: (z,) + (0,) * len(tail))
    shared = lambda *shape: pl.BlockSpec(shape, lambda z: (0,) * len(shape))
    out, c, nn, m = pl.pallas_call(
        _mlstm_step_kernel,
        grid=(n,),
        in_specs=[per(h, DK_C), per(h, DK_C), per(h, DV_C), per(h, DV_C), per(h, 2),
                  shared(h, 2), shared(h, DV_C), shared(h, DV_C),
                  per(h, DK_C, DV_C), per(h, DK_C), per(h, 1)],
        out_specs=[per(h, DV_C), per(h, DK_C, DV_C), per(h, DK_C), per(h, 1)],
        out_shape=[jax.ShapeDtypeStruct((n, h, DV_C), BF16),
                   jax.ShapeDtypeStruct((n, h, DK_C, DV_C), F32),
                   jax.ShapeDtypeStruct((n, h, DK_C), F32),
                   jax.ShapeDtypeStruct((n, h, 1), F32)],
        compiler_params=_cparams(1),
        name="mlstm_step",
    )(q, k, v, o, gcols, gb, b_o.reshape(h, DV_C), g_norm.reshape(h, DV_C), c0, n0, m0.reshape(n, h, 1))
    return out.reshape(n, VC), c, nn, m.reshape(n, h)


def _prep_weights(w):
    a_in = w['a_w_in']
    q, k, v, xr, gr = jnp.split(a_in, [QA, QA + KA, QA + 2 * KA, QA + 2 * KA + D_RNN], axis=-1)
    c_in = w['c_w_in']
    gate_w = jnp.pad(c_in[..., IN_C_MAIN:], ((0, 0), (0, 0), (0, GATE_PAD - 2 * N_HEADS_C)))
    nc = c_in.shape[0]
    gbias = jnp.concatenate([w['c_b_i'], w['c_b_f'], jnp.zeros((nc, GATE_PAD - 2 * N_HEADS_C), F32)], axis=-1)
    return {
        'a_w_in': jnp.concatenate([q, xr, gr, k, v], axis=-1).astype(BF16),
        'a_w_out': w['a_w_out'].astype(BF16),
        'c_w_in': c_in[..., :IN_C_MAIN].astype(BF16),
        'c_w_gate': gate_w.astype(BF16),
        'c_gbias': gbias.reshape(nc, 1, GATE_PAD),
        'c_w_out': w['c_w_out'].astype(BF16),
        'ffn_w_up': [w['ffn_w_up'][0].astype(BF16)],
        'ffn_w_down': [w['ffn_w_down'][0].astype(BF16)],
        'lru_wa': jax.vmap(_lru_blockdiag)(w['a_lru_wa']),
        'lru_wx': jax.vmap(_lru_blockdiag)(w['a_lru_wx']),
    }


def _trunk(x, seq_len, state, w, pw):
    m = x.shape[0]
    nbatch = m // seq_len
    outs = {'k': [], 'v': [], 'h': [], 'conv': [], 'C': [], 'n': [], 'm': [], 'f': []}
    for l in range(DEPTH):
        j = l // 2
        if l % 2 == 0:
            proj = norm_matmul(x, w['norm_mix_pre'][l], pw['a_w_in'][j], _proj_tn(m, IN_A))
            lru_args = (w['a_conv_w'][j], w['a_conv_b'][j], pw['lru_wa'][j], pw['lru_wx'][j],
                        w['a_lru_ba'][j], w['a_lru_bx'][j], w['a_lru_lambda'][j])
            if state is None:
                att, k_new, v_new = swa_prompt(proj, w['a_sinks'][j], nbatch, seq_len)
                rec, h_new, c_new = lru_seq(proj, nbatch, seq_len, *lru_args)
            else:
                att, k_new, v_new = swa_sample(proj, state['cache_k'][j], state['cache_v'][j], w['a_sinks'][j])
                rec, h_new, c_new = lru_step(proj, state['lru_h'][j], state['lru_conv'][j], *lru_args)
            outs['k'].append(k_new.reshape(nbatch, -1, N_KV_A, HD_A))
            outs['v'].append(v_new.reshape(nbatch, -1, N_KV_A, HD_A))
            outs['h'].append(h_new)
            outs['conv'].append(c_new)
            x = out_proj([att, rec], pw['a_w_out'][j], x, w['norm_mix_post'][l])
        else:
            proj, gates = norm_matmul(x, w['norm_mix_pre'][l], pw['c_w_in'][j], _proj_tn(m, IN_C_MAIN),
                                      w_side=pw['c_w_gate'][j])
            if state is None:
                hg, c_new, n_new, m_new = mlstm_seq(proj, gates, nbatch, seq_len, pw['c_gbias'][j],
                                                    w['c_b_o'][j], w['c_norm'][j])
            else:
                hg, c_new, n_new, m_new = mlstm_step(proj, gates, pw['c_gbias'][j], w['c_b_o'][j], w['c_norm'][j],
                                                     state['mlstm_C'][j], state['mlstm_n'][j], state['mlstm_m'][j])
            outs['C'].append(c_new)
            outs['n'].append(n_new)
            outs['m'].append(m_new)
            x = out_proj([hg], pw['c_w_out'][j], x, w['norm_mix_post'][l])
        ffn_args = (w['norm_ffn_pre'][l], pw['ffn_w_up'][l], w['ffn_conv_w'][l], w['ffn_conv_b'][l],
                    pw['ffn_w_down'][l], w['norm_ffn_post'][l])
        if state is None:
            nxt = (w['ffn_w_up'][l + 1], w['ffn_w_down'][l + 1]) if l + 1 < DEPTH else None
            x, f_new, cast = ffn_seq(x, seq_len, *ffn_args, next_f32=nxt)
            if cast:
                pw['ffn_w_up'].append(cast[0])
                pw['ffn_w_down'].append(cast[1])
        else:
            x, f_new = ffn_step(x, state['ffn_conv'][l], *ffn_args)
        outs['f'].append(f_new)
    return x, {name: jnp.stack(vals) for name, vals in outs.items()}


def kernel(x_prompt, x_sample, cache_k, cache_v, state_lru_h, state_lru_conv, state_mlstm_C, state_mlstm_n, state_mlstm_m, state_ffn_conv, norm_mix_pre, norm_mix_post, norm_ffn_pre, norm_ffn_post, a_w_in, a_sinks, a_conv_w, a_conv_b, a_lru_wa, a_lru_ba, a_lru_wx, a_lru_bx, a_lru_lambda, a_w_out, c_w_in, c_b_i, c_b_f, c_b_o, c_norm, c_w_out, ffn_w_up, ffn_conv_w, ffn_conv_b, ffn_w_down):
    w = {
        'norm_mix_pre': norm_mix_pre, 'norm_mix_post': norm_mix_post,
        'norm_ffn_pre': norm_ffn_pre, 'norm_ffn_post': norm_ffn_post,
        'a_w_in': a_w_in, 'a_sinks': a_sinks, 'a_conv_w': a_conv_w, 'a_conv_b': a_conv_b,
        'a_lru_wa': a_lru_wa, 'a_lru_ba': a_lru_ba, 'a_lru_wx': a_lru_wx, 'a_lru_bx': a_lru_bx,
        'a_lru_lambda': a_lru_lambda, 'a_w_out': a_w_out,
        'c_w_in': c_w_in, 'c_b_i': c_b_i, 'c_b_f': c_b_f, 'c_b_o': c_b_o, 'c_norm': c_norm, 'c_w_out': c_w_out,
        'ffn_w_up': ffn_w_up, 'ffn_conv_w': ffn_conv_w, 'ffn_conv_b': ffn_conv_b, 'ffn_w_down': ffn_w_down,
    }
    pw = _prep_weights(w)
    bp, sp, d = x_prompt.shape
    bs, ss, _ = x_sample.shape
    assert ss == 1, "the sample group advances one token per step"
    y_p, st_p = _trunk(x_prompt.reshape(bp * sp, d), sp, None, w, pw)
    state = {'cache_k': cache_k, 'cache_v': cache_v, 'lru_h': state_lru_h, 'lru_conv': state_lru_conv,
             'mlstm_C': state_mlstm_C, 'mlstm_n': state_mlstm_n, 'mlstm_m': state_mlstm_m,
             'ffn_conv': state_ffn_conv}
    y_s, st_s = _trunk(x_sample.reshape(bs * ss, d), 1, state, w, pw)
    return (y_p.reshape(bp, sp, d), y_s.reshape(bs, ss, d),
            st_p['k'], st_s['k'], st_p['v'], st_s['v'], st_p['h'], st_s['h'],
            st_p['conv'], st_s['conv'], st_p['C'], st_s['C'], st_p['n'], st_s['n'],
            st_p['m'], st_s['m'], st_p['f'], st_s['f'])
```

```python
import functools
import math

import jax
import jax.numpy as jnp
from jax import lax
from jax.experimental import pallas as pl
from jax.experimental.pallas import tpu as pltpu

F32 = jnp.float32
BF16 = jnp.bfloat16

D_MODEL = 2048
DEPTH = 4
PAST_LEN = 16384
N_HEADS_A = 16
N_KV_A = 4
HD_A = 64
GROUP_A = N_HEADS_A // N_KV_A
WINDOW = 128
BLOCK_A = 128
ROPE_THETA = 10000.0
D_RNN = 1024
N_BLOCKS_B = 16
BS_B = D_RNN // N_BLOCKS_B
CONV_B = 4
LRU_C = 8.0
N_HEADS_C = 8
DK_C = 128
DV_C = 256
D_FF = 6144
CONV_F = 3
EPS = 1e-6
QA = N_HEADS_A * HD_A
KA = N_KV_A * HD_A
IN_A = QA + 2 * KA + 2 * D_RNN
QC = N_HEADS_C * DK_C
VC = N_HEADS_C * DV_C
IN_C_MAIN = 2 * QC + 2 * VC

LANES = 128
SUBLANES = 8
VMEM_LIMIT_BYTES = 60 * 1024 * 1024

TM = 512
TM_PROJ = 1024
TF = 1024
FFN_SUB = 1024
LRU_TT = 256
MLSTM_L = 256
LRU_GROUP = 256
GATE_PAD = LANES
SAMPLES_PER_STEP = 4


def _cparams(n_axes):
    return pltpu.CompilerParams(dimension_semantics=("arbitrary",) * n_axes,
                                vmem_limit_bytes=VMEM_LIMIT_BYTES)


def _rms(x, g):
    ms = jnp.mean(x * x, axis=-1, keepdims=True)
    return x * lax.rsqrt(ms + EPS) * g


def _gelu_tanh(x):
    c = math.sqrt(2.0 / math.pi)
    hx = 0.5 * x
    return hx + hx * jnp.tanh(x * (c + (c * 0.044715) * (x * x)))


def _sigmoid(x):
    return 0.5 + 0.5 * jnp.tanh(0.5 * x)


def _log_sigmoid(x):
    return jnp.minimum(x, 0.0) - jnp.log1p(jnp.exp(-jnp.abs(x)))


def _norm_matmul_kernel(x_ref, g_ref, w_ref, *rest, with_side):
    o_ref, hn_ref = rest[1 if with_side else 0], rest[-1]

    @pl.when(pl.program_id(1) == 0)
    def _():
        hn_ref[...] = _rms(x_ref[...], g_ref[...]).astype(BF16)
        if with_side:
            rest[2][...] = jnp.dot(hn_ref[...], rest[0][...], preferred_element_type=F32)

    o_ref[...] = jnp.dot(hn_ref[...], w_ref[...], preferred_element_type=F32)


def _proj_tn(m, n):
    del m
    budget = 8 * 1024 * 1024
    best = LANES
    for tn in range(LANES, n + 1, LANES):
        if n % tn == 0 and D_MODEL * tn * 2 <= budget:
            best = tn
    return best


def _layer(opnd, block, imap):
    arr, idx = opnd
    return pl.BlockSpec((None,) + tuple(block), lambda *g: (idx,) + tuple(imap(*g))), arr


def _unzip(pairs):
    return [p[0] for p in pairs], [p[1] for p in pairs]


def _add_cast_job(jobs, nsteps, step_of, in_specs, args, out_specs, out_shape):
    for arr, layer, ncols in jobs:
        rows = arr.shape[-2]
        rw, rem = divmod(rows, nsteps)
        assert rem == 0 and rw % (2 * SUBLANES) == 0 and ncols % LANES == 0, (rows, nsteps, ncols)
        if layer is None:
            in_specs.append(pl.BlockSpec((rw, ncols), lambda *g: (step_of(*g), 0)))
        else:
            in_specs.append(pl.BlockSpec((None, rw, ncols), lambda *g, _l=layer: (_l, step_of(*g), 0)))
        args.append(arr)
        out_specs.append(pl.BlockSpec((rw, ncols), lambda *g: (step_of(*g), 0)))
        out_shape.append(jax.ShapeDtypeStruct((rows, ncols), BF16))


def norm_matmul(x, g, w, n, tn, w_side=None):
    m, k = x.shape
    assert n % tn == 0 and n <= w[0].shape[-1]
    tm = min(TM_PROJ, m)
    specs, args = _unzip([(pl.BlockSpec((tm, k), lambda i, j: (i, 0)), x),
                          _layer(g, (1, k), lambda i, j: (0, 0)),
                          _layer(w, (k, tn), lambda i, j: (0, j))])
    in_specs = list(specs)
    out_specs = [pl.BlockSpec((tm, tn), lambda i, j: (i, j))]
    out_shape = [jax.ShapeDtypeStruct((m, n), F32)]
    if w_side is not None:
        ns = w_side[0].shape[-1]
        spec, arr = _layer(w_side, (k, ns), lambda i, j: (0, 0))
        in_specs.append(spec)
        out_specs.append(pl.BlockSpec((tm, ns), lambda i, j: (i, 0)))
        out_shape.append(jax.ShapeDtypeStruct((m, ns), F32))
        args.append(arr)
    outs = pl.pallas_call(
        functools.partial(_norm_matmul_kernel, with_side=w_side is not None),
        grid=(m // tm, n // tn),
        in_specs=in_specs,
        out_specs=out_specs,
        out_shape=out_shape,
        scratch_shapes=[pltpu.VMEM((tm, k), BF16)],
        compiler_params=_cparams(2),
        name="norm_matmul",
    )(*args)
    return outs[0] if w_side is None else outs


def _out_proj_kernel(*refs, n_in):
    a_refs = refs[:n_in]
    w_ref, x_ref, g_ref, o_ref = refs[n_in:]
    acc = None
    row = 0
    for a_ref in a_refs:
        ka = a_ref.shape[1]
        part = jnp.dot(a_ref[...], w_ref[row:row + ka, :], preferred_element_type=F32)
        acc = part if acc is None else acc + part
        row += ka
    o_ref[...] = x_ref[...] + _rms(acc, g_ref[...])


def out_proj(inputs, w, x, g):
    m, d = x.shape
    tm = min(TM, m)
    w_spec, w_arr = _layer(w, w[0].shape[1:], lambda i: (0, 0))
    g_spec, g_arr = _layer(g, (1, d), lambda i: (0, 0))
    in_specs = [pl.BlockSpec((tm, a.shape[1]), lambda i: (i, 0)) for a in inputs]
    in_specs += [w_spec, pl.BlockSpec((tm, d), lambda i: (i, 0)), g_spec]
    return pl.pallas_call(
        functools.partial(_out_proj_kernel, n_in=len(inputs)),
        grid=(m // tm,),
        in_specs=in_specs,
        out_specs=pl.BlockSpec((tm, d), lambda i: (i, 0)),
        out_shape=jax.ShapeDtypeStruct((m, d), F32),
        compiler_params=_cparams(1),
        name="out_proj",
    )(*inputs, w_arr, x, g_arr)


def _ffn_conv_params(convw, convb):
    nl = convw.shape[0]
    p = jnp.concatenate([convw, convb[:, None, :], jnp.zeros((nl, SUBLANES - CONV_F - 1, 2 * D_FF), F32)], axis=1)
    return jnp.swapaxes(p.reshape(nl, SUBLANES, 2 * D_FF // TF, TF), 1, 2)


def _ffn_conv(u, prev, cp):
    cw = cp[0:CONV_F, :]
    cb = cp[CONV_F:CONV_F + 1, :]
    y = pltpu.roll(u, 2, axis=0) * cw[0:1, :] + pltpu.roll(u, 1, axis=0) * cw[1:2, :] + u * cw[2:3, :] + cb
    head = jnp.concatenate([prev, u[0:SUBLANES, :]], axis=0)
    y_head = (head[SUBLANES - 2:2 * SUBLANES - 2, :] * cw[0:1, :] + head[SUBLANES - 1:2 * SUBLANES - 1, :] * cw[1:2, :]
              + head[SUBLANES:, :] * cw[2:3, :] + cb)
    return jnp.concatenate([y_head, y[SUBLANES:, :]], axis=0)


def _ffn_seq_kernel(x_ref, gpre_ref, wg_ref, wv_ref, cp_ref, wd_ref, gpost_ref,
                    *rest, tiles_per_seq, cast_next):
    if cast_next:
        wupn_ref, wdn_ref, o_ref, nu_ref, wupn_out, wdn_out, hn_ref, carry_ref = rest
        wupn_out[...] = wupn_ref[...].astype(BF16)
        wdn_out[...] = wdn_ref[...].astype(BF16)
    else:
        o_ref, nu_ref, hn_ref, carry_ref = rest
    i = pl.program_id(0)
    f = pl.program_id(1)
    nf = pl.num_programs(1)
    tm = x_ref.shape[0]

    @pl.when(f == 0)
    def _():
        hn_ref[...] = _rms(x_ref[...], gpre_ref[...]).astype(BF16)
        o_ref[...] = jnp.zeros_like(o_ref)

    @pl.when(i % tiles_per_seq == 0)
    def _():
        carry_ref[f] = jnp.zeros(carry_ref.shape[1:], F32)

    hn = hn_ref[...]
    acc = None
    for c in range(wg_ref.shape[1] // FFN_SUB):
        sl = slice(c * FFN_SUB, (c + 1) * FFN_SUB)
        ug = jnp.dot(hn, wg_ref[:, sl], preferred_element_type=F32)
        uv = jnp.dot(hn, wv_ref[:, sl], preferred_element_type=F32)
        cg = _ffn_conv(ug, carry_ref[f, 0, :, sl], cp_ref[f, :, sl])
        cv = _ffn_conv(uv, carry_ref[f, 1, :, sl], cp_ref[f + nf, :, sl])
        carry_ref[f, 0, :, sl] = ug[tm - SUBLANES:tm, :]
        carry_ref[f, 1, :, sl] = uv[tm - SUBLANES:tm, :]
        nu_ref[0, :, sl] = ug[tm - (CONV_F - 1):tm, :]
        nu_ref[1, :, sl] = uv[tm - (CONV_F - 1):tm, :]
        act = (_gelu_tanh(cg) * cv).astype(BF16)
        part = jnp.dot(act, wd_ref[sl, :], preferred_element_type=F32)
        acc = part if acc is None else acc + part
    o_ref[...] += acc

    @pl.when(f == nf - 1)
    def _():
        o_ref[...] = x_ref[...] + _rms(o_ref[...], gpost_ref[...])


def ffn_seq(x, seq_len, gpre, wup, cp, wdown, gpost, next_f32=None):
    m, d = x.shape
    tm = min(TM, seq_len)
    tps = seq_len // tm
    nb = m // seq_len
    nf = D_FF // TF
    nsteps = (m // tm) * nf
    cast_next = next_f32 is not None
    kern = functools.partial(_ffn_seq_kernel, tiles_per_seq=tps, cast_next=cast_next)
    in_specs, args = _unzip([
        (pl.BlockSpec((tm, d), lambda i, f: (i, 0)), x),
        _layer(gpre, (1, d), lambda i, f: (0, 0)),
        (pl.BlockSpec((d, TF), lambda i, f: (0, f)), wup),
        (pl.BlockSpec((d, TF), lambda i, f: (0, f + nf)), wup),
        _layer(cp, cp[0].shape[1:], lambda i, f: (0, 0, 0)),
        (pl.BlockSpec((TF, d), lambda i, f: (f, 0)), wdown),
        _layer(gpost, (1, d), lambda i, f: (0, 0))])
    out_specs = [pl.BlockSpec((tm, d), lambda i, f: (i, 0)),
                 pl.BlockSpec((None, 2, CONV_F - 1, TF), lambda i, f: (i, 0, 0, f))]
    out_shape = [jax.ShapeDtypeStruct((m, d), F32),
                 jax.ShapeDtypeStruct((m // tm, 2, CONV_F - 1, D_FF), F32)]
    if cast_next:
        cw, rem_c = divmod(2 * D_FF, nsteps)
        rw, rem_r = divmod(D_FF, nsteps)
        assert rem_c == 0 and rem_r == 0 and cw % LANES == 0 and rw % (2 * SUBLANES) == 0, (nsteps, cw, rw)
        up_in, up_arr = _layer(next_f32[0], (d, cw), lambda i, f: (0, i * nf + f))
        dn_in, dn_arr = _layer(next_f32[1], (rw, d), lambda i, f: (i * nf + f, 0))
        in_specs += [up_in, dn_in]
        args += [up_arr, dn_arr]
        out_specs += [pl.BlockSpec((d, cw), lambda i, f: (0, i * nf + f)),
                      pl.BlockSpec((rw, d), lambda i, f: (i * nf + f, 0))]
        out_shape += [jax.ShapeDtypeStruct((d, 2 * D_FF), BF16), jax.ShapeDtypeStruct((D_FF, d), BF16)]
    outs = pl.pallas_call(
        kern,
        grid=(m // tm, nf),
        in_specs=in_specs,
        out_specs=out_specs,
        out_shape=out_shape,
        scratch_shapes=[pltpu.VMEM((tm, d), BF16),
                        pltpu.VMEM((nf, 2, SUBLANES, TF), F32)],
        compiler_params=_cparams(2),
        name="ffn_seq",
    )(*args)
    y, nu = outs[:2]
    last = nu.reshape(nb, tps, 2, CONV_F - 1, D_FF)[:, tps - 1]
    last = jnp.swapaxes(last, 1, 2).reshape(nb, CONV_F - 1, 2 * D_FF)
    return y, last, tuple(outs[2:])


def _ffn_step_kernel(x_ref, gpre_ref, wg_ref, wv_ref, cwg_ref, cwv_ref, cbg_ref, cbv_ref, wd_ref, gpost_ref,
                     pg_ref, pv_ref,
                     o_ref, ng_ref, nv_ref, hn_ref, acc_ref):
    f = pl.program_id(0)
    nf = pl.num_programs(0)

    @pl.when(f == 0)
    def _():
        hn_ref[...] = _rms(x_ref[...], gpre_ref[...]).astype(BF16)
        acc_ref[...] = jnp.zeros_like(acc_ref)

    hn = hn_ref[...]
    ug = jnp.dot(hn, wg_ref[...], preferred_element_type=F32)
    uv = jnp.dot(hn, wv_ref[...], preferred_element_type=F32)
    p0g, p1g = pg_ref[:, 0, :], pg_ref[:, 1, :]
    p0v, p1v = pv_ref[:, 0, :], pv_ref[:, 1, :]
    ng_ref[:, 0, :] = p1g
    ng_ref[:, 1, :] = ug
    nv_ref[:, 0, :] = p1v
    nv_ref[:, 1, :] = uv
    cwg = cwg_ref[...]
    cwv = cwv_ref[...]
    cg = p0g * cwg[0:1, :] + p1g * cwg[1:2, :] + ug * cwg[2:3, :] + cbg_ref[...]
    cv = p0v * cwv[0:1, :] + p1v * cwv[1:2, :] + uv * cwv[2:3, :] + cbv_ref[...]
    act = (_gelu_tanh(cg) * cv).astype(BF16)
    acc_ref[...] += jnp.dot(act, wd_ref[...], preferred_element_type=F32)

    @pl.when(f == nf - 1)
    def _():
        o_ref[...] = x_ref[...] + _rms(acc_ref[...], gpost_ref[...])


def ffn_step(x, past, gpre, wup, convw, convb, wdown, gpost):
    n, d = x.shape
    nf = D_FF // TF
    phalf = lambda c: _layer(past, (n, CONV_F - 1, TF), lambda f: (0, 0, f + c * nf))
    in_specs, args = _unzip([
        (pl.BlockSpec((n, d), lambda f: (0, 0)), x),
        _layer(gpre, (1, d), lambda f: (0, 0)),
        (pl.BlockSpec((d, TF), lambda f: (0, f)), wup),
        (pl.BlockSpec((d, TF), lambda f: (0, f + nf)), wup),
        _layer(convw, (CONV_F, TF), lambda f: (0, f)),
        _layer(convw, (CONV_F, TF), lambda f: (0, f + nf)),
        _layer(convb, (1, TF), lambda f: (0, f)),
        _layer(convb, (1, TF), lambda f: (0, f + nf)),
        (pl.BlockSpec((TF, d), lambda f: (f, 0)), wdown),
        _layer(gpost, (1, d), lambda f: (0, 0)),
        phalf(0), phalf(1)])
    y, ng, nv = pl.pallas_call(
        _ffn_step_kernel,
        grid=(nf,),
        in_specs=in_specs,
        out_specs=[pl.BlockSpec((n, d), lambda f: (0, 0)),
                   pl.BlockSpec((n, CONV_F - 1, TF), lambda f: (0, 0, f)),
                   pl.BlockSpec((n, CONV_F - 1, TF), lambda f: (0, 0, f))],
        out_shape=[jax.ShapeDtypeStruct((n, d), F32),
                   jax.ShapeDtypeStruct((n, CONV_F - 1, D_FF), F32),
                   jax.ShapeDtypeStruct((n, CONV_F - 1, D_FF), F32)],
        scratch_shapes=[pltpu.VMEM((n, d), BF16), pltpu.VMEM((n, d), F32)],
        compiler_params=_cparams(1),
        name="ffn_step",
    )(*args)
    return y, jnp.concatenate([ng, nv], axis=-1)


def _rope_tables(pos0, t):
    half = HD_A // 2
    pos = (pos0 + jnp.arange(t)).astype(F32)
    inv = ROPE_THETA ** (-jnp.arange(half, dtype=F32) / half)
    ang = pos[:, None] * inv[None, :]
    cos, sin = jnp.cos(ang), jnp.sin(ang)
    zero = jnp.zeros_like(sin)
    cos_h = jnp.concatenate([cos, cos], axis=-1)
    sin_up = jnp.concatenate([-sin, zero], axis=-1)
    sin_dn = jnp.concatenate([zero, sin], axis=-1)
    two = lambda a: jnp.concatenate([a, a], axis=-1)
    return two(cos_h), two(sin_up), two(sin_dn)


def _rope(x, cos, sin_up, sin_dn):
    w = x.shape[1]
    reps = w // LANES
    tile = lambda a: jnp.concatenate([a] * reps, axis=1) if reps > 1 else a
    x_up = pltpu.roll(x, w - HD_A // 2, axis=1)
    x_dn = pltpu.roll(x, HD_A // 2, axis=1)
    return x * tile(cos) + x_up * tile(sin_up) + x_dn * tile(sin_dn)


def _dup_half(a, half):
    lane = lax.broadcasted_iota(jnp.int32, a.shape, 1)
    sw = pltpu.roll(a, HD_A, axis=1)
    lo = lane < HD_A
    return jnp.where(lo, a, sw) if half == 0 else jnp.where(lo, sw, a)


def _sink_softmax(s, sk_rows):
    m = jnp.maximum(jnp.max(s, axis=-1, keepdims=True), sk_rows)
    p = jnp.exp(s - m)
    return p, jnp.sum(p, axis=-1, keepdims=True) + jnp.exp(sk_rows - m)


def _swa_prompt_kernel(sinks_ref, q_ref, kc_ref, kp_ref, vc_ref, vp_ref,
                       cc_ref, suc_ref, sdc_ref, cp_ref, sup_ref, sdp_ref, *rest, cast_side):
    att_ref, knew_ref, vnew_ref = rest[cast_side:cast_side + 3]
    for src, dst in zip(rest[:cast_side], rest[cast_side + 3:]):
        dst[...] = src[...].astype(BF16)
    c = pl.program_id(1)
    nb = pl.num_programs(1)
    bq = q_ref.shape[0]
    q = _rope(q_ref[...], cc_ref[...], suc_ref[...], sdc_ref[...])
    kc = _rope(kc_ref[...], cc_ref[...], suc_ref[...], sdc_ref[...])
    kp = _rope(kp_ref[...], cp_ref[...], sup_ref[...], sdp_ref[...])
    vc = vc_ref[...]
    vp = vp_ref[...]

    @pl.when(c == nb - 1)
    def _():
        knew_ref[...] = kc
        vnew_ref[...] = vc

    rows = GROUP_A * bq
    r = lax.broadcasted_iota(jnp.int32, (rows, 2 * bq), 0) % bq
    col = lax.broadcasted_iota(jnp.int32, (rows, 2 * bq), 1)
    valid = (col >= r) & (col <= r + WINDOW) & ((c > 0) | (col >= bq))
    lane = lax.broadcasted_iota(jnp.int32, (bq, LANES), 1)
    lo = lane < HD_A
    row_head = lax.broadcasted_iota(jnp.int32, (rows, 1), 0) // bq

    for h in range(N_KV_A):
        t, half = divmod(h, 2)
        ksl = slice(t * LANES, (t + 1) * LANES)
        kd = _dup_half(jnp.concatenate([kp[:, ksl], kc[:, ksl]], axis=0), half).astype(BF16)
        vd = _dup_half(jnp.concatenate([vp[:, ksl], vc[:, ksl]], axis=0), half).astype(BF16)
        parts = []
        for p in range(2):
            q2 = q[:, (2 * h + p) * LANES:(2 * h + p + 1) * LANES]
            parts.append(jnp.where(lo, q2, 0.0))
            parts.append(jnp.where(lo, 0.0, q2))
        qs = (jnp.concatenate(parts, axis=0) * (HD_A ** -0.5)).astype(BF16)
        s = lax.dot_general(qs, kd, (((1,), (1,)), ((), ())), preferred_element_type=F32)
        s = jnp.where(valid, s, -jnp.inf)
        sk = jnp.zeros((rows, 1), F32)
        for g in range(GROUP_A):
            sk = jnp.where(row_head == g, sinks_ref[h * GROUP_A + g], sk)
        pr, den = _sink_softmax(s, sk)
        o = jnp.dot(pr.astype(BF16), vd, preferred_element_type=F32) / den
        for p in range(2):
            o_lo = o[(2 * p) * bq:(2 * p + 1) * bq, :]
            o_hi = o[(2 * p + 1) * bq:(2 * p + 2) * bq, :]
            att_ref[:, (2 * h + p) * LANES:(2 * h + p + 1) * LANES] = jnp.where(lo, o_lo, o_hi).astype(BF16)


def swa_prompt(proj, sinks, nbatch, seq_len, cast_f32=None):
    nb = seq_len // BLOCK_A
    p3 = proj.reshape(nbatch, seq_len, IN_A)
    kcol = (QA + 2 * D_RNN) // KA
    cos, sup, sdn = _rope_tables(0, seq_len)
    prev = lambda c: jnp.maximum(c - 1, 0)
    tab_c = pl.BlockSpec((BLOCK_A, LANES), lambda b, c: (c, 0))
    tab_p = pl.BlockSpec((BLOCK_A, LANES), lambda b, c: (prev(c), 0))
    in_specs = [pl.BlockSpec(memory_space=pltpu.SMEM),
                pl.BlockSpec((None, BLOCK_A, QA), lambda b, c: (b, c, 0)),
                pl.BlockSpec((None, BLOCK_A, KA), lambda b, c: (b, c, kcol)),
                pl.BlockSpec((None, BLOCK_A, KA), lambda b, c: (b, prev(c), kcol)),
                pl.BlockSpec((None, BLOCK_A, KA), lambda b, c: (b, c, kcol + 1)),
                pl.BlockSpec((None, BLOCK_A, KA), lambda b, c: (b, prev(c), kcol + 1)),
                tab_c, tab_c, tab_c, tab_p, tab_p, tab_p]
    args = [sinks, p3, p3, p3, p3, p3, cos, sup, sdn, cos, sup, sdn]
    out_specs = [pl.BlockSpec((None, BLOCK_A, QA), lambda b, c: (b, c, 0)),
                 pl.BlockSpec((None, BLOCK_A, KA), lambda b, c: (b, 0, 0)),
                 pl.BlockSpec((None, BLOCK_A, KA), lambda b, c: (b, 0, 0))]
    out_shape = [jax.ShapeDtypeStruct((nbatch, seq_len, QA), BF16),
                 jax.ShapeDtypeStruct((nbatch, BLOCK_A, KA), F32),
                 jax.ShapeDtypeStruct((nbatch, BLOCK_A, KA), F32)]
    if cast_f32 is not None:
        _add_cast_job(cast_f32, nbatch * nb, lambda b, c: b * nb + c, in_specs, args, out_specs, out_shape)
    outs = pl.pallas_call(
        functools.partial(_swa_prompt_kernel, cast_side=len(cast_f32 or ())),
        grid=(nbatch, nb),
        in_specs=in_specs,
        out_specs=out_specs,
        out_shape=out_shape,
        compiler_params=_cparams(2),
        name="swa_prompt",
    )(*args)
    att, knew, vnew = outs[:3]
    return att.reshape(nbatch * seq_len, QA), knew, vnew, tuple(outs[3:])


def _swa_sample_kernel(sinks_ref, q_ref, k_ref, v_ref, kc_ref, vc_ref, cos_ref, sup_ref, sdn_ref,
                       att_ref, knew_ref, vnew_ref):
    w = kc_ref.shape[1]
    lane = lax.broadcasted_iota(jnp.int32, (1, LANES), 1)
    lo = lane < HD_A
    row = lax.broadcasted_iota(jnp.int32, (SUBLANES, 1), 0)
    scale = HD_A ** -0.5
    todo = []
    for z in range(q_ref.shape[0]):
        q = _rope(q_ref[z], cos_ref[...], sup_ref[...], sdn_ref[...])
        k = _rope(k_ref[z], cos_ref[...], sup_ref[...], sdn_ref[...])
        v = v_ref[z]
        knew_ref[z, 0:w - 1, :] = kc_ref[z, 1:w, :]
        knew_ref[z, w - 1:w, :] = k
        vnew_ref[z, 0:w - 1, :] = vc_ref[z, 1:w, :]
        vnew_ref[z, w - 1:w, :] = v
        for h in range(N_KV_A):
            t, half = divmod(h, 2)
            ksl = slice(t * LANES, (t + 1) * LANES)
            kd = _dup_half(kc_ref[z, :, ksl], half)
            kd_new = _dup_half(k[:, ksl], half)
            parts = []
            for p in range(2):
                q2 = q[:, (2 * h + p) * LANES:(2 * h + p + 1) * LANES]
                parts.append(jnp.where(lo, q2, 0.0))
                parts.append(jnp.where(lo, 0.0, q2))
            parts.append(jnp.zeros((SUBLANES - GROUP_A, LANES), F32))
            qs = jnp.concatenate(parts, axis=0)
            s = lax.dot_general(qs.astype(BF16), kd.astype(BF16), (((1,), (1,)), ((), ())),
                                preferred_element_type=F32) * scale
            s_new = jnp.sum(qs * kd_new, axis=-1, keepdims=True) * scale
            todo.append((z, h, s, s_new, v))
    probs = []
    for z, h, s, s_new, v in todo:
        sk = jnp.zeros((SUBLANES, 1), F32)
        for g in range(GROUP_A):
            sk = jnp.where(row == g, sinks_ref[h * GROUP_A + g], sk)
        m = jnp.maximum(jnp.maximum(jnp.max(s, axis=-1, keepdims=True), s_new), sk)
        pr = jnp.exp(s - m)
        p_new = jnp.exp(s_new - m)
        den = jnp.sum(pr, axis=-1, keepdims=True) + p_new + jnp.exp(sk - m)
        probs.append((z, h, (pr / den).astype(BF16), p_new / den, v))
    for z, h, pr, p_new, v in probs:
        t, half = divmod(h, 2)
        ksl = slice(t * LANES, (t + 1) * LANES)
        vd = _dup_half(vc_ref[z, :, ksl], half)
        vd_new = _dup_half(v[:, ksl], half)
        o = jnp.dot(pr, vd.astype(BF16), preferred_element_type=F32) + p_new * vd_new
        for p in range(2):
            att_ref[z, :, (2 * h + p) * LANES:(2 * h + p + 1) * LANES] = jnp.where(
                lo, o[2 * p:2 * p + 1, :], o[2 * p + 1:2 * p + 2, :]).astype(BF16)


def swa_sample(proj, cache_k, cache_v, sinks):
    n = proj.shape[0]
    nl, _, w = cache_k[0].shape[:3]
    p3 = proj.reshape(n, 1, IN_A)
    kcol = (QA + 2 * D_RNN) // KA
    ck_spec, ck = _layer((cache_k[0].reshape(nl, n, w, KA), cache_k[1]), (SAMPLES_PER_STEP, w, KA),
                         lambda z: (z, 0, 0))
    cv_spec, cv = _layer((cache_v[0].reshape(nl, n, w, KA), cache_v[1]), (SAMPLES_PER_STEP, w, KA),
                         lambda z: (z, 0, 0))
    cos, sup, sdn = _rope_tables(PAST_LEN, 1)
    tab = pl.BlockSpec((1, LANES), lambda z: (0, 0))
    gs = SAMPLES_PER_STEP
    assert n % gs == 0
    att, knew, vnew = pl.pallas_call(
        _swa_sample_kernel,
        grid=(n // gs,),
        in_specs=[pl.BlockSpec(memory_space=pltpu.SMEM),
                  pl.BlockSpec((gs, 1, QA), lambda z: (z, 0, 0)),
                  pl.BlockSpec((gs, 1, KA), lambda z: (z, 0, kcol)),
                  pl.BlockSpec((gs, 1, KA), lambda z: (z, 0, kcol + 1)),
                  ck_spec, cv_spec,
                  tab, tab, tab],
        out_specs=[pl.BlockSpec((gs, 1, QA), lambda z: (z, 0, 0)),
                   pl.BlockSpec((gs, w, KA), lambda z: (z, 0, 0)),
                   pl.BlockSpec((gs, w, KA), lambda z: (z, 0, 0))],
        out_shape=[jax.ShapeDtypeStruct((n, 1, QA), BF16),
                   jax.ShapeDtypeStruct((n, w, KA), F32),
                   jax.ShapeDtypeStruct((n, w, KA), F32)],
        compiler_params=_cparams(1),
        name="swa_sample",
    )(sinks, p3, p3, p3, ck, cv, cos, sup, sdn)
    return att.reshape(n, QA), knew, vnew


def _lru_gates(xc, wa_ref, wx_ref, ba_ref, bx_ref, lam_ref):
    rs, is_ = [], []
    for c in range(D_RNN // LRU_GROUP):
        xg = xc[:, c * LRU_GROUP:(c + 1) * LRU_GROUP].astype(BF16)
        rs.append(jnp.dot(xg, wa_ref[c], preferred_element_type=F32))
        is_.append(jnp.dot(xg, wx_ref[c], preferred_element_type=F32))
    r = _sigmoid(jnp.concatenate(rs, axis=1) + ba_ref[...])
    i = _sigmoid(jnp.concatenate(is_, axis=1) + bx_ref[...])
    log_a = LRU_C * r * _log_sigmoid(lam_ref[...])
    a = jnp.exp(log_a)
    b = jnp.sqrt(-jnp.tanh(log_a) * (a * a + 1.0)) * (i * xc)
    return a, b


def _lru_seq_kernel(xr_ref, gr_ref, cw_ref, cb_ref, wa_ref, wx_ref, ba_ref, bx_ref, lam_ref, *rest, cast_side):
    rec_ref, hlast_ref, cnew_ref = rest[cast_side:cast_side + 3]
    xbuf_ref, a_ref, b_ref, h_ref, hc_ref = rest[2 * cast_side + 3:]
    for src, dst in zip(rest[:cast_side], rest[cast_side + 3:2 * cast_side + 3]):
        dst[...] = src[...].astype(BF16)
    t = pl.program_id(1)
    tt = xr_ref.shape[0]

    @pl.when(t == 0)
    def _():
        xbuf_ref[0:SUBLANES, :] = jnp.zeros((SUBLANES, D_RNN), F32)
        hc_ref[...] = jnp.zeros_like(hc_ref)

    xr = xr_ref[...]
    xbuf_ref[SUBLANES:SUBLANES + tt, :] = xr
    cw = cw_ref[...]
    xc = xr * cw[CONV_B - 1:CONV_B, :] + cb_ref[...]
    for j in range(CONV_B - 1):
        off = SUBLANES - (CONV_B - 1) + j
        xc = xc + xbuf_ref[off:off + tt, :] * cw[j:j + 1, :]
    xbuf_ref[0:SUBLANES, :] = xr[tt - SUBLANES:tt, :]
    cnew_ref[...] = xr[tt - (CONV_B - 1):tt, :]

    a, b = _lru_gates(xc, wa_ref, wx_ref, ba_ref, bx_ref, lam_ref)
    a_ref[...] = a
    b_ref[...] = b
    row = lax.broadcasted_iota(jnp.int32, (SUBLANES, D_RNN), 0)

    def body(g, h):
        base = pl.multiple_of(g * SUBLANES, SUBLANES)
        a8 = a_ref[pl.ds(base, SUBLANES), :]
        b8 = b_ref[pl.ds(base, SUBLANES), :]
        for s in (1, 2, 4):
            a_sh = pltpu.roll(a8, s, axis=0)
            b_sh = pltpu.roll(b8, s, axis=0)
            keep = row >= s
            b8 = jnp.where(keep, a8 * b_sh + b8, b8)
            a8 = jnp.where(keep, a8 * a_sh, a8)
        h8 = b8 + a8 * h
        h_ref[pl.ds(base, SUBLANES), :] = h8
        return h8[SUBLANES - 1:SUBLANES, :]

    h_last = lax.fori_loop(0, tt // SUBLANES, body, hc_ref[...])
    hc_ref[...] = h_last
    hlast_ref[...] = h_last
    rec_ref[...] = (h_ref[...] * _gelu_tanh(gr_ref[...])).astype(BF16)


def _lru_blockdiag(w):
    per = LRU_GROUP // BS_B
    w4 = w.reshape(D_RNN // LRU_GROUP, per, BS_B, BS_B)
    eye = jnp.eye(per, dtype=w.dtype)
    return jnp.einsum('cpij,pq->cpiqj', w4, eye).reshape(D_RNN // LRU_GROUP, LRU_GROUP, LRU_GROUP).astype(BF16)


def lru_seq(proj, nbatch, seq_len, cw, cb, wa_bd, wx_bd, ba, bx, lam, cast_f32=None):
    p3 = proj.reshape(nbatch, seq_len, IN_A)
    tt = min(LRU_TT, seq_len)
    nt = seq_len // tt
    vec = lambda o: _layer(o, (1, D_RNN), lambda b, t: (0, 0))
    wspec = lambda o: _layer(o, o[0].shape[1:], lambda b, t: (0, 0, 0))
    in_specs, args = _unzip([
        (pl.BlockSpec((None, tt, D_RNN), lambda b, t: (b, t, QA // D_RNN)), p3),
        (pl.BlockSpec((None, tt, D_RNN), lambda b, t: (b, t, QA // D_RNN + 1)), p3),
        _layer(cw, (CONV_B, D_RNN), lambda b, t: (0, 0)),
        vec(cb), wspec(wa_bd), wspec(wx_bd), vec(ba), vec(bx), vec(lam)])
    out_specs = [pl.BlockSpec((None, tt, D_RNN), lambda b, t: (b, t, 0)),
                 pl.BlockSpec((None, 1, D_RNN), lambda b, t: (b, 0, 0)),
                 pl.BlockSpec((None, CONV_B - 1, D_RNN), lambda b, t: (b, 0, 0))]
    out_shape = [jax.ShapeDtypeStruct((nbatch, seq_len, D_RNN), BF16),
                 jax.ShapeDtypeStruct((nbatch, 1, D_RNN), F32),
                 jax.ShapeDtypeStruct((nbatch, CONV_B - 1, D_RNN), F32)]
    if cast_f32 is not None:
        _add_cast_job(cast_f32, nbatch * nt, lambda b, t: b * nt + t, in_specs, args, out_specs, out_shape)
    outs = pl.pallas_call(
        functools.partial(_lru_seq_kernel, cast_side=len(cast_f32 or ())),
        grid=(nbatch, nt),
        in_specs=in_specs,
        out_specs=out_specs,
        out_shape=out_shape,
        scratch_shapes=[pltpu.VMEM((tt + SUBLANES, D_RNN), F32),
                        pltpu.VMEM((tt, D_RNN), F32),
                        pltpu.VMEM((tt, D_RNN), F32),
                        pltpu.VMEM((tt, D_RNN), F32),
                        pltpu.VMEM((1, D_RNN), F32)],
        compiler_params=_cparams(2),
        name="lru_seq",
    )(*args)
    rec, hlast, cnew = outs[:3]
    return rec.reshape(nbatch * seq_len, D_RNN), hlast.reshape(nbatch, D_RNN), cnew, tuple(outs[3:])


def _lru_step_kernel(xr_ref, gr_ref, past_ref, h0_ref, cw_ref, cb_ref,
                     wa_ref, wx_ref, ba_ref, bx_ref, lam_ref, rec_ref, h_ref, cnew_ref):
    cw = cw_ref[...]
    xr = xr_ref[...]
    xc = xr * cw[CONV_B - 1:CONV_B, :] + cb_ref[...]
    for r in range(CONV_B - 1):
        p = past_ref[:, r, :]
        xc = xc + p * cw[r:r + 1, :]
        if r > 0:
            cnew_ref[:, r - 1, :] = p
    cnew_ref[:, CONV_B - 2, :] = xr
    a, b = _lru_gates(xc, wa_ref, wx_ref, ba_ref, bx_ref, lam_ref)
    h = b + a * h0_ref[...]
    h_ref[...] = h
    rec_ref[...] = (h * _gelu_tanh(gr_ref[...])).astype(BF16)


def lru_step(proj, h0, conv0, cw, cb, wa_bd, wx_bd, ba, bx, lam):
    n = proj.shape[0]
    col = lambda j: pl.BlockSpec((n, D_RNN), lambda i: (0, j))
    vec = lambda o: _layer(o, (1, D_RNN), lambda i: (0, 0))
    wspec = lambda o: _layer(o, o[0].shape[1:], lambda i: (0, 0, 0))
    hist = pl.BlockSpec((n, CONV_B - 1, D_RNN), lambda i: (0, 0, 0))
    in_specs, args = _unzip([
        (col(QA // D_RNN), proj), (col(QA // D_RNN + 1), proj),
        _layer(conv0, (n, CONV_B - 1, D_RNN), lambda i: (0, 0, 0)),
        _layer(h0, (n, D_RNN), lambda i: (0, 0)),
        _layer(cw, (CONV_B, D_RNN), lambda i: (0, 0)),
        vec(cb), wspec(wa_bd), wspec(wx_bd), vec(ba), vec(bx), vec(lam)])
    rec, h, conv_new = pl.pallas_call(
        _lru_step_kernel,
        grid=(1,),
        in_specs=in_specs,
        out_specs=[col(0), col(0), hist],
        out_shape=[jax.ShapeDtypeStruct((n, D_RNN), BF16), jax.ShapeDtypeStruct((n, D_RNN), F32),
                   jax.ShapeDtypeStruct((n, CONV_B - 1, D_RNN), F32)],
        compiler_params=_cparams(1),
        name="lru_step",
    )(*args)
    return rec, h, conv_new


def _cumsum_lanes(x):
    n = x.shape[1]
    lane = lax.broadcasted_iota(jnp.int32, x.shape, 1)
    s = 1
    while s < n:
        x = x + jnp.where(lane >= s, pltpu.roll(x, s, axis=1), 0.0)
        s *= 2
    return x


def _cummax_lanes(x):
    n = x.shape[1]
    lane = lax.broadcasted_iota(jnp.int32, x.shape, 1)
    s = 1
    while s < n:
        x = jnp.maximum(x, jnp.where(lane >= s, pltpu.roll(x, s, axis=1), -jnp.inf))
        s *= 2
    return x


def _mlstm_seq_kernel(q_ref, k_ref, v_ref, o_ref, gate_ref, gbias_ref, bo_ref, gn_ref,
                      out_ref, c_ref, n_ref, m_ref, cx_ref):
    ci = pl.program_id(1)
    nh = N_HEADS_C
    ln = q_ref.shape[0]

    @pl.when(ci == 0)
    def _():
        cx_ref[...] = jnp.zeros_like(cx_ref)
        m_ref[...] = jnp.zeros_like(m_ref)

    gate_t = (gate_ref[...] + gbias_ref[...]).T
    i_rows = gate_t[0:nh, :]
    b_rows = _cumsum_lanes(_log_sigmoid(gate_t[nh:2 * nh, :]))
    m_prev = m_ref[:, 0:1]
    r_rows = i_rows - b_rows
    mt_rows = b_rows + jnp.maximum(m_prev, _cummax_lanes(r_rows))
    m_last = mt_rows[:, ln - 1:ln]
    b_last = b_rows[:, ln - 1:ln]
    decay = jnp.exp(b_last + m_prev - m_last)
    per_token = jnp.concatenate([
        b_rows - mt_rows,
        jnp.exp(b_rows + m_prev - mt_rows),
        jnp.exp(-mt_rows),
        jnp.exp(b_last - b_rows + i_rows - m_last),
        jnp.zeros((LANES - 4 * nh, ln), F32)], axis=0)
    cols = per_token.T
    m_ref[...] = jnp.broadcast_to(m_last, m_ref.shape)

    ri = lax.broadcasted_iota(jnp.int32, (ln, ln), 0)
    cj = lax.broadcasted_iota(jnp.int32, (ln, ln), 1)
    causal = cj <= ri
    ones_l = jnp.ones((ln, LANES), BF16)
    ones_v = jnp.ones((DV_C, LANES), BF16)
    two = lambda a: jnp.concatenate([a] * (DV_C // LANES), axis=1)

    for h in range(nh):
        qterm = cols[:, h:h + 1]
        inter_b = jnp.broadcast_to(cols[:, nh + h:nh + h + 1], (ln, LANES))
        emt_b = jnp.broadcast_to(cols[:, 2 * nh + h:2 * nh + h + 1], (ln, LANES))
        w_col = cols[:, 3 * nh + h:3 * nh + h + 1]
        qb = q_ref[:, h * DK_C:(h + 1) * DK_C].astype(BF16)
        kh = k_ref[:, h * DK_C:(h + 1) * DK_C] * (DK_C ** -0.5)
        vx = jnp.concatenate([v_ref[:, h * DV_C:(h + 1) * DV_C].astype(BF16), ones_l], axis=1)

        s = lax.dot_general(qb, kh.astype(BF16), (((1,), (1,)), ((), ())), preferred_element_type=F32)
        amat = (jnp.exp(jnp.where(causal, qterm + r_rows[h:h + 1, :], -jnp.inf)) * s).astype(BF16)
        cx = cx_ref[h]
        qc = jnp.dot(qb, cx.astype(BF16), preferred_element_type=F32)
        av = jnp.dot(amat, vx, preferred_element_type=F32)
        num = two(inter_b) * qc[:, :DV_C] + av[:, :DV_C]
        den_b = inter_b * qc[:, DV_C:] + av[:, DV_C:]
        hh = num * two(1.0 / jnp.maximum(jnp.abs(den_b), emt_b))

        kw = (kh * w_col).astype(BF16)
        cx_ref[h] = decay[h:h + 1, :] * cx + lax.dot_general(kw, vx, (((0,), (0,)), ((), ())),
                                                             preferred_element_type=F32)

        vsl = slice(h * DV_C, (h + 1) * DV_C)
        ssq_b = jnp.dot((hh * hh).astype(BF16), ones_v, preferred_element_type=F32)
        hn = hh * two(lax.rsqrt(ssq_b * (1.0 / DV_C) + EPS)) * gn_ref[:, vsl]
        og = _sigmoid(o_ref[:, vsl] + bo_ref[:, vsl])
        out_ref[:, vsl] = (og * hn).astype(BF16)

    @pl.when(ci == pl.num_programs(1) - 1)
    def _():
        for h in range(nh):
            cx = cx_ref[h]
            c_ref[h] = cx[:, :DV_C]
            n_ref[h:h + 1, :] = cx[:, DV_C:].T[0:1, :]


def mlstm_seq(proj, gates, nbatch, seq_len, gbias, b_o, g_norm):
    ln = min(MLSTM_L, seq_len)
    p3 = proj.reshape(nbatch, seq_len, IN_C_MAIN)
    g3 = gates.reshape(nbatch, seq_len, GATE_PAD)
    in_specs, args = _unzip([
        (pl.BlockSpec((None, ln, QC), lambda b, c: (b, c, 0)), p3),
        (pl.BlockSpec((None, ln, QC), lambda b, c: (b, c, 1)), p3),
        (pl.BlockSpec((None, ln, VC), lambda b, c: (b, c, 2 * QC // VC)), p3),
        (pl.BlockSpec((None, ln, VC), lambda b, c: (b, c, 2 * QC // VC + 1)), p3),
        (pl.BlockSpec((None, ln, GATE_PAD), lambda b, c: (b, c, 0)), g3),
        _layer(gbias, (1, GATE_PAD), lambda b, c: (0, 0)),
        _layer(b_o, (1, VC), lambda b, c: (0, 0)),
        _layer(g_norm, (1, VC), lambda b, c: (0, 0))])
    out, c, n, m = pl.pallas_call(
        _mlstm_seq_kernel,
        grid=(nbatch, seq_len // ln),
        in_specs=in_specs,
        out_specs=[pl.BlockSpec((None, ln, VC), lambda b, c: (b, c, 0)),
                   pl.BlockSpec((None, N_HEADS_C, DK_C, DV_C), lambda b, c: (b, 0, 0, 0)),
                   pl.BlockSpec((None, N_HEADS_C, DK_C), lambda b, c: (b, 0, 0)),
                   pl.BlockSpec((None, N_HEADS_C, LANES), lambda b, c: (b, 0, 0))],
        out_shape=[jax.ShapeDtypeStruct((nbatch, seq_len, VC), BF16),
                   jax.ShapeDtypeStruct((nbatch, N_HEADS_C, DK_C, DV_C), F32),
                   jax.ShapeDtypeStruct((nbatch, N_HEADS_C, DK_C), F32),
                   jax.ShapeDtypeStruct((nbatch, N_HEADS_C, LANES), F32)],
        scratch_shapes=[pltpu.VMEM((N_HEADS_C, DK_C, DV_C + LANES), F32)],
        compiler_params=_cparams(2),
        name="mlstm_seq",
    )(*args)
    return out.reshape(nbatch * seq_len, VC), c, n, m[:, :, 0]


def _mlstm_step_kernel(q_ref, k_ref, v_ref, o_ref, gate_ref, gbias_ref, bo_ref, gn_ref, c0_ref, n0_ref, m0_ref,
                       out_ref, c_ref, n_ref, m_ref):
    q8 = q_ref[...]
    k8 = k_ref[...] * (DK_C ** -0.5)
    v8 = v_ref[...]
    gate = gate_ref[...] + gbias_ref[...]
    i8 = gate[:, 0:1]
    lf8 = _log_sigmoid(gate[:, 1:2])
    g8 = lf8 + m0_ref[...]
    mt = jnp.maximum(g8, i8)
    inter = jnp.exp(g8 - mt)
    wl = jnp.exp(i8 - mt)
    a8 = wl * jnp.sum(q8 * k8, axis=1, keepdims=True)
    row = lax.broadcasted_iota(jnp.int32, (N_HEADS_C, 1), 0)
    qb = q8.astype(BF16)
    kw = k8 * wl
    qc = jnp.zeros((N_HEADS_C, DV_C), F32)
    for h in range(N_HEADS_C):
        qc = qc + jnp.where(row == h, jnp.dot(qb, c0_ref[h].astype(BF16), preferred_element_type=F32), 0.0)
    vb = v8.astype(BF16)
    for h in range(N_HEADS_C):
        kz = jnp.where(row == h, kw, 0.0).astype(BF16)
        c_ref[h] = inter[h:h + 1, :] * c0_ref[h] + lax.dot_general(kz, vb, (((0,), (0,)), ((), ())),
                                                                   preferred_element_type=F32)
    n0 = n0_ref[...]
    num = inter * qc + a8 * v8
    den = inter * jnp.sum(q8 * n0, axis=1, keepdims=True) + a8
    hh = num / jnp.maximum(jnp.abs(den), jnp.exp(-mt))
    n_ref[...] = inter * n0 + kw
    m_ref[...] = mt
    hn = _rms(hh, gn_ref[...])
    og = _sigmoid(o_ref[...] + bo_ref[...])
    out_ref[...] = (og * hn).astype(BF16)


def mlstm_step(proj, gates, gbias, b_o, g_norm, c0, n0, m0):
    n = proj.shape[0]
    h = N_HEADS_C
    q = proj[:, :QC].reshape(n, h, DK_C)
    k = proj[:, QC:2 * QC].reshape(n, h, DK_C)
    v = proj[:, 2 * QC:2 * QC + VC].reshape(n, h, DV_C)
    o = proj[:, 2 * QC + VC:].reshape(n, h, DV_C)
    gcols = jnp.swapaxes(gates[:, :2 * h].reshape(n, 2, h), 1, 2)
    gb = jnp.swapaxes(gbias[0][gbias[1], 0, :2 * h].reshape(2, h), 0, 1)
    nl = c0[0].shape[0]
    per = lambda *tail: pl.BlockSpec((None,) + tail, lambda z: (z,) + (0,) * len(tail))
    lper = lambda o_, *tail: _layer(o_, (None,) + tail, lambda z: (z,) + (0,) * len(tail))
    lshared = lambda o_, *shape: _layer(o_, shape, lambda z: (0,) * len(shape))
    in_specs, args = _unzip([
        (per(h, DK_C), q), (per(h, DK_C), k), (per(h, DV_C), v), (per(h, DV_C), o), (per(h, 2), gcols),
        (pl.BlockSpec((h, 2), lambda z: (0, 0)), gb),
        lshared((b_o[0].reshape(nl, h, DV_C), b_o[1]), h, DV_C),
        lshared((g_norm[0].reshape(nl, h, DV_C), g_norm[1]), h, DV_C),
        lper(c0, h, DK_C, DV_C), lper(n0, h, DK_C), lper((m0[0].reshape(nl, n, h, 1), m0[1]), h, 1)])
    out, c, nn, m = pl.pallas_call(
        _mlstm_step_kernel,
        grid=(n,),
        in_specs=in_specs,
        out_specs=[per(h, DV_C), per(h, DK_C, DV_C), per(h, DK_C), per(h, 1)],
        out_shape=[jax.ShapeDtypeStruct((n, h, DV_C), BF16),
                   jax.ShapeDtypeStruct((n, h, DK_C, DV_C), F32),
                   jax.ShapeDtypeStruct((n, h, DK_C), F32),
                   jax.ShapeDtypeStruct((n, h, 1), F32)],
        compiler_params=_cparams(1),
        name="mlstm_step",
    )(*args)
    return out.reshape(n, VC), c, nn, m.reshape(n, h)


def _prep_weights(w):
    a_in = w['a_w_in']
    q, k, v, xr, gr = jnp.split(a_in, [QA, QA + KA, QA + 2 * KA, QA + 2 * KA + D_RNN], axis=-1)
    c_in = w['c_w_in']
    gate_w = jnp.pad(c_in[..., IN_C_MAIN:], ((0, 0), (0, 0), (0, GATE_PAD - 2 * N_HEADS_C)))
    nc = c_in.shape[0]
    gbias = jnp.concatenate([w['c_b_i'], w['c_b_f'], jnp.zeros((nc, GATE_PAD - 2 * N_HEADS_C), F32)], axis=-1)
    return {
        'a_w_in': jnp.concatenate([q, xr, gr, k, v], axis=-1).astype(BF16),
        'a_w_out': w['a_w_out'].astype(BF16),
        'c_w_in': None,
        'c_w_gate': gate_w.astype(BF16),
        'c_gbias': gbias.reshape(nc, 1, GATE_PAD),
        'c_w_out': w['c_w_out'].astype(BF16),
        'ffn_w_up': [],
        'ffn_w_down': [],
        'ffn_cp': _ffn_conv_params(w['ffn_conv_w'], w['ffn_conv_b']),
        'lru_wa': jax.vmap(_lru_blockdiag)(w['a_lru_wa']),
        'lru_wx': jax.vmap(_lru_blockdiag)(w['a_lru_wx']),
    }


def _trunk(x, seq_len, state, w, pw):
    m = x.shape[0]
    nbatch = m // seq_len
    outs = {'k': [], 'v': [], 'h': [], 'conv': [], 'C': [], 'n': [], 'm': [], 'f': []}

    def vec(name, idx):
        a = w[name]
        return a.reshape(a.shape[0], 1, a.shape[1]), idx

    for l in range(DEPTH):
        j = l // 2
        if l % 2 == 0:
            proj = norm_matmul(x, vec('norm_mix_pre', l), (pw['a_w_in'], j), IN_A, _proj_tn(m, IN_A))
            lru_args = ((w['a_conv_w'], j), vec('a_conv_b', j), (pw['lru_wa'], j), (pw['lru_wx'], j),
                        vec('a_lru_ba', j), vec('a_lru_bx', j), vec('a_lru_lambda', j))
            if state is None:
                first = ((w['ffn_w_up'], 0, 2 * D_FF), (w['ffn_w_down'], 0, D_MODEL)) if l == 0 else None
                att, k_new, v_new, cast = swa_prompt(proj, w['a_sinks'][j], nbatch, seq_len, cast_f32=first)
                if cast:
                    pw['ffn_w_up'].append(cast[0])
                    pw['ffn_w_down'].append(cast[1])
                c_in = w['c_w_in']
                job = ((c_in.reshape(-1, c_in.shape[-1]), None, IN_C_MAIN),) if l == 0 else None
                rec, h_new, c_new, cast = lru_seq(proj, nbatch, seq_len, *lru_args, cast_f32=job)
                if cast:
                    pw['c_w_in'] = cast[0].reshape(c_in.shape[0], c_in.shape[1], IN_C_MAIN)
            else:
                att, k_new, v_new = swa_sample(proj, (state['cache_k'], j), (state['cache_v'], j), w['a_sinks'][j])
                rec, h_new, c_new = lru_step(proj, (state['lru_h'], j), (state['lru_conv'], j), *lru_args)
            outs['k'].append(k_new.reshape(nbatch, -1, N_KV_A, HD_A))
            outs['v'].append(v_new.reshape(nbatch, -1, N_KV_A, HD_A))
            outs['h'].append(h_new)
            outs['conv'].append(c_new)
            x = out_proj([att, rec], (pw['a_w_out'], j), x, vec('norm_mix_post', l))
        else:
            proj, gates = norm_matmul(x, vec('norm_mix_pre', l), (pw['c_w_in'], j), IN_C_MAIN,
                                      _proj_tn(m, IN_C_MAIN), w_side=(pw['c_w_gate'], j))
            mix_args = ((pw['c_gbias'], j), vec('c_b_o', j), vec('c_norm', j))
            if state is None:
                hg, c_new, n_new, m_new = mlstm_seq(proj, gates, nbatch, seq_len, *mix_args)
            else:
                hg, c_new, n_new, m_new = mlstm_step(proj, gates, *mix_args, (state['mlstm_C'], j),
                                                     (state['mlstm_n'], j), (state['mlstm_m'], j))
            outs['C'].append(c_new)
            outs['n'].append(n_new)
            outs['m'].append(m_new)
            x = out_proj([hg], (pw['c_w_out'], j), x, vec('norm_mix_post', l))
        if state is None:
            nxt = ((w['ffn_w_up'], l + 1), (w['ffn_w_down'], l + 1)) if l + 1 < DEPTH else None
            x, f_new, cast = ffn_seq(x, seq_len, vec('norm_ffn_pre', l), pw['ffn_w_up'][l], (pw['ffn_cp'], l),
                                     pw['ffn_w_down'][l], vec('norm_ffn_post', l), next_f32=nxt)
            if cast:
                pw['ffn_w_up'].append(cast[0])
                pw['ffn_w_down'].append(cast[1])
        else:
            x, f_new = ffn_step(x, (state['ffn_conv'], l), vec('norm_ffn_pre', l), pw['ffn_w_up'][l],
                                (w['ffn_conv_w'], l), vec('ffn_conv_b', l), pw['ffn_w_down'][l],
                                vec('norm_ffn_post', l))
        outs['f'].append(f_new)
    return x, {name: jnp.stack(vals) for name, vals in outs.items()}


def kernel(x_prompt, x_sample, cache_k, cache_v, state_lru_h, state_lru_conv, state_mlstm_C, state_mlstm_n, state_mlstm_m, state_ffn_conv, norm_mix_pre, norm_mix_post, norm_ffn_pre, norm_ffn_post, a_w_in, a_sinks, a_conv_w, a_conv_b, a_lru_wa, a_lru_ba, a_lru_wx, a_lru_bx, a_lru_lambda, a_w_out, c_w_in, c_b_i, c_b_f, c_b_o, c_norm, c_w_out, ffn_w_up, ffn_conv_w, ffn_conv_b, ffn_w_down):
    w = {
        'norm_mix_pre': norm_mix_pre, 'norm_mix_post': norm_mix_post,
        'norm_ffn_pre': norm_ffn_pre, 'norm_ffn_post': norm_ffn_post,
        'a_w_in': a_w_in, 'a_sinks': a_sinks, 'a_conv_w': a_conv_w, 'a_conv_b': a_conv_b,
        'a_lru_wa': a_lru_wa, 'a_lru_ba': a_lru_ba, 'a_lru_wx': a_lru_wx, 'a_lru_bx': a_lru_bx,
        'a_lru_lambda': a_lru_lambda, 'a_w_out': a_w_out,
        'c_w_in': c_w_in, 'c_b_i': c_b_i, 'c_b_f': c_b_f, 'c_b_o': c_b_o, 'c_norm': c_norm, 'c_w_out': c_w_out,
        'ffn_w_up': ffn_w_up, 'ffn_conv_w': ffn_conv_w, 'ffn_conv_b': ffn_conv_b, 'ffn_w_down': ffn_w_down,
    }
    pw = _prep_weights(w)
    bp, sp, d = x_prompt.shape
    bs, ss, _ = x_sample.shape
    assert ss == 1, "the sample group advances one token per step"
    y_p, st_p = _trunk(x_prompt.reshape(bp * sp, d), sp, None, w, pw)
    state = {'cache_k': cache_k, 'cache_v': cache_v, 'lru_h': state_lru_h, 'lru_conv': state_lru_conv,
             'mlstm_C': state_mlstm_C, 'mlstm_n': state_mlstm_n, 'mlstm_m': state_mlstm_m,
             'ffn_conv': state_ffn_conv}
    y_s, st_s = _trunk(x_sample.reshape(bs * ss, d), 1, state, w, pw)
    return (y_p.reshape(bp, sp, d), y_s.reshape(bs, ss, d),
            st_p['k'], st_s['k'], st_p['v'], st_s['v'], st_p['h'], st_s['h'],
            st_p['conv'], st_s['conv'], st_p['C'], st_s['C'], st_p['n'], st_s['n'],
            st_p['m'], st_s['m'], st_p['f'], st_s['f'])
```

```python
import functools
import math

import jax
import jax.numpy as jnp
from jax import lax
from jax.experimental import pallas as pl
from jax.experimental.pallas import tpu as pltpu

F32 = jnp.float32
BF16 = jnp.bfloat16

D_MODEL = 2048
DEPTH = 4
PAST_LEN = 16384
N_HEADS_A = 16
N_KV_A = 4
HD_A = 64
GROUP_A = N_HEADS_A // N_KV_A
WINDOW = 128
BLOCK_A = 128
ROPE_THETA = 10000.0
D_RNN = 1024
N_BLOCKS_B = 16
BS_B = D_RNN // N_BLOCKS_B
CONV_B = 4
LRU_C = 8.0
N_HEADS_C = 8
DK_C = 128
DV_C = 256
D_FF = 6144
CONV_F = 3
EPS = 1e-6
QA = N_HEADS_A * HD_A
KA = N_KV_A * HD_A
IN_A = QA + 2 * KA + 2 * D_RNN
QC = N_HEADS_C * DK_C
VC = N_HEADS_C * DV_C
IN_C_MAIN = 2 * QC + 2 * VC

LANES = 128
SUBLANES = 8
VMEM_LIMIT_BYTES = 60 * 1024 * 1024

TM = 512
TM_PROJ = 1024
TF = 1024
FFN_SUB = 1024
LRU_TT = 256
MLSTM_L = 256
LRU_GROUP = 256
GATE_PAD = LANES
SAMPLES_PER_STEP = 4


def _cparams(n_axes):
    return pltpu.CompilerParams(dimension_semantics=("arbitrary",) * n_axes,
                                vmem_limit_bytes=VMEM_LIMIT_BYTES)


def _rms(x, g):
    ms = jnp.mean(x * x, axis=-1, keepdims=True)
    return x * lax.rsqrt(ms + EPS) * g


def _gelu_tanh(x):
    c = math.sqrt(2.0 / math.pi)
    hx = 0.5 * x
    return hx + hx * jnp.tanh(x * (c + (c * 0.044715) * (x * x)))


def _sigmoid(x):
    return 0.5 + 0.5 * jnp.tanh(0.5 * x)


def _log_sigmoid(x):
    return jnp.minimum(x, 0.0) - jnp.log1p(jnp.exp(-jnp.abs(x)))


def _norm_matmul_kernel(x_ref, g_ref, w_ref, *rest, with_side):
    o_ref, hn_ref = rest[1 if with_side else 0], rest[-1]

    @pl.when(pl.program_id(1) == 0)
    def _():
        hn_ref[...] = _rms(x_ref[...], g_ref[...]).astype(BF16)
        if with_side:
            rest[2][...] = jnp.dot(hn_ref[...], rest[0][...], preferred_element_type=F32)

    o_ref[...] = jnp.dot(hn_ref[...], w_ref[...], preferred_element_type=F32)


def _proj_tn(m, n):
    del m
    budget = 8 * 1024 * 1024
    best = LANES
    for tn in range(LANES, n + 1, LANES):
        if n % tn == 0 and D_MODEL * tn * 2 <= budget:
            best = tn
    return best


def _layer(opnd, block, imap):
    arr, idx = opnd
    return pl.BlockSpec((None,) + tuple(block), lambda *g: (idx,) + tuple(imap(*g))), arr


def _unzip(pairs):
    return [p[0] for p in pairs], [p[1] for p in pairs]


def _cast_slabs(srcs, dsts, valid):
    for src, dst, nv in zip(srcs, dsts, valid):
        if nv == 'interleave':
            for c in range(D_FF // TF):
                dst[:, 2 * c * TF:(2 * c + 1) * TF] = src[:, c * TF:(c + 1) * TF].astype(BF16)
                dst[:, (2 * c + 1) * TF:(2 * c + 2) * TF] = src[:, D_FF + c * TF:D_FF + (c + 1) * TF].astype(BF16)
            continue
        x = src[...]
        if nv is not None:
            x = jnp.where(lax.broadcasted_iota(jnp.int32, x.shape, 1) < nv, x, 0.0)
        dst[...] = x.astype(BF16)


def _add_cast_job(jobs, nsteps, step_of, in_specs, args, out_specs, out_shape):
    for arr, layer, ncols, cblk in jobs:
        rows = arr.shape[-2]
        rw, rem = divmod(rows, nsteps)
        assert rem == 0 and rw % (2 * SUBLANES) == 0 and ncols % LANES == 0, (rows, nsteps, ncols)
        if layer is None:
            in_specs.append(pl.BlockSpec((rw, ncols), lambda *g, _c=cblk: (step_of(*g), _c)))
        else:
            in_specs.append(pl.BlockSpec((None, rw, ncols), lambda *g, _l=layer, _c=cblk: (_l, step_of(*g), _c)))
        args.append(arr)
        out_specs.append(pl.BlockSpec((rw, ncols), lambda *g: (step_of(*g), 0)))
        out_shape.append(jax.ShapeDtypeStruct((rows, ncols), BF16))


def norm_matmul(x, g, w, n, tn, w_side=None):
    m, k = x.shape
    assert n % tn == 0 and n <= w[0].shape[-1]
    tm = min(TM_PROJ, m)
    specs, args = _unzip([(pl.BlockSpec((tm, k), lambda i, j: (i, 0)), x),
                          _layer(g, (1, k), lambda i, j: (0, 0)),
                          _layer(w, (k, tn), lambda i, j: (0, j))])
    in_specs = list(specs)
    out_specs = [pl.BlockSpec((tm, tn), lambda i, j: (i, j))]
    out_shape = [jax.ShapeDtypeStruct((m, n), F32)]
    if w_side is not None:
        ns = w_side[0].shape[-1]
        spec, arr = _layer(w_side, (k, ns), lambda i, j: (0, 0))
        in_specs.append(spec)
        out_specs.append(pl.BlockSpec((tm, ns), lambda i, j: (i, 0)))
        out_shape.append(jax.ShapeDtypeStruct((m, ns), F32))
        args.append(arr)
    outs = pl.pallas_call(
        functools.partial(_norm_matmul_kernel, with_side=w_side is not None),
        grid=(m // tm, n // tn),
        in_specs=in_specs,
        out_specs=out_specs,
        out_shape=out_shape,
        scratch_shapes=[pltpu.VMEM((tm, k), BF16)],
        compiler_params=_cparams(2),
        name="norm_matmul",
    )(*args)
    return outs[0] if w_side is None else outs


def _out_proj_kernel(*refs, n_in):
    a_refs = refs[:n_in]
    w_ref, x_ref, g_ref, o_ref = refs[n_in:]
    acc = None
    row = 0
    for a_ref in a_refs:
        ka = a_ref.shape[1]
        part = jnp.dot(a_ref[...], w_ref[row:row + ka, :], preferred_element_type=F32)
        acc = part if acc is None else acc + part
        row += ka
    o_ref[...] = x_ref[...] + _rms(acc, g_ref[...])


def out_proj(inputs, w, x, g):
    m, d = x.shape
    tm = min(TM, m)
    w_spec, w_arr = _layer(w, w[0].shape[1:], lambda i: (0, 0))
    g_spec, g_arr = _layer(g, (1, d), lambda i: (0, 0))
    in_specs = [pl.BlockSpec((tm, a.shape[1]), lambda i: (i, 0)) for a in inputs]
    in_specs += [w_spec, pl.BlockSpec((tm, d), lambda i: (i, 0)), g_spec]
    return pl.pallas_call(
        functools.partial(_out_proj_kernel, n_in=len(inputs)),
        grid=(m // tm,),
        in_specs=in_specs,
        out_specs=pl.BlockSpec((tm, d), lambda i: (i, 0)),
        out_shape=jax.ShapeDtypeStruct((m, d), F32),
        compiler_params=_cparams(1),
        name="out_proj",
    )(*inputs, w_arr, x, g_arr)


def _ffn_conv_params(convw, convb):
    nl = convw.shape[0]
    p = jnp.concatenate([convw, convb[:, None, :], jnp.zeros((nl, SUBLANES - CONV_F - 1, 2 * D_FF), F32)], axis=1)
    return jnp.swapaxes(p.reshape(nl, SUBLANES, 2 * D_FF // TF, TF), 1, 2)


def _ffn_conv(u, prev, cp):
    cw = cp[0:CONV_F, :]
    cb = cp[CONV_F:CONV_F + 1, :]
    y = pltpu.roll(u, 2, axis=0) * cw[0:1, :] + pltpu.roll(u, 1, axis=0) * cw[1:2, :] + u * cw[2:3, :] + cb
    head = jnp.concatenate([prev, u[0:SUBLANES, :]], axis=0)
    y_head = (head[SUBLANES - 2:2 * SUBLANES - 2, :] * cw[0:1, :] + head[SUBLANES - 1:2 * SUBLANES - 1, :] * cw[1:2, :]
              + head[SUBLANES:, :] * cw[2:3, :] + cb)
    return jnp.concatenate([y_head, y[SUBLANES:, :]], axis=0)


def _ffn_seq_kernel(x_ref, gpre_ref, wgv_ref, cp_ref, wd_ref, gpost_ref,
                    *rest, tiles_per_seq, cast_next, n_chunks):
    if cast_next:
        wupn_ref, wdn_ref, o_ref, nu_ref, wupn_out, wdn_out, hn_ref, carry_ref, act_ref = rest
    else:
        o_ref, nu_ref, hn_ref, carry_ref, act_ref = rest
    i = pl.program_id(0)
    f = pl.program_id(1)
    nf = n_chunks
    tm = x_ref.shape[0]

    @pl.when((i % tiles_per_seq == 0) & (f < nf))
    def _():
        carry_ref[f] = jnp.zeros(carry_ref.shape[1:], F32)

    def up_part():
        if cast_next:
            wupn_out[...] = wupn_ref[...].astype(BF16)
            wdn_out[...] = wdn_ref[...].astype(BF16)
        u = jnp.dot(hn_ref[...], wgv_ref[...], preferred_element_type=F32)
        tf = u.shape[1] // 2
        ug, uv = u[:, :tf], u[:, tf:]
        cg = _ffn_conv(ug, carry_ref[f, 0], cp_ref[f])
        cv = _ffn_conv(uv, carry_ref[f, 1], cp_ref[f + nf])
        carry_ref[f, 0] = ug[tm - SUBLANES:tm, :]
        carry_ref[f, 1] = uv[tm - SUBLANES:tm, :]
        nu_ref[0] = ug[tm - (CONV_F - 1):tm, :]
        nu_ref[1] = uv[tm - (CONV_F - 1):tm, :]
        act_ref[f % 2] = (_gelu_tanh(cg) * cv).astype(BF16)

    def down_part():
        o_ref[...] += jnp.dot(act_ref[(f + 1) % 2], wd_ref[...], preferred_element_type=F32)

    @pl.when(f == 0)
    def _():
        hn_ref[...] = _rms(x_ref[...], gpre_ref[...]).astype(BF16)
        o_ref[...] = jnp.zeros_like(o_ref)
        up_part()

    @pl.when((f > 0) & (f < nf))
    def _():
        down_part()
        up_part()

    @pl.when(f == nf)
    def _():
        down_part()
        o_ref[...] = x_ref[...] + _rms(o_ref[...], gpost_ref[...])


def ffn_seq(x, seq_len, gpre, wup, cp, wdown, gpost, next_f32=None):
    m, d = x.shape
    tm = min(TM, seq_len)
    tps = seq_len // tm
    nb = m // seq_len
    nf = D_FF // TF
    nsteps = (m // tm) * nf
    cast_next = next_f32 is not None
    kern = functools.partial(_ffn_seq_kernel, tiles_per_seq=tps, cast_next=cast_next, n_chunks=nf)
    up = lambda f: jnp.minimum(f, nf - 1)
    dn = lambda f: jnp.maximum(f - 1, 0)
    in_specs, args = _unzip([
        (pl.BlockSpec((tm, d), lambda i, f: (i, 0)), x),
        _layer(gpre, (1, d), lambda i, f: (0, 0)),
        (pl.BlockSpec((d, 2 * TF), lambda i, f: (0, up(f))), wup),
        _layer(cp, cp[0].shape[1:], lambda i, f: (0, 0, 0)),
        (pl.BlockSpec((TF, d), lambda i, f: (dn(f), 0)), wdown),
        _layer(gpost, (1, d), lambda i, f: (0, 0))])
    out_specs = [pl.BlockSpec((tm, d), lambda i, f: (i, 0)),
                 pl.BlockSpec((None, 2, CONV_F - 1, TF), lambda i, f: (i, 0, 0, up(f)))]
    out_shape = [jax.ShapeDtypeStruct((m, d), F32),
                 jax.ShapeDtypeStruct((m // tm, 2, CONV_F - 1, D_FF), F32)]
    if cast_next:
        cw, rem_c = divmod(2 * D_FF, nsteps)
        rw, rem_r = divmod(D_FF, nsteps)
        assert rem_c == 0 and rem_r == 0 and cw % LANES == 0 and rw % (2 * SUBLANES) == 0, (nsteps, cw, rw)
        up_in, up_arr = _layer(next_f32[0], (d, cw), lambda i, f: (0, i * nf + up(f)))
        dn_in, dn_arr = _layer(next_f32[1], (rw, d), lambda i, f: (i * nf + up(f), 0))
        in_specs += [up_in, dn_in]
        args += [up_arr, dn_arr]
        per, half = TF // cw, D_FF // cw
        assert TF % cw == 0

        def up_dst(i, f):
            s = i * nf + up(f)
            return 0, ((s % half) // per) * (2 * per) + (s // half) * per + s % per

        out_specs += [pl.BlockSpec((d, cw), up_dst),
                      pl.BlockSpec((rw, d), lambda i, f: (i * nf + up(f), 0))]
        out_shape += [jax.ShapeDtypeStruct((d, 2 * D_FF), BF16), jax.ShapeDtypeStruct((D_FF, d), BF16)]
    outs = pl.pallas_call(
        kern,
        grid=(m // tm, nf + 1),
        in_specs=in_specs,
        out_specs=out_specs,
        out_shape=out_shape,
        scratch_shapes=[pltpu.VMEM((tm, d), BF16),
                        pltpu.VMEM((nf, 2, SUBLANES, TF), F32),
                        pltpu.VMEM((2, tm, TF), BF16)],
        compiler_params=_cparams(2),
        name="ffn_seq",
    )(*args)
    y, nu = outs[:2]
    last = nu.reshape(nb, tps, 2, CONV_F - 1, D_FF)[:, tps - 1]
    last = jnp.swapaxes(last, 1, 2).reshape(nb, CONV_F - 1, 2 * D_FF)
    return y, last, tuple(outs[2:])


def _ffn_step_kernel(x_ref, gpre_ref, wg_ref, wv_ref, cwg_ref, cwv_ref, cbg_ref, cbv_ref, wd_ref, gpost_ref,
                     pg_ref, pv_ref,
                     o_ref, ng_ref, nv_ref, hn_ref, acc_ref):
    f = pl.program_id(0)
    nf = pl.num_programs(0)

    @pl.when(f == 0)
    def _():
        hn_ref[...] = _rms(x_ref[...], gpre_ref[...]).astype(BF16)
        acc_ref[...] = jnp.zeros_like(acc_ref)

    hn = hn_ref[...]
    ug = jnp.dot(hn, wg_ref[...], preferred_element_type=F32)
    uv = jnp.dot(hn, wv_ref[...], preferred_element_type=F32)
    p0g, p1g = pg_ref[:, 0, :], pg_ref[:, 1, :]
    p0v, p1v = pv_ref[:, 0, :], pv_ref[:, 1, :]
    ng_ref[:, 0, :] = p1g
    ng_ref[:, 1, :] = ug
    nv_ref[:, 0, :] = p1v
    nv_ref[:, 1, :] = uv
    cwg = cwg_ref[...]
    cwv = cwv_ref[...]
    cg = p0g * cwg[0:1, :] + p1g * cwg[1:2, :] + ug * cwg[2:3, :] + cbg_ref[...]
    cv = p0v * cwv[0:1, :] + p1v * cwv[1:2, :] + uv * cwv[2:3, :] + cbv_ref[...]
    act = (_gelu_tanh(cg) * cv).astype(BF16)
    acc_ref[...] += jnp.dot(act, wd_ref[...], preferred_element_type=F32)

    @pl.when(f == nf - 1)
    def _():
        o_ref[...] = x_ref[...] + _rms(acc_ref[...], gpost_ref[...])


def ffn_step(x, past, gpre, wup, convw, convb, wdown, gpost):
    n, d = x.shape
    nf = D_FF // TF
    phalf = lambda c: _layer(past, (n, CONV_F - 1, TF), lambda f: (0, 0, f + c * nf))
    in_specs, args = _unzip([
        (pl.BlockSpec((n, d), lambda f: (0, 0)), x),
        _layer(gpre, (1, d), lambda f: (0, 0)),
        (pl.BlockSpec((d, TF), lambda f: (0, 2 * f)), wup),
        (pl.BlockSpec((d, TF), lambda f: (0, 2 * f + 1)), wup),
        _layer(convw, (CONV_F, TF), lambda f: (0, f)),
        _layer(convw, (CONV_F, TF), lambda f: (0, f + nf)),
        _layer(convb, (1, TF), lambda f: (0, f)),
        _layer(convb, (1, TF), lambda f: (0, f + nf)),
        (pl.BlockSpec((TF, d), lambda f: (f, 0)), wdown),
        _layer(gpost, (1, d), lambda f: (0, 0)),
        phalf(0), phalf(1)])
    y, ng, nv = pl.pallas_call(
        _ffn_step_kernel,
        grid=(nf,),
        in_specs=in_specs,
        out_specs=[pl.BlockSpec((n, d), lambda f: (0, 0)),
                   pl.BlockSpec((n, CONV_F - 1, TF), lambda f: (0, 0, f)),
                   pl.BlockSpec((n, CONV_F - 1, TF), lambda f: (0, 0, f))],
        out_shape=[jax.ShapeDtypeStruct((n, d), F32),
                   jax.ShapeDtypeStruct((n, CONV_F - 1, D_FF), F32),
                   jax.ShapeDtypeStruct((n, CONV_F - 1, D_FF), F32)],
        scratch_shapes=[pltpu.VMEM((n, d), BF16), pltpu.VMEM((n, d), F32)],
        compiler_params=_cparams(1),
        name="ffn_step",
    )(*args)
    return y, jnp.concatenate([ng, nv], axis=-1)


def _rope_tables(pos0, t):
    half = HD_A // 2
    pos = (pos0 + jnp.arange(t)).astype(F32)
    inv = ROPE_THETA ** (-jnp.arange(half, dtype=F32) / half)
    ang = pos[:, None] * inv[None, :]
    cos, sin = jnp.cos(ang), jnp.sin(ang)
    zero = jnp.zeros_like(sin)
    cos_h = jnp.concatenate([cos, cos], axis=-1)
    sin_up = jnp.concatenate([-sin, zero], axis=-1)
    sin_dn = jnp.concatenate([zero, sin], axis=-1)
    two = lambda a: jnp.concatenate([a, a], axis=-1)
    return two(cos_h), two(sin_up), two(sin_dn)


def _rope(x, cos, sin_up, sin_dn):
    w = x.shape[1]
    reps = w // LANES
    tile = lambda a: jnp.concatenate([a] * reps, axis=1) if reps > 1 else a
    x_up = pltpu.roll(x, w - HD_A // 2, axis=1)
    x_dn = pltpu.roll(x, HD_A // 2, axis=1)
    return x * tile(cos) + x_up * tile(sin_up) + x_dn * tile(sin_dn)


def _dup_half(a, half):
    lane = lax.broadcasted_iota(jnp.int32, a.shape, 1)
    sw = pltpu.roll(a, HD_A, axis=1)
    lo = lane < HD_A
    return jnp.where(lo, a, sw) if half == 0 else jnp.where(lo, sw, a)


def _sink_softmax(s, sk_rows):
    m = jnp.maximum(jnp.max(s, axis=-1, keepdims=True), sk_rows)
    p = jnp.exp(s - m)
    return p, jnp.sum(p, axis=-1, keepdims=True) + jnp.exp(sk_rows - m)


def _swa_prompt_kernel(sinks_ref, q_ref, kc_ref, kp_ref, vc_ref, vp_ref,
                       cc_ref, suc_ref, sdc_ref, cp_ref, sup_ref, sdp_ref, *rest, cast_valid):
    nc = len(cast_valid)
    att_ref, knew_ref, vnew_ref = rest[nc:nc + 3]
    _cast_slabs(rest[:nc], rest[nc + 3:], cast_valid)
    c = pl.program_id(1)
    nb = pl.num_programs(1)
    bq = q_ref.shape[0]
    q = _rope(q_ref[...], cc_ref[...], suc_ref[...], sdc_ref[...])
    kc = _rope(kc_ref[...], cc_ref[...], suc_ref[...], sdc_ref[...])
    kp = _rope(kp_ref[...], cp_ref[...], sup_ref[...], sdp_ref[...])
    vc = vc_ref[...]
    vp = vp_ref[...]

    @pl.when(c == nb - 1)
    def _():
        knew_ref[...] = kc
        vnew_ref[...] = vc

    rows = GROUP_A * bq
    r = lax.broadcasted_iota(jnp.int32, (rows, 2 * bq), 0) % bq
    col = lax.broadcasted_iota(jnp.int32, (rows, 2 * bq), 1)
    valid = (col >= r) & (col <= r + WINDOW) & ((c > 0) | (col >= bq))
    lane = lax.broadcasted_iota(jnp.int32, (bq, LANES), 1)
    lo = lane < HD_A
    row_head = lax.broadcasted_iota(jnp.int32, (rows, 1), 0) // bq

    for h in range(N_KV_A):
        t, half = divmod(h, 2)
        ksl = slice(t * LANES, (t + 1) * LANES)
        kd = _dup_half(jnp.concatenate([kp[:, ksl], kc[:, ksl]], axis=0), half).astype(BF16)
        vd = _dup_half(jnp.concatenate([vp[:, ksl], vc[:, ksl]], axis=0), half).astype(BF16)
        parts = []
        for p in range(2):
            q2 = q[:, (2 * h + p) * LANES:(2 * h + p + 1) * LANES]
            parts.append(jnp.where(lo, q2, 0.0))
            parts.append(jnp.where(lo, 0.0, q2))
        qs = (jnp.concatenate(parts, axis=0) * (HD_A ** -0.5)).astype(BF16)
        s = lax.dot_general(qs, kd, (((1,), (1,)), ((), ())), preferred_element_type=F32)
        s = jnp.where(valid, s, -jnp.inf)
        sk = jnp.zeros((rows, 1), F32)
        for g in range(GROUP_A):
            sk = jnp.where(row_head == g, sinks_ref[h * GROUP_A + g], sk)
        pr, den = _sink_softmax(s, sk)
        o = jnp.dot(pr.astype(BF16), vd, preferred_element_type=F32) / den
        for p in range(2):
            o_lo = o[(2 * p) * bq:(2 * p + 1) * bq, :]
            o_hi = o[(2 * p + 1) * bq:(2 * p + 2) * bq, :]
            att_ref[:, (2 * h + p) * LANES:(2 * h + p + 1) * LANES] = jnp.where(lo, o_lo, o_hi).astype(BF16)


def swa_prompt(proj, sinks, nbatch, seq_len, cast_f32=None, cast_valid=()):
    nb = seq_len // BLOCK_A
    p3 = proj.reshape(nbatch, seq_len, IN_A)
    kcol = (QA + 2 * D_RNN) // KA
    cos, sup, sdn = _rope_tables(0, seq_len)
    prev = lambda c: jnp.maximum(c - 1, 0)
    tab_c = pl.BlockSpec((BLOCK_A, LANES), lambda b, c: (c, 0))
    tab_p = pl.BlockSpec((BLOCK_A, LANES), lambda b, c: (prev(c), 0))
    in_specs = [pl.BlockSpec(memory_space=pltpu.SMEM),
                pl.BlockSpec((None, BLOCK_A, QA), lambda b, c: (b, c, 0)),
                pl.BlockSpec((None, BLOCK_A, KA), lambda b, c: (b, c, kcol)),
                pl.BlockSpec((None, BLOCK_A, KA), lambda b, c: (b, prev(c), kcol)),
                pl.BlockSpec((None, BLOCK_A, KA), lambda b, c: (b, c, kcol + 1)),
                pl.BlockSpec((None, BLOCK_A, KA), lambda b, c: (b, prev(c), kcol + 1)),
                tab_c, tab_c, tab_c, tab_p, tab_p, tab_p]
    args = [sinks, p3, p3, p3, p3, p3, cos, sup, sdn, cos, sup, sdn]
    out_specs = [pl.BlockSpec((None, BLOCK_A, QA), lambda b, c: (b, c, 0)),
                 pl.BlockSpec((None, BLOCK_A, KA), lambda b, c: (b, 0, 0)),
                 pl.BlockSpec((None, BLOCK_A, KA), lambda b, c: (b, 0, 0))]
    out_shape = [jax.ShapeDtypeStruct((nbatch, seq_len, QA), BF16),
                 jax.ShapeDtypeStruct((nbatch, BLOCK_A, KA), F32),
                 jax.ShapeDtypeStruct((nbatch, BLOCK_A, KA), F32)]
    if cast_f32 is not None:
        _add_cast_job(cast_f32, nbatch * nb, lambda b, c: b * nb + c, in_specs, args, out_specs, out_shape)
    outs = pl.pallas_call(
        functools.partial(_swa_prompt_kernel, cast_valid=tuple(cast_valid)),
        grid=(nbatch, nb),
        in_specs=in_specs,
        out_specs=out_specs,
        out_shape=out_shape,
        compiler_params=_cparams(2),
        name="swa_prompt",
    )(*args)
    att, knew, vnew = outs[:3]
    return att.reshape(nbatch * seq_len, QA), knew, vnew, tuple(outs[3:])


def _swa_sample_kernel(sinks_ref, q_ref, k_ref, v_ref, kc_ref, vc_ref, cos_ref, sup_ref, sdn_ref,
                       att_ref, knew_ref, vnew_ref):
    w = kc_ref.shape[1]
    lane = lax.broadcasted_iota(jnp.int32, (1, LANES), 1)
    lo = lane < HD_A
    row = lax.broadcasted_iota(jnp.int32, (SUBLANES, 1), 0)
    scale = HD_A ** -0.5
    todo = []
    for z in range(q_ref.shape[0]):
        q = _rope(q_ref[z], cos_ref[...], sup_ref[...], sdn_ref[...])
        k = _rope(k_ref[z], cos_ref[...], sup_ref[...], sdn_ref[...])
        v = v_ref[z]
        knew_ref[z, 0:w - 1, :] = kc_ref[z, 1:w, :]
        knew_ref[z, w - 1:w, :] = k
        vnew_ref[z, 0:w - 1, :] = vc_ref[z, 1:w, :]
        vnew_ref[z, w - 1:w, :] = v
        for h in range(N_KV_A):
            t, half = divmod(h, 2)
            ksl = slice(t * LANES, (t + 1) * LANES)
            kd = _dup_half(kc_ref[z, :, ksl], half)
            kd_new = _dup_half(k[:, ksl], half)
            parts = []
            for p in range(2):
                q2 = q[:, (2 * h + p) * LANES:(2 * h + p + 1) * LANES]
                parts.append(jnp.where(lo, q2, 0.0))
                parts.append(jnp.where(lo, 0.0, q2))
            parts.append(jnp.zeros((SUBLANES - GROUP_A, LANES), F32))
            qs = jnp.concatenate(parts, axis=0)
            s = lax.dot_general(qs.astype(BF16), kd.astype(BF16), (((1,), (1,)), ((), ())),
                                preferred_element_type=F32) * scale
            s_new = jnp.sum(qs * kd_new, axis=-1, keepdims=True) * scale
            todo.append((z, h, s, s_new, v))
    probs = []
    for z, h, s, s_new, v in todo:
        sk = jnp.zeros((SUBLANES, 1), F32)
        for g in range(GROUP_A):
            sk = jnp.where(row == g, sinks_ref[h * GROUP_A + g], sk)
        m = jnp.maximum(jnp.maximum(jnp.max(s, axis=-1, keepdims=True), s_new), sk)
        pr = jnp.exp(s - m)
        p_new = jnp.exp(s_new - m)
        den = jnp.sum(pr, axis=-1, keepdims=True) + p_new + jnp.exp(sk - m)
        probs.append((z, h, (pr / den).astype(BF16), p_new / den, v))
    for z, h, pr, p_new, v in probs:
        t, half = divmod(h, 2)
        ksl = slice(t * LANES, (t + 1) * LANES)
        vd = _dup_half(vc_ref[z, :, ksl], half)
        vd_new = _dup_half(v[:, ksl], half)
        o = jnp.dot(pr, vd.astype(BF16), preferred_element_type=F32) + p_new * vd_new
        for p in range(2):
            att_ref[z, :, (2 * h + p) * LANES:(2 * h + p + 1) * LANES] = jnp.where(
                lo, o[2 * p:2 * p + 1, :], o[2 * p + 1:2 * p + 2, :]).astype(BF16)


def swa_sample(proj, cache_k, cache_v, sinks):
    n = proj.shape[0]
    nl, _, w = cache_k[0].shape[:3]
    p3 = proj.reshape(n, 1, IN_A)
    kcol = (QA + 2 * D_RNN) // KA
    ck_spec, ck = _layer((cache_k[0].reshape(nl, n, w, KA), cache_k[1]), (SAMPLES_PER_STEP, w, KA),
                         lambda z: (z, 0, 0))
    cv_spec, cv = _layer((cache_v[0].reshape(nl, n, w, KA), cache_v[1]), (SAMPLES_PER_STEP, w, KA),
                         lambda z: (z, 0, 0))
    cos, sup, sdn = _rope_tables(PAST_LEN, 1)
    tab = pl.BlockSpec((1, LANES), lambda z: (0, 0))
    gs = SAMPLES_PER_STEP
    assert n % gs == 0
    att, knew, vnew = pl.pallas_call(
        _swa_sample_kernel,
        grid=(n // gs,),
        in_specs=[pl.BlockSpec(memory_space=pltpu.SMEM),
                  pl.BlockSpec((gs, 1, QA), lambda z: (z, 0, 0)),
                  pl.BlockSpec((gs, 1, KA), lambda z: (z, 0, kcol)),
                  pl.BlockSpec((gs, 1, KA), lambda z: (z, 0, kcol + 1)),
                  ck_spec, cv_spec,
                  tab, tab, tab],
        out_specs=[pl.BlockSpec((gs, 1, QA), lambda z: (z, 0, 0)),
                   pl.BlockSpec((gs, w, KA), lambda z: (z, 0, 0)),
                   pl.BlockSpec((gs, w, KA), lambda z: (z, 0, 0))],
        out_shape=[jax.ShapeDtypeStruct((n, 1, QA), BF16),
                   jax.ShapeDtypeStruct((n, w, KA), F32),
                   jax.ShapeDtypeStruct((n, w, KA), F32)],
        compiler_params=_cparams(1),
        name="swa_sample",
    )(sinks, p3, p3, p3, ck, cv, cos, sup, sdn)
    return att.reshape(n, QA), knew, vnew


def _lru_gates(xc, wa_ref, wx_ref, ba_ref, bx_ref, lam_ref):
    rs, is_ = [], []
    for c in range(D_RNN // LRU_GROUP):
        xg = xc[:, c * LRU_GROUP:(c + 1) * LRU_GROUP].astype(BF16)
        rs.append(jnp.dot(xg, wa_ref[c], preferred_element_type=F32))
        is_.append(jnp.dot(xg, wx_ref[c], preferred_element_type=F32))
    r = _sigmoid(jnp.concatenate(rs, axis=1) + ba_ref[...])
    i = _sigmoid(jnp.concatenate(is_, axis=1) + bx_ref[...])
    log_a = LRU_C * r * _log_sigmoid(lam_ref[...])
    a = jnp.exp(log_a)
    b = jnp.sqrt(-jnp.tanh(log_a) * (a * a + 1.0)) * (i * xc)
    return a, b


def _lru_seq_kernel(xr_ref, gr_ref, cw_ref, cb_ref, wa_ref, wx_ref, ba_ref, bx_ref, lam_ref, *rest, cast_valid):
    nc = len(cast_valid)
    rec_ref, hlast_ref, cnew_ref = rest[nc:nc + 3]
    xbuf_ref, a_ref, b_ref, h_ref, hc_ref = rest[2 * nc + 3:]
    _cast_slabs(rest[:nc], rest[nc + 3:2 * nc + 3], cast_valid)
    t = pl.program_id(1)
    tt = xr_ref.shape[0]

    @pl.when(t == 0)
    def _():
        xbuf_ref[0:SUBLANES, :] = jnp.zeros((SUBLANES, D_RNN), F32)
        hc_ref[...] = jnp.zeros_like(hc_ref)

    xr = xr_ref[...]
    xbuf_ref[SUBLANES:SUBLANES + tt, :] = xr
    cw = cw_ref[...]
    xc = xr * cw[CONV_B - 1:CONV_B, :] + cb_ref[...]
    for j in range(CONV_B - 1):
        off = SUBLANES - (CONV_B - 1) + j
        xc = xc + xbuf_ref[off:off + tt, :] * cw[j:j + 1, :]
    xbuf_ref[0:SUBLANES, :] = xr[tt - SUBLANES:tt, :]
    cnew_ref[...] = xr[tt - (CONV_B - 1):tt, :]

    a, b = _lru_gates(xc, wa_ref, wx_ref, ba_ref, bx_ref, lam_ref)
    a_ref[...] = a
    b_ref[...] = b
    row = lax.broadcasted_iota(jnp.int32, (SUBLANES, D_RNN), 0)

    def body(g, h):
        base = pl.multiple_of(g * SUBLANES, SUBLANES)
        a8 = a_ref[pl.ds(base, SUBLANES), :]
        b8 = b_ref[pl.ds(base, SUBLANES), :]
        for s in (1, 2, 4):
            a_sh = pltpu.roll(a8, s, axis=0)
            b_sh = pltpu.roll(b8, s, axis=0)
            keep = row >= s
            b8 = jnp.where(keep, a8 * b_sh + b8, b8)
            a8 = jnp.where(keep, a8 * a_sh, a8)
        h8 = b8 + a8 * h
        h_ref[pl.ds(base, SUBLANES), :] = h8
        return h8[SUBLANES - 1:SUBLANES, :]

    h_last = lax.fori_loop(0, tt // SUBLANES, body, hc_ref[...])
    hc_ref[...] = h_last
    hlast_ref[...] = h_last
    rec_ref[...] = (h_ref[...] * _gelu_tanh(gr_ref[...])).astype(BF16)


def _lru_blockdiag(w):
    per = LRU_GROUP // BS_B
    w4 = w.reshape(D_RNN // LRU_GROUP, per, BS_B, BS_B)
    eye = jnp.eye(per, dtype=w.dtype)
    return jnp.einsum('cpij,pq->cpiqj', w4, eye).reshape(D_RNN // LRU_GROUP, LRU_GROUP, LRU_GROUP).astype(BF16)


def lru_seq(proj, nbatch, seq_len, cw, cb, wa_bd, wx_bd, ba, bx, lam, cast_f32=None, cast_valid=()):
    p3 = proj.reshape(nbatch, seq_len, IN_A)
    tt = min(LRU_TT, seq_len)
    nt = seq_len // tt
    vec = lambda o: _layer(o, (1, D_RNN), lambda b, t: (0, 0))
    wspec = lambda o: _layer(o, o[0].shape[1:], lambda b, t: (0, 0, 0))
    in_specs, args = _unzip([
        (pl.BlockSpec((None, tt, D_RNN), lambda b, t: (b, t, QA // D_RNN)), p3),
        (pl.BlockSpec((None, tt, D_RNN), lambda b, t: (b, t, QA // D_RNN + 1)), p3),
        _layer(cw, (CONV_B, D_RNN), lambda b, t: (0, 0)),
        vec(cb), wspec(wa_bd), wspec(wx_bd), vec(ba), vec(bx), vec(lam)])
    out_specs = [pl.BlockSpec((None, tt, D_RNN), lambda b, t: (b, t, 0)),
                 pl.BlockSpec((None, 1, D_RNN), lambda b, t: (b, 0, 0)),
                 pl.BlockSpec((None, CONV_B - 1, D_RNN), lambda b, t: (b, 0, 0))]
    out_shape = [jax.ShapeDtypeStruct((nbatch, seq_len, D_RNN), BF16),
                 jax.ShapeDtypeStruct((nbatch, 1, D_RNN), F32),
                 jax.ShapeDtypeStruct((nbatch, CONV_B - 1, D_RNN), F32)]
    if cast_f32 is not None:
        _add_cast_job(cast_f32, nbatch * nt, lambda b, t: b * nt + t, in_specs, args, out_specs, out_shape)
    outs = pl.pallas_call(
        functools.partial(_lru_seq_kernel, cast_valid=tuple(cast_valid)),
        grid=(nbatch, nt),
        in_specs=in_specs,
        out_specs=out_specs,
        out_shape=out_shape,
        scratch_shapes=[pltpu.VMEM((tt + SUBLANES, D_RNN), F32),
                        pltpu.VMEM((tt, D_RNN), F32),
                        pltpu.VMEM((tt, D_RNN), F32),
                        pltpu.VMEM((tt, D_RNN), F32),
                        pltpu.VMEM((1, D_RNN), F32)],
        compiler_params=_cparams(2),
        name="lru_seq",
    )(*args)
    rec, hlast, cnew = outs[:3]
    return rec.reshape(nbatch * seq_len, D_RNN), hlast.reshape(nbatch, D_RNN), cnew, tuple(outs[3:])


def _lru_step_kernel(xr_ref, gr_ref, past_ref, h0_ref, cw_ref, cb_ref,
                     wa_ref, wx_ref, ba_ref, bx_ref, lam_ref, rec_ref, h_ref, cnew_ref):
    cw = cw_ref[...]
    xr = xr_ref[...]
    xc = xr * cw[CONV_B - 1:CONV_B, :] + cb_ref[...]
    for r in range(CONV_B - 1):
        p = past_ref[:, r, :]
        xc = xc + p * cw[r:r + 1, :]
        if r > 0:
            cnew_ref[:, r - 1, :] = p
    cnew_ref[:, CONV_B - 2, :] = xr
    a, b = _lru_gates(xc, wa_ref, wx_ref, ba_ref, bx_ref, lam_ref)
    h = b + a * h0_ref[...]
    h_ref[...] = h
    rec_ref[...] = (h * _gelu_tanh(gr_ref[...])).astype(BF16)


def lru_step(proj, h0, conv0, cw, cb, wa_bd, wx_bd, ba, bx, lam):
    n = proj.shape[0]
    col = lambda j: pl.BlockSpec((n, D_RNN), lambda i: (0, j))
    vec = lambda o: _layer(o, (1, D_RNN), lambda i: (0, 0))
    wspec = lambda o: _layer(o, o[0].shape[1:], lambda i: (0, 0, 0))
    hist = pl.BlockSpec((n, CONV_B - 1, D_RNN), lambda i: (0, 0, 0))
    in_specs, args = _unzip([
        (col(QA // D_RNN), proj), (col(QA // D_RNN + 1), proj),
        _layer(conv0, (n, CONV_B - 1, D_RNN), lambda i: (0, 0, 0)),
        _layer(h0, (n, D_RNN), lambda i: (0, 0)),
        _layer(cw, (CONV_B, D_RNN), lambda i: (0, 0)),
        vec(cb), wspec(wa_bd), wspec(wx_bd), vec(ba), vec(bx), vec(lam)])
    rec, h, conv_new = pl.pallas_call(
        _lru_step_kernel,
        grid=(1,),
        in_specs=in_specs,
        out_specs=[col(0), col(0), hist],
        out_shape=[jax.ShapeDtypeStruct((n, D_RNN), BF16), jax.ShapeDtypeStruct((n, D_RNN), F32),
                   jax.ShapeDtypeStruct((n, CONV_B - 1, D_RNN), F32)],
        compiler_params=_cparams(1),
        name="lru_step",
    )(*args)
    return rec, h, conv_new


def _cumsum_lanes(x):
    n = x.shape[1]
    lane = lax.broadcasted_iota(jnp.int32, x.shape, 1)
    s = 1
    while s < n:
        x = x + jnp.where(lane >= s, pltpu.roll(x, s, axis=1), 0.0)
        s *= 2
    return x


def _cummax_lanes(x):
    n = x.shape[1]
    lane = lax.broadcasted_iota(jnp.int32, x.shape, 1)
    s = 1
    while s < n:
        x = jnp.maximum(x, jnp.where(lane >= s, pltpu.roll(x, s, axis=1), -jnp.inf))
        s *= 2
    return x


def _mlstm_seq_kernel(q_ref, k_ref, v_ref, o_ref, gate_ref, gbias_ref, bo_ref, gn_ref,
                      out_ref, c_ref, n_ref, m_ref, cx_ref):
    ci = pl.program_id(1)
    nh = N_HEADS_C
    ln = q_ref.shape[0]

    @pl.when(ci == 0)
    def _():
        cx_ref[...] = jnp.zeros_like(cx_ref)
        m_ref[...] = jnp.zeros_like(m_ref)

    gate_t = (gate_ref[...] + gbias_ref[...]).T
    i_rows = gate_t[0:nh, :]
    b_rows = _cumsum_lanes(_log_sigmoid(gate_t[nh:2 * nh, :]))
    m_prev = m_ref[:, 0:1]
    r_rows = i_rows - b_rows
    mt_rows = b_rows + jnp.maximum(m_prev, _cummax_lanes(r_rows))
    m_last = mt_rows[:, ln - 1:ln]
    b_last = b_rows[:, ln - 1:ln]
    decay = jnp.exp(b_last + m_prev - m_last)
    per_token = jnp.concatenate([
        b_rows - mt_rows,
        jnp.exp(b_rows + m_prev - mt_rows),
        jnp.exp(-mt_rows),
        jnp.exp(b_last - b_rows + i_rows - m_last),
        jnp.zeros((LANES - 4 * nh, ln), F32)], axis=0)
    cols = per_token.T
    m_ref[...] = jnp.broadcast_to(m_last, m_ref.shape)

    ri = lax.broadcasted_iota(jnp.int32, (ln, ln), 0)
    cj = lax.broadcasted_iota(jnp.int32, (ln, ln), 1)
    causal = cj <= ri
    ones_l = jnp.ones((ln, LANES), BF16)
    ones_v = jnp.ones((DV_C, LANES), BF16)
    two = lambda a: jnp.concatenate([a] * (DV_C // LANES), axis=1)

    for h in range(nh):
        qterm = cols[:, h:h + 1]
        inter_b = jnp.broadcast_to(cols[:, nh + h:nh + h + 1], (ln, LANES))
        emt_b = jnp.broadcast_to(cols[:, 2 * nh + h:2 * nh + h + 1], (ln, LANES))
        w_col = cols[:, 3 * nh + h:3 * nh + h + 1]
        qb = q_ref[:, h * DK_C:(h + 1) * DK_C].astype(BF16)
        kh = k_ref[:, h * DK_C:(h + 1) * DK_C] * (DK_C ** -0.5)
        vx = jnp.concatenate([v_ref[:, h * DV_C:(h + 1) * DV_C].astype(BF16), ones_l], axis=1)

        s = lax.dot_general(qb, kh.astype(BF16), (((1,), (1,)), ((), ())), preferred_element_type=F32)
        amat = (jnp.exp(jnp.where(causal, qterm + r_rows[h:h + 1, :], -jnp.inf)) * s).astype(BF16)
        cx = cx_ref[h]
        qc = jnp.dot(qb, cx.astype(BF16), preferred_element_type=F32)
        av = jnp.dot(amat, vx, preferred_element_type=F32)
        num = two(inter_b) * qc[:, :DV_C] + av[:, :DV_C]
        den_b = inter_b * qc[:, DV_C:] + av[:, DV_C:]
        hh = num * two(1.0 / jnp.maximum(jnp.abs(den_b), emt_b))

        kw = (kh * w_col).astype(BF16)
        cx_ref[h] = decay[h:h + 1, :] * cx + lax.dot_general(kw, vx, (((0,), (0,)), ((), ())),
                                                             preferred_element_type=F32)

        vsl = slice(h * DV_C, (h + 1) * DV_C)
        ssq_b = jnp.dot((hh * hh).astype(BF16), ones_v, preferred_element_type=F32)
        hn = hh * two(lax.rsqrt(ssq_b * (1.0 / DV_C) + EPS)) * gn_ref[:, vsl]
        og = _sigmoid(o_ref[:, vsl] + bo_ref[:, vsl])
        out_ref[:, vsl] = (og * hn).astype(BF16)

    @pl.when(ci == pl.num_programs(1) - 1)
    def _():
        for h in range(nh):
            cx = cx_ref[h]
            c_ref[h] = cx[:, :DV_C]
            n_ref[h:h + 1, :] = cx[:, DV_C:].T[0:1, :]


def mlstm_seq(proj, gates, nbatch, seq_len, gbias, b_o, g_norm):
    ln = min(MLSTM_L, seq_len)
    p3 = proj.reshape(nbatch, seq_len, IN_C_MAIN)
    g3 = gates.reshape(nbatch, seq_len, GATE_PAD)
    in_specs, args = _unzip([
        (pl.BlockSpec((None, ln, QC), lambda b, c: (b, c, 0)), p3),
        (pl.BlockSpec((None, ln, QC), lambda b, c: (b, c, 1)), p3),
        (pl.BlockSpec((None, ln, VC), lambda b, c: (b, c, 2 * QC // VC)), p3),
        (pl.BlockSpec((None, ln, VC), lambda b, c: (b, c, 2 * QC // VC + 1)), p3),
        (pl.BlockSpec((None, ln, GATE_PAD), lambda b, c: (b, c, 0)), g3),
        _layer(gbias, (1, GATE_PAD), lambda b, c: (0, 0)),
        _layer(b_o, (1, VC), lambda b, c: (0, 0)),
        _layer(g_norm, (1, VC), lambda b, c: (0, 0))])
    out, c, n, m = pl.pallas_call(
        _mlstm_seq_kernel,
        grid=(nbatch, seq_len // ln),
        in_specs=in_specs,
        out_specs=[pl.BlockSpec((None, ln, VC), lambda b, c: (b, c, 0)),
                   pl.BlockSpec((None, N_HEADS_C, DK_C, DV_C), lambda b, c: (b, 0, 0, 0)),
                   pl.BlockSpec((None, N_HEADS_C, DK_C), lambda b, c: (b, 0, 0)),
                   pl.BlockSpec((None, N_HEADS_C, LANES), lambda b, c: (b, 0, 0))],
        out_shape=[jax.ShapeDtypeStruct((nbatch, seq_len, VC), BF16),
                   jax.ShapeDtypeStruct((nbatch, N_HEADS_C, DK_C, DV_C), F32),
                   jax.ShapeDtypeStruct((nbatch, N_HEADS_C, DK_C), F32),
                   jax.ShapeDtypeStruct((nbatch, N_HEADS_C, LANES), F32)],
        scratch_shapes=[pltpu.VMEM((N_HEADS_C, DK_C, DV_C + LANES), F32)],
        compiler_params=_cparams(2),
        name="mlstm_seq",
    )(*args)
    return out.reshape(nbatch * seq_len, VC), c, n, m[:, :, 0]


def _mlstm_step_kernel(q_ref, k_ref, v_ref, o_ref, gate_ref, gbias_ref, bo_ref, gn_ref, c0_ref, n0_ref, m0_ref,
                       out_ref, c_ref, n_ref, m_ref):
    q8 = q_ref[...]
    k8 = k_ref[...] * (DK_C ** -0.5)
    v8 = v_ref[...]
    gate = gate_ref[...] + gbias_ref[...]
    i8 = gate[:, 0:1]
    lf8 = _log_sigmoid(gate[:, 1:2])
    g8 = lf8 + m0_ref[...]
    mt = jnp.maximum(g8, i8)
    inter = jnp.exp(g8 - mt)
    wl = jnp.exp(i8 - mt)
    a8 = wl * jnp.sum(q8 * k8, axis=1, keepdims=True)
    row = lax.broadcasted_iota(jnp.int32, (N_HEADS_C, 1), 0)
    qb = q8.astype(BF16)
    kw = k8 * wl
    qc = jnp.zeros((N_HEADS_C, DV_C), F32)
    for h in range(N_HEADS_C):
        qc = qc + jnp.where(row == h, jnp.dot(qb, c0_ref[h].astype(BF16), preferred_element_type=F32), 0.0)
    vb = v8.astype(BF16)
    for h in range(N_HEADS_C):
        kz = jnp.where(row == h, kw, 0.0).astype(BF16)
        c_ref[h] = inter[h:h + 1, :] * c0_ref[h] + lax.dot_general(kz, vb, (((0,), (0,)), ((), ())),
                                                                   preferred_element_type=F32)
    n0 = n0_ref[...]
    num = inter * qc + a8 * v8
    den = inter * jnp.sum(q8 * n0, axis=1, keepdims=True) + a8
    hh = num / jnp.maximum(jnp.abs(den), jnp.exp(-mt))
    n_ref[...] = inter * n0 + kw
    m_ref[...] = mt
    hn = _rms(hh, gn_ref[...])
    og = _sigmoid(o_ref[...] + bo_ref[...])
    out_ref[...] = (og * hn).astype(BF16)


def mlstm_step(proj, gates, gbias, b_o, g_norm, c0, n0, m0):
    n = proj.shape[0]
    h = N_HEADS_C
    q = proj[:, :QC].reshape(n, h, DK_C)
    k = proj[:, QC:2 * QC].reshape(n, h, DK_C)
    v = proj[:, 2 * QC:2 * QC + VC].reshape(n, h, DV_C)
    o = proj[:, 2 * QC + VC:].reshape(n, h, DV_C)
    gcols = jnp.swapaxes(gates[:, :2 * h].reshape(n, 2, h), 1, 2)
    gb = jnp.swapaxes(gbias[0][gbias[1], 0, :2 * h].reshape(2, h), 0, 1)
    nl = c0[0].shape[0]
    per = lambda *tail: pl.BlockSpec((None,) + tail, lambda z: (z,) + (0,) * len(tail))
    lper = lambda o_, *tail: _layer(o_, (None,) + tail, lambda z: (z,) + (0,) * len(tail))
    lshared = lambda o_, *shape: _layer(o_, shape, lambda z: (0,) * len(shape))
    in_specs, args = _unzip([
        (per(h, DK_C), q), (per(h, DK_C), k), (per(h, DV_C), v), (per(h, DV_C), o), (per(h, 2), gcols),
        (pl.BlockSpec((h, 2), lambda z: (0, 0)), gb),
        lshared((b_o[0].reshape(nl, h, DV_C), b_o[1]), h, DV_C),
        lshared((g_norm[0].reshape(nl, h, DV_C), g_norm[1]), h, DV_C),
        lper(c0, h, DK_C, DV_C), lper(n0, h, DK_C), lper((m0[0].reshape(nl, n, h, 1), m0[1]), h, 1)])
    out, c, nn, m = pl.pallas_call(
        _mlstm_step_kernel,
        grid=(n,),
        in_specs=in_specs,
        out_specs=[per(h, DV_C), per(h, DK_C, DV_C), per(h, DK_C), per(h, 1)],
        out_shape=[jax.ShapeDtypeStruct((n, h, DV_C), BF16),
                   jax.ShapeDtypeStruct((n, h, DK_C, DV_C), F32),
                   jax.ShapeDtypeStruct((n, h, DK_C), F32),
                   jax.ShapeDtypeStruct((n, h, 1), F32)],
        compiler_params=_cparams(1),
        name="mlstm_step",
    )(*args)
    return out.reshape(n, VC), c, nn, m.reshape(n, h)


def _prep_weights(w):
    a_in = w['a_w_in']
    q, k, v, xr, gr = jnp.split(a_in, [QA, QA + KA, QA + 2 * KA, QA + 2 * KA + D_RNN], axis=-1)
    nc = w['c_w_in'].shape[0]
    gbias = jnp.concatenate([w['c_b_i'], w['c_b_f'], jnp.zeros((nc, GATE_PAD - 2 * N_HEADS_C), F32)], axis=-1)
    return {
        'a_w_in': jnp.concatenate([q, xr, gr, k, v], axis=-1).astype(BF16),
        'a_w_out': w['a_w_out'].astype(BF16),
        'c_w_in': None,
        'c_w_gate': None,
        'c_gbias': gbias.reshape(nc, 1, GATE_PAD),
        'c_w_out': w['c_w_out'].astype(BF16),
        'ffn_w_up': [],
        'ffn_w_down': [],
        'ffn_cp': _ffn_conv_params(w['ffn_conv_w'], w['ffn_conv_b']),
        'lru_wa': jax.vmap(_lru_blockdiag)(w['a_lru_wa']),
        'lru_wx': jax.vmap(_lru_blockdiag)(w['a_lru_wx']),
    }


def _trunk(x, seq_len, state, w, pw):
    m = x.shape[0]
    nbatch = m // seq_len
    outs = {'k': [], 'v': [], 'h': [], 'conv': [], 'C': [], 'n': [], 'm': [], 'f': []}

    def vec(name, idx):
        a = w[name]
        return a.reshape(a.shape[0], 1, a.shape[1]), idx

    for l in range(DEPTH):
        j = l // 2
        if l % 2 == 0:
            proj = norm_matmul(x, vec('norm_mix_pre', l), (pw['a_w_in'], j), IN_A, _proj_tn(m, IN_A))
            lru_args = ((w['a_conv_w'], j), vec('a_conv_b', j), (pw['lru_wa'], j), (pw['lru_wx'], j),
                        vec('a_lru_ba', j), vec('a_lru_bx', j), vec('a_lru_lambda', j))
            if state is None:
                first = ((w['ffn_w_up'], 0, 2 * D_FF, 0), (w['ffn_w_down'], 0, D_MODEL, 0)) if l == 0 else None
                att, k_new, v_new, cast = swa_prompt(proj, w['a_sinks'][j], nbatch, seq_len, cast_f32=first,
                                                     cast_valid=('interleave', None) if l == 0 else ())
                if cast:
                    pw['ffn_w_up'].append(cast[0])
                    pw['ffn_w_down'].append(cast[1])
                c_in = w['c_w_in']
                nc, dm, n_in = c_in.shape
                c2 = c_in.reshape(nc * dm, n_in)
                job = ((c2, None, IN_C_MAIN, 0), (c2, None, GATE_PAD, IN_C_MAIN // GATE_PAD)) if l == 0 else None
                valid = (None, n_in - IN_C_MAIN) if l == 0 else ()
                rec, h_new, c_new, cast = lru_seq(proj, nbatch, seq_len, *lru_args, cast_f32=job, cast_valid=valid)
                if cast:
                    pw['c_w_in'] = cast[0].reshape(nc, dm, IN_C_MAIN)
                    pw['c_w_gate'] = cast[1].reshape(nc, dm, GATE_PAD)
            else:
                att, k_new, v_new = swa_sample(proj, (state['cache_k'], j), (state['cache_v'], j), w['a_sinks'][j])
                rec, h_new, c_new = lru_step(proj, (state['lru_h'], j), (state['lru_conv'], j), *lru_args)
            outs['k'].append(k_new.reshape(nbatch, -1, N_KV_A, HD_A))
            outs['v'].append(v_new.reshape(nbatch, -1, N_KV_A, HD_A))
            outs['h'].append(h_new)
            outs['conv'].append(c_new)
            x = out_proj([att, rec], (pw['a_w_out'], j), x, vec('norm_mix_post', l))
        else:
            proj, gates = norm_matmul(x, vec('norm_mix_pre', l), (pw['c_w_in'], j), IN_C_MAIN,
                                      _proj_tn(m, IN_C_MAIN), w_side=(pw['c_w_gate'], j))
            mix_args = ((pw['c_gbias'], j), vec('c_b_o', j), vec('c_norm', j))
            if state is None:
                hg, c_new, n_new, m_new = mlstm_seq(proj, gates, nbatch, seq_len, *mix_args)
            else:
                hg, c_new, n_new, m_new = mlstm_step(proj, gates, *mix_args, (state['mlstm_C'], j),
                                                     (state['mlstm_n'], j), (state['mlstm_m'], j))
            outs['C'].append(c_new)
            outs['n'].append(n_new)
            outs['m'].append(m_new)
            x = out_proj([hg], (pw['c_w_out'], j), x, vec('norm_mix_post', l))
        if state is None:
            nxt = ((w['ffn_w_up'], l + 1), (w['ffn_w_down'], l + 1)) if l + 1 < DEPTH else None
            x, f_new, cast = ffn_seq(x, seq_len, vec('norm_ffn_pre', l), pw['ffn_w_up'][l], (pw['ffn_cp'], l),
                                     pw['ffn_w_down'][l], vec('norm_ffn_post', l), next_f32=nxt)
            if cast:
                pw['ffn_w_up'].append(cast[0])
                pw['ffn_w_down'].append(cast[1])
        else:
            x, f_new = ffn_step(x, (state['ffn_conv'], l), vec('norm_ffn_pre', l), pw['ffn_w_up'][l],
                                (w['ffn_conv_w'], l), vec('ffn_conv_b', l), pw['ffn_w_down'][l],
                                vec('norm_ffn_post', l))
        outs['f'].append(f_new)
    return x, {name: jnp.stack(vals) for name, vals in outs.items()}


def kernel(x_prompt, x_sample, cache_k, cache_v, state_lru_h, state_lru_conv, state_mlstm_C, state_mlstm_n, state_mlstm_m, state_ffn_conv, norm_mix_pre, norm_mix_post, norm_ffn_pre, norm_ffn_post, a_w_in, a_sinks, a_conv_w, a_conv_b, a_lru_wa, a_lru_ba, a_lru_wx, a_lru_bx, a_lru_lambda, a_w_out, c_w_in, c_b_i, c_b_f, c_b_o, c_norm, c_w_out, ffn_w_up, ffn_conv_w, ffn_conv_b, ffn_w_down):
    w = {
        'norm_mix_pre': norm_mix_pre, 'norm_mix_post': norm_mix_post,
        'norm_ffn_pre': norm_ffn_pre, 'norm_ffn_post': norm_ffn_post,
        'a_w_in': a_w_in, 'a_sinks': a_sinks, 'a_conv_w': a_conv_w, 'a_conv_b': a_conv_b,
        'a_lru_wa': a_lru_wa, 'a_lru_ba': a_lru_ba, 'a_lru_wx': a_lru_wx, 'a_lru_bx': a_lru_bx,
        'a_lru_lambda': a_lru_lambda, 'a_w_out': a_w_out,
        'c_w_in': c_w_in, 'c_b_i': c_b_i, 'c_b_f': c_b_f, 'c_b_o': c_b_o, 'c_norm': c_norm, 'c_w_out': c_w_out,
        'ffn_w_up': ffn_w_up, 'ffn_conv_w': ffn_conv_w, 'ffn_conv_b': ffn_conv_b, 'ffn_w_down': ffn_w_down,
    }
    pw = _prep_weights(w)
    bp, sp, d = x_prompt.shape
    bs, ss, _ = x_sample.shape
    assert ss == 1, "the sample group advances one token per step"
    y_p, st_p = _trunk(x_prompt.reshape(bp * sp, d), sp, None, w, pw)
    state = {'cache_k': cache_k, 'cache_v': cache_v, 'lru_h': state_lru_h, 'lru_conv': state_lru_conv,
             'mlstm_C': state_mlstm_C, 'mlstm_n': state_mlstm_n, 'mlstm_m': state_mlstm_m,
             'ffn_conv': state_ffn_conv}
    y_s, st_s = _trunk(x_sample.reshape(bs * ss, d), 1, state, w, pw)
    return (y_p.reshape(bp, sp, d), y_s.reshape(bs, ss, d),
            st_p['k'], st_s['k'], st_p['v'], st_s['v'], st_p['h'], st_s['h'],
            st_p['conv'], st_s['conv'], st_p['C'], st_s['C'], st_p['n'], st_s['n'],
            st_p['m'], st_s['m'], st_p['f'], st_s['f'])
```

```python
import functools
import math

import jax
import jax.numpy as jnp
from jax import lax
from jax.experimental import pallas as pl
from jax.experimental.pallas import tpu as pltpu

F32 = jnp.float32
BF16 = jnp.bfloat16

D_MODEL = 2048
DEPTH = 4
PAST_LEN = 16384
N_HEADS_A = 16
N_KV_A = 4
HD_A = 64
GROUP_A = N_HEADS_A // N_KV_A
WINDOW = 128
BLOCK_A = 128
ROPE_THETA = 10000.0
D_RNN = 1024
N_BLOCKS_B = 16
BS_B = D_RNN // N_BLOCKS_B
CONV_B = 4
LRU_C = 8.0
N_HEADS_C = 8
DK_C = 128
DV_C = 256
D_FF = 6144
CONV_F = 3
EPS = 1e-6
QA = N_HEADS_A * HD_A
KA = N_KV_A * HD_A
IN_A = QA + 2 * KA + 2 * D_RNN
QC = N_HEADS_C * DK_C
VC = N_HEADS_C * DV_C
IN_C_MAIN = 2 * QC + 2 * VC

LANES = 128
SUBLANES = 8
VMEM_LIMIT_BYTES = 60 * 1024 * 1024

TM = 512
TM_PROJ = 1024
TF = 1024
FFN_SUB = 1024
LRU_TT = 256
MLSTM_L = 256
LRU_GROUP = 256
GATE_PAD = LANES
SAMPLES_PER_STEP = 4


def _cparams(n_axes):
    return pltpu.CompilerParams(dimension_semantics=("arbitrary",) * n_axes,
                                vmem_limit_bytes=VMEM_LIMIT_BYTES)


def _rms(x, g):
    ms = jnp.mean(x * x, axis=-1, keepdims=True)
    return x * lax.rsqrt(ms + EPS) * g


def _gelu_tanh(x):
    c = math.sqrt(2.0 / math.pi)
    hx = 0.5 * x
    return hx + hx * jnp.tanh(x * (c + (c * 0.044715) * (x * x)))


def _sigmoid(x):
    return 0.5 + 0.5 * jnp.tanh(0.5 * x)


def _log_sigmoid(x):
    return jnp.minimum(x, 0.0) - jnp.log1p(jnp.exp(-jnp.abs(x)))


def _norm_matmul_kernel(x_ref, g_ref, w_ref, *rest, with_side):
    o_ref, hn_ref = rest[1 if with_side else 0], rest[-1]

    @pl.when(pl.program_id(1) == 0)
    def _():
        hn_ref[...] = _rms(x_ref[...], g_ref[...]).astype(BF16)
        if with_side:
            rest[2][...] = jnp.dot(hn_ref[...], rest[0][...], preferred_element_type=F32)

    o_ref[...] = jnp.dot(hn_ref[...], w_ref[...], preferred_element_type=F32)


def _proj_tn(m, n):
    del m
    budget = 8 * 1024 * 1024
    best = LANES
    for tn in range(LANES, n + 1, LANES):
        if n % tn == 0 and D_MODEL * tn * 2 <= budget:
            best = tn
    return best


def _layer(opnd, block, imap):
    arr, idx = opnd
    return pl.BlockSpec((None,) + tuple(block), lambda *g: (idx,) + tuple(imap(*g))), arr


def _unzip(pairs):
    return [p[0] for p in pairs], [p[1] for p in pairs]


def _cast_slabs(srcs, dsts, valid):
    for src, dst, nv in zip(srcs, dsts, valid):
        if nv == 'interleave':
            for c in range(D_FF // TF):
                dst[:, 2 * c * TF:(2 * c + 1) * TF] = src[:, c * TF:(c + 1) * TF].astype(BF16)
                dst[:, (2 * c + 1) * TF:(2 * c + 2) * TF] = src[:, D_FF + c * TF:D_FF + (c + 1) * TF].astype(BF16)
            continue
        x = src[...]
        if nv is not None:
            x = jnp.where(lax.broadcasted_iota(jnp.int32, x.shape, 1) < nv, x, 0.0)
        dst[...] = x.astype(BF16)


def _add_cast_job(jobs, nsteps, step_of, in_specs, args, out_specs, out_shape):
    for arr, layer, ncols, cblk in jobs:
        rows = arr.shape[-2]
        rw, rem = divmod(rows, nsteps)
        assert rem == 0 and rw % (2 * SUBLANES) == 0 and ncols % LANES == 0, (rows, nsteps, ncols)
        if layer is None:
            in_specs.append(pl.BlockSpec((rw, ncols), lambda *g, _c=cblk: (step_of(*g), _c)))
        else:
            in_specs.append(pl.BlockSpec((None, rw, ncols), lambda *g, _l=layer, _c=cblk: (_l, step_of(*g), _c)))
        args.append(arr)
        out_specs.append(pl.BlockSpec((rw, ncols), lambda *g: (step_of(*g), 0)))
        out_shape.append(jax.ShapeDtypeStruct((rows, ncols), BF16))


def norm_matmul(x, g, w, n, tn, w_side=None):
    m, k = x.shape
    assert n % tn == 0 and n <= w[0].shape[-1]
    tm = min(TM_PROJ, m)
    specs, args = _unzip([(pl.BlockSpec((tm, k), lambda i, j: (i, 0)), x),
                          _layer(g, (1, k), lambda i, j: (0, 0)),
                          _layer(w, (k, tn), lambda i, j: (0, j))])
    in_specs = list(specs)
    out_specs = [pl.BlockSpec((tm, tn), lambda i, j: (i, j))]
    out_shape = [jax.ShapeDtypeStruct((m, n), F32)]
    if w_side is not None:
        ns = w_side[0].shape[-1]
        spec, arr = _layer(w_side, (k, ns), lambda i, j: (0, 0))
        in_specs.append(spec)
        out_specs.append(pl.BlockSpec((tm, ns), lambda i, j: (i, 0)))
        out_shape.append(jax.ShapeDtypeStruct((m, ns), F32))
        args.append(arr)
    outs = pl.pallas_call(
        functools.partial(_norm_matmul_kernel, with_side=w_side is not None),
        grid=(m // tm, n // tn),
        in_specs=in_specs,
        out_specs=out_specs,
        out_shape=out_shape,
        scratch_shapes=[pltpu.VMEM((tm, k), BF16)],
        compiler_params=_cparams(2),
        name="norm_matmul",
    )(*args)
    return outs[0] if w_side is None else outs


def _out_proj_kernel(*refs, n_in):
    a_refs = refs[:n_in]
    w_ref, x_ref, g_ref, o_ref = refs[n_in:]
    acc = None
    row = 0
    for a_ref in a_refs:
        ka = a_ref.shape[1]
        part = jnp.dot(a_ref[...], w_ref[row:row + ka, :], preferred_element_type=F32)
        acc = part if acc is None else acc + part
        row += ka
    o_ref[...] = x_ref[...] + _rms(acc, g_ref[...])


def out_proj(inputs, w, x, g):
    m, d = x.shape
    tm = min(TM, m)
    w_spec, w_arr = _layer(w, w[0].shape[1:], lambda i: (0, 0))
    g_spec, g_arr = _layer(g, (1, d), lambda i: (0, 0))
    in_specs = [pl.BlockSpec((tm, a.shape[1]), lambda i: (i, 0)) for a in inputs]
    in_specs += [w_spec, pl.BlockSpec((tm, d), lambda i: (i, 0)), g_spec]
    return pl.pallas_call(
        functools.partial(_out_proj_kernel, n_in=len(inputs)),
        grid=(m // tm,),
        in_specs=in_specs,
        out_specs=pl.BlockSpec((tm, d), lambda i: (i, 0)),
        out_shape=jax.ShapeDtypeStruct((m, d), F32),
        compiler_params=_cparams(1),
        name="out_proj",
    )(*inputs, w_arr, x, g_arr)


def _ffn_conv_params(convw, convb):
    nl = convw.shape[0]
    p = jnp.concatenate([convw, convb[:, None, :], jnp.zeros((nl, SUBLANES - CONV_F - 1, 2 * D_FF), F32)], axis=1)
    return jnp.swapaxes(p.reshape(nl, SUBLANES, 2 * D_FF // TF, TF), 1, 2)


def _ffn_conv(u, prev, cp):
    cw = cp[0:CONV_F, :]
    cb = cp[CONV_F:CONV_F + 1, :]
    y = pltpu.roll(u, 2, axis=0) * cw[0:1, :] + pltpu.roll(u, 1, axis=0) * cw[1:2, :] + u * cw[2:3, :] + cb
    head = jnp.concatenate([prev, u[0:SUBLANES, :]], axis=0)
    y_head = (head[SUBLANES - 2:2 * SUBLANES - 2, :] * cw[0:1, :] + head[SUBLANES - 1:2 * SUBLANES - 1, :] * cw[1:2, :]
              + head[SUBLANES:, :] * cw[2:3, :] + cb)
    return jnp.concatenate([y_head, y[SUBLANES:, :]], axis=0)


def _ffn_seq_kernel(x_ref, gpre_ref, wgv_ref, cp_ref, wd_ref, gpost_ref,
                    *rest, tiles_per_seq, cast_next, n_chunks):
    if cast_next:
        wupn_ref, wdn_ref, o_ref, nu_ref, wupn_out, wdn_out, hn_ref, carry_ref = rest
        wupn_out[...] = wupn_ref[...].astype(BF16)
        wdn_out[...] = wdn_ref[...].astype(BF16)
    else:
        o_ref, nu_ref, hn_ref, carry_ref = rest
    i = pl.program_id(0)
    f = pl.program_id(1)
    nf = n_chunks
    tm = x_ref.shape[0]

    @pl.when(f == 0)
    def _():
        hn_ref[...] = _rms(x_ref[...], gpre_ref[...]).astype(BF16)
        o_ref[...] = jnp.zeros_like(o_ref)

    @pl.when(i % tiles_per_seq == 0)
    def _():
        carry_ref[f] = jnp.zeros(carry_ref.shape[1:], F32)

    u = jnp.dot(hn_ref[...], wgv_ref[...], preferred_element_type=F32)
    tf = u.shape[1] // 2
    ug, uv = u[:, :tf], u[:, tf:]
    cg = _ffn_conv(ug, carry_ref[f, 0], cp_ref[f])
    cv = _ffn_conv(uv, carry_ref[f, 1], cp_ref[f + nf])
    carry_ref[f, 0] = ug[tm - SUBLANES:tm, :]
    carry_ref[f, 1] = uv[tm - SUBLANES:tm, :]
    nu_ref[0] = ug[tm - (CONV_F - 1):tm, :]
    nu_ref[1] = uv[tm - (CONV_F - 1):tm, :]
    act = (_gelu_tanh(cg) * cv).astype(BF16)
    o_ref[...] += jnp.dot(act, wd_ref[...], preferred_element_type=F32)

    @pl.when(f == nf - 1)
    def _():
        o_ref[...] = x_ref[...] + _rms(o_ref[...], gpost_ref[...])


def ffn_seq(x, seq_len, gpre, wup, cp, wdown, gpost, next_f32=None):
    m, d = x.shape
    tm = min(TM, seq_len)
    tps = seq_len // tm
    nb = m // seq_len
    nf = D_FF // TF
    nsteps = (m // tm) * nf
    cast_next = next_f32 is not None
    kern = functools.partial(_ffn_seq_kernel, tiles_per_seq=tps, cast_next=cast_next, n_chunks=nf)
    up = dn = lambda f: f
    in_specs, args = _unzip([
        (pl.BlockSpec((tm, d), lambda i, f: (i, 0)), x),
        _layer(gpre, (1, d), lambda i, f: (0, 0)),
        (pl.BlockSpec((d, 2 * TF), lambda i, f: (0, up(f))), wup),
        _layer(cp, cp[0].shape[1:], lambda i, f: (0, 0, 0)),
        (pl.BlockSpec((TF, d), lambda i, f: (dn(f), 0)), wdown),
        _layer(gpost, (1, d), lambda i, f: (0, 0))])
    out_specs = [pl.BlockSpec((tm, d), lambda i, f: (i, 0)),
                 pl.BlockSpec((None, 2, CONV_F - 1, TF), lambda i, f: (i, 0, 0, up(f)))]
    out_shape = [jax.ShapeDtypeStruct((m, d), F32),
                 jax.ShapeDtypeStruct((m // tm, 2, CONV_F - 1, D_FF), F32)]
    if cast_next:
        cw, rem_c = divmod(2 * D_FF, nsteps)
        rw, rem_r = divmod(D_FF, nsteps)
        assert rem_c == 0 and rem_r == 0 and cw % LANES == 0 and rw % (2 * SUBLANES) == 0, (nsteps, cw, rw)
        up_in, up_arr = _layer(next_f32[0], (d, cw), lambda i, f: (0, i * nf + up(f)))
        dn_in, dn_arr = _layer(next_f32[1], (rw, d), lambda i, f: (i * nf + up(f), 0))
        in_specs += [up_in, dn_in]
        args += [up_arr, dn_arr]
        per, half = TF // cw, D_FF // cw
        assert TF % cw == 0

        def up_dst(i, f):
            s = i * nf + up(f)
            return 0, ((s % half) // per) * (2 * per) + (s // half) * per + s % per

        out_specs += [pl.BlockSpec((d, cw), up_dst),
                      pl.BlockSpec((rw, d), lambda i, f: (i * nf + up(f), 0))]
        out_shape += [jax.ShapeDtypeStruct((d, 2 * D_FF), BF16), jax.ShapeDtypeStruct((D_FF, d), BF16)]
    outs = pl.pallas_call(
        kern,
        grid=(m // tm, nf),
        in_specs=in_specs,
        out_specs=out_specs,
        out_shape=out_shape,
        scratch_shapes=[pltpu.VMEM((tm, d), BF16),
                        pltpu.VMEM((nf, 2, SUBLANES, TF), F32)],
        compiler_params=_cparams(2),
        name="ffn_seq",
    )(*args)
    y, nu = outs[:2]
    last = nu.reshape(nb, tps, 2, CONV_F - 1, D_FF)[:, tps - 1]
    last = jnp.swapaxes(last, 1, 2).reshape(nb, CONV_F - 1, 2 * D_FF)
    return y, last, tuple(outs[2:])


def _ffn_step_kernel(x_ref, gpre_ref, wg_ref, wv_ref, cwg_ref, cwv_ref, cbg_ref, cbv_ref, wd_ref, gpost_ref,
                     pg_ref, pv_ref,
                     o_ref, ng_ref, nv_ref, hn_ref, acc_ref):
    f = pl.program_id(0)
    nf = pl.num_programs(0)

    @pl.when(f == 0)
    def _():
        hn_ref[...] = _rms(x_ref[...], gpre_ref[...]).astype(BF16)
        acc_ref[...] = jnp.zeros_like(acc_ref)

    hn = hn_ref[...]
    ug = jnp.dot(hn, wg_ref[...], preferred_element_type=F32)
    uv = jnp.dot(hn, wv_ref[...], preferred_element_type=F32)
    p0g, p1g = pg_ref[:, 0, :], pg_ref[:, 1, :]
    p0v, p1v = pv_ref[:, 0, :], pv_ref[:, 1, :]
    ng_ref[:, 0, :] = p1g
    ng_ref[:, 1, :] = ug
    nv_ref[:, 0, :] = p1v
    nv_ref[:, 1, :] = uv
    cwg = cwg_ref[...]
    cwv = cwv_ref[...]
    cg = p0g * cwg[0:1, :] + p1g * cwg[1:2, :] + ug * cwg[2:3, :] + cbg_ref[...]
    cv = p0v * cwv[0:1, :] + p1v * cwv[1:2, :] + uv * cwv[2:3, :] + cbv_ref[...]
    act = (_gelu_tanh(cg) * cv).astype(BF16)
    acc_ref[...] += jnp.dot(act, wd_ref[...], preferred_element_type=F32)

    @pl.when(f == nf - 1)
    def _():
        o_ref[...] = x_ref[...] + _rms(acc_ref[...], gpost_ref[...])


def ffn_step(x, past, gpre, wup, convw, convb, wdown, gpost):
    n, d = x.shape
    nf = D_FF // TF
    phalf = lambda c: _layer(past, (n, CONV_F - 1, TF), lambda f: (0, 0, f + c * nf))
    in_specs, args = _unzip([
        (pl.BlockSpec((n, d), lambda f: (0, 0)), x),
        _layer(gpre, (1, d), lambda f: (0, 0)),
        (pl.BlockSpec((d, TF), lambda f: (0, 2 * f)), wup),
        (pl.BlockSpec((d, TF), lambda f: (0, 2 * f + 1)), wup),
        _layer(convw, (CONV_F, TF), lambda f: (0, f)),
        _layer(convw, (CONV_F, TF), lambda f: (0, f + nf)),
        _layer(convb, (1, TF), lambda f: (0, f)),
        _layer(convb, (1, TF), lambda f: (0, f + nf)),
        (pl.BlockSpec((TF, d), lambda f: (f, 0)), wdown),
        _layer(gpost, (1, d), lambda f: (0, 0)),
        phalf(0), phalf(1)])
    y, ng, nv = pl.pallas_call(
        _ffn_step_kernel,
        grid=(nf,),
        in_specs=in_specs,
        out_specs=[pl.BlockSpec((n, d), lambda f: (0, 0)),
                   pl.BlockSpec((n, CONV_F - 1, TF), lambda f: (0, 0, f)),
                   pl.BlockSpec((n, CONV_F - 1, TF), lambda f: (0, 0, f))],
        out_shape=[jax.ShapeDtypeStruct((n, d), F32),
                   jax.ShapeDtypeStruct((n, CONV_F - 1, D_FF), F32),
                   jax.ShapeDtypeStruct((n, CONV_F - 1, D_FF), F32)],
        scratch_shapes=[pltpu.VMEM((n, d), BF16), pltpu.VMEM((n, d), F32)],
        compiler_params=_cparams(1),
        name="ffn_step",
    )(*args)
    return y, jnp.concatenate([ng, nv], axis=-1)


def _rope_tables(pos0, t):
    half = HD_A // 2
    pos = (pos0 + jnp.arange(t)).astype(F32)
    inv = ROPE_THETA ** (-jnp.arange(half, dtype=F32) / half)
    ang = pos[:, None] * inv[None, :]
    cos, sin = jnp.cos(ang), jnp.sin(ang)
    zero = jnp.zeros_like(sin)
    cos_h = jnp.concatenate([cos, cos], axis=-1)
    sin_up = jnp.concatenate([-sin, zero], axis=-1)
    sin_dn = jnp.concatenate([zero, sin], axis=-1)
    two = lambda a: jnp.concatenate([a, a], axis=-1)
    return two(cos_h), two(sin_up), two(sin_dn)


def _rope(x, cos, sin_up, sin_dn):
    w = x.shape[1]
    reps = w // LANES
    tile = lambda a: jnp.concatenate([a] * reps, axis=1) if reps > 1 else a
    x_up = pltpu.roll(x, w - HD_A // 2, axis=1)
    x_dn = pltpu.roll(x, HD_A // 2, axis=1)
    return x * tile(cos) + x_up * tile(sin_up) + x_dn * tile(sin_dn)


def _dup_half(a, half):
    lane = lax.broadcasted_iota(jnp.int32, a.shape, 1)
    sw = pltpu.roll(a, HD_A, axis=1)
    lo = lane < HD_A
    return jnp.where(lo, a, sw) if half == 0 else jnp.where(lo, sw, a)


def _sink_softmax(s, sk_rows):
    m = jnp.maximum(jnp.max(s, axis=-1, keepdims=True), sk_rows)
    p = jnp.exp(s - m)
    return p, jnp.sum(p, axis=-1, keepdims=True) + jnp.exp(sk_rows - m)


def _swa_prompt_kernel(sinks_ref, q_ref, kc_ref, kp_ref, vc_ref, vp_ref,
                       cc_ref, suc_ref, sdc_ref, cp_ref, sup_ref, sdp_ref, *rest, cast_valid):
    nc = len(cast_valid)
    att_ref, knew_ref, vnew_ref = rest[nc:nc + 3]
    _cast_slabs(rest[:nc], rest[nc + 3:], cast_valid)
    c = pl.program_id(1)
    nb = pl.num_programs(1)
    bq = q_ref.shape[0]
    q = _rope(q_ref[...], cc_ref[...], suc_ref[...], sdc_ref[...])
    kc = _rope(kc_ref[...], cc_ref[...], suc_ref[...], sdc_ref[...])
    kp = _rope(kp_ref[...], cp_ref[...], sup_ref[...], sdp_ref[...])
    vc = vc_ref[...]
    vp = vp_ref[...]

    @pl.when(c == nb - 1)
    def _():
        knew_ref[...] = kc
        vnew_ref[...] = vc

    rows = GROUP_A * bq
    r = lax.broadcasted_iota(jnp.int32, (rows, 2 * bq), 0) % bq
    col = lax.broadcasted_iota(jnp.int32, (rows, 2 * bq), 1)
    valid = (col >= r) & (col <= r + WINDOW) & ((c > 0) | (col >= bq))
    lane = lax.broadcasted_iota(jnp.int32, (bq, LANES), 1)
    lo = lane < HD_A
    row_head = lax.broadcasted_iota(jnp.int32, (rows, 1), 0) // bq

    for h in range(N_KV_A):
        t, half = divmod(h, 2)
        ksl = slice(t * LANES, (t + 1) * LANES)
        kd = _dup_half(jnp.concatenate([kp[:, ksl], kc[:, ksl]], axis=0), half).astype(BF16)
        vd = _dup_half(jnp.concatenate([vp[:, ksl], vc[:, ksl]], axis=0), half).astype(BF16)
        parts = []
        for p in range(2):
            q2 = q[:, (2 * h + p) * LANES:(2 * h + p + 1) * LANES]
            parts.append(jnp.where(lo, q2, 0.0))
            parts.append(jnp.where(lo, 0.0, q2))
        qs = (jnp.concatenate(parts, axis=0) * (HD_A ** -0.5)).astype(BF16)
        s = lax.dot_general(qs, kd, (((1,), (1,)), ((), ())), preferred_element_type=F32)
        s = jnp.where(valid, s, -jnp.inf)
        sk = jnp.zeros((rows, 1), F32)
        for g in range(GROUP_A):
            sk = jnp.where(row_head == g, sinks_ref[h * GROUP_A + g], sk)
        pr, den = _sink_softmax(s, sk)
        o = jnp.dot(pr.astype(BF16), vd, preferred_element_type=F32) / den
        for p in range(2):
            o_lo = o[(2 * p) * bq:(2 * p + 1) * bq, :]
            o_hi = o[(2 * p + 1) * bq:(2 * p + 2) * bq, :]
            att_ref[:, (2 * h + p) * LANES:(2 * h + p + 1) * LANES] = jnp.where(lo, o_lo, o_hi).astype(BF16)


def swa_prompt(proj, sinks, nbatch, seq_len, cast_f32=None, cast_valid=()):
    nb = seq_len // BLOCK_A
    p3 = proj.reshape(nbatch, seq_len, IN_A)
    kcol = (QA + 2 * D_RNN) // KA
    cos, sup, sdn = _rope_tables(0, seq_len)
    prev = lambda c: jnp.maximum(c - 1, 0)
    tab_c = pl.BlockSpec((BLOCK_A, LANES), lambda b, c: (c, 0))
    tab_p = pl.BlockSpec((BLOCK_A, LANES), lambda b, c: (prev(c), 0))
    in_specs = [pl.BlockSpec(memory_space=pltpu.SMEM),
                pl.BlockSpec((None, BLOCK_A, QA), lambda b, c: (b, c, 0)),
                pl.BlockSpec((None, BLOCK_A, KA), lambda b, c: (b, c, kcol)),
                pl.BlockSpec((None, BLOCK_A, KA), lambda b, c: (b, prev(c), kcol)),
                pl.BlockSpec((None, BLOCK_A, KA), lambda b, c: (b, c, kcol + 1)),
                pl.BlockSpec((None, BLOCK_A, KA), lambda b, c: (b, prev(c), kcol + 1)),
                tab_c, tab_c, tab_c, tab_p, tab_p, tab_p]
    args = [sinks, p3, p3, p3, p3, p3, cos, sup, sdn, cos, sup, sdn]
    out_specs = [pl.BlockSpec((None, BLOCK_A, QA), lambda b, c: (b, c, 0)),
                 pl.BlockSpec((None, BLOCK_A, KA), lambda b, c: (b, 0, 0)),
                 pl.BlockSpec((None, BLOCK_A, KA), lambda b, c: (b, 0, 0))]
    out_shape = [jax.ShapeDtypeStruct((nbatch, seq_len, QA), BF16),
                 jax.ShapeDtypeStruct((nbatch, BLOCK_A, KA), F32),
                 jax.ShapeDtypeStruct((nbatch, BLOCK_A, KA), F32)]
    if cast_f32 is not None:
        _add_cast_job(cast_f32, nbatch * nb, lambda b, c: b * nb + c, in_specs, args, out_specs, out_shape)
    outs = pl.pallas_call(
        functools.partial(_swa_prompt_kernel, cast_valid=tuple(cast_valid)),
        grid=(nbatch, nb),
        in_specs=in_specs,
        out_specs=out_specs,
        out_shape=out_shape,
        compiler_params=_cparams(2),
        name="swa_prompt",
    )(*args)
    att, knew, vnew = outs[:3]
    return att.reshape(nbatch * seq_len, QA), knew, vnew, tuple(outs[3:])


def _swa_sample_kernel(sinks_ref, q_ref, k_ref, v_ref, kc_ref, vc_ref, cos_ref, sup_ref, sdn_ref,
                       att_ref, knew_ref, vnew_ref):
    w = kc_ref.shape[1]
    lane = lax.broadcasted_iota(jnp.int32, (1, LANES), 1)
    lo = lane < HD_A
    row = lax.broadcasted_iota(jnp.int32, (SUBLANES, 1), 0)
    scale = HD_A ** -0.5
    todo = []
    for z in range(q_ref.shape[0]):
        q = _rope(q_ref[z], cos_ref[...], sup_ref[...], sdn_ref[...])
        k = _rope(k_ref[z], cos_ref[...], sup_ref[...], sdn_ref[...])
        v = v_ref[z]
        knew_ref[z, 0:w - 1, :] = kc_ref[z, 1:w, :]
        knew_ref[z, w - 1:w, :] = k
        vnew_ref[z, 0:w - 1, :] = vc_ref[z, 1:w, :]
        vnew_ref[z, w - 1:w, :] = v
        for h in range(N_KV_A):
            t, half = divmod(h, 2)
            ksl = slice(t * LANES, (t + 1) * LANES)
            kd = _dup_half(kc_ref[z, :, ksl], half)
            kd_new = _dup_half(k[:, ksl], half)
            parts = []
            for p in range(2):
                q2 = q[:, (2 * h + p) * LANES:(2 * h + p + 1) * LANES]
                parts.append(jnp.where(lo, q2, 0.0))
                parts.append(jnp.where(lo, 0.0, q2))
            parts.append(jnp.zeros((SUBLANES - GROUP_A, LANES), F32))
            qs = jnp.concatenate(parts, axis=0)
            s = lax.dot_general(qs.astype(BF16), kd.astype(BF16), (((1,), (1,)), ((), ())),
                                preferred_element_type=F32) * scale
            s_new = jnp.sum(qs * kd_new, axis=-1, keepdims=True) * scale
            todo.append((z, h, s, s_new, v))
    probs = []
    for z, h, s, s_new, v in todo:
        sk = jnp.zeros((SUBLANES, 1), F32)
        for g in range(GROUP_A):
            sk = jnp.where(row == g, sinks_ref[h * GROUP_A + g], sk)
        m = jnp.maximum(jnp.maximum(jnp.max(s, axis=-1, keepdims=True), s_new), sk)
        pr = jnp.exp(s - m)
        p_new = jnp.exp(s_new - m)
        den = jnp.sum(pr, axis=-1, keepdims=True) + p_new + jnp.exp(sk - m)
        probs.append((z, h, (pr / den).astype(BF16), p_new / den, v))
    for z, h, pr, p_new, v in probs:
        t, half = divmod(h, 2)
        ksl = slice(t * LANES, (t + 1) * LANES)
        vd = _dup_half(vc_ref[z, :, ksl], half)
        vd_new = _dup_half(v[:, ksl], half)
        o = jnp.dot(pr, vd.astype(BF16), preferred_element_type=F32) + p_new * vd_new
        for p in range(2):
            att_ref[z, :, (2 * h + p) * LANES:(2 * h + p + 1) * LANES] = jnp.where(
                lo, o[2 * p:2 * p + 1, :], o[2 * p + 1:2 * p + 2, :]).astype(BF16)


def swa_sample(proj, cache_k, cache_v, sinks):
    n = proj.shape[0]
    nl, _, w = cache_k[0].shape[:3]
    p3 = proj.reshape(n, 1, IN_A)
    kcol = (QA + 2 * D_RNN) // KA
    ck_spec, ck = _layer((cache_k[0].reshape(nl, n, w, KA), cache_k[1]), (SAMPLES_PER_STEP, w, KA),
                         lambda z: (z, 0, 0))
    cv_spec, cv = _layer((cache_v[0].reshape(nl, n, w, KA), cache_v[1]), (SAMPLES_PER_STEP, w, KA),
                         lambda z: (z, 0, 0))
    cos, sup, sdn = _rope_tables(PAST_LEN, 1)
    tab = pl.BlockSpec((1, LANES), lambda z: (0, 0))
    gs = SAMPLES_PER_STEP
    assert n % gs == 0
    att, knew, vnew = pl.pallas_call(
        _swa_sample_kernel,
        grid=(n // gs,),
        in_specs=[pl.BlockSpec(memory_space=pltpu.SMEM),
                  pl.BlockSpec((gs, 1, QA), lambda z: (z, 0, 0)),
                  pl.BlockSpec((gs, 1, KA), lambda z: (z, 0, kcol)),
                  pl.BlockSpec((gs, 1, KA), lambda z: (z, 0, kcol + 1)),
                  ck_spec, cv_spec,
                  tab, tab, tab],
        out_specs=[pl.BlockSpec((gs, 1, QA), lambda z: (z, 0, 0)),
                   pl.BlockSpec((gs, w, KA), lambda z: (z, 0, 0)),
                   pl.BlockSpec((gs, w, KA), lambda z: (z, 0, 0))],
        out_shape=[jax.ShapeDtypeStruct((n, 1, QA), BF16),
                   jax.ShapeDtypeStruct((n, w, KA), F32),
                   jax.ShapeDtypeStruct((n, w, KA), F32)],
        compiler_params=_cparams(1),
        name="swa_sample",
    )(sinks, p3, p3, p3, ck, cv, cos, sup, sdn)
    return att.reshape(n, QA), knew, vnew


def _lru_gates(xc, wa_ref, wx_ref, ba_ref, bx_ref, lam_ref):
    rs, is_ = [], []
    for c in range(D_RNN // LRU_GROUP):
        xg = xc[:, c * LRU_GROUP:(c + 1) * LRU_GROUP].astype(BF16)
        rs.append(jnp.dot(xg, wa_ref[c], preferred_element_type=F32))
        is_.append(jnp.dot(xg, wx_ref[c], preferred_element_type=F32))
    r = _sigmoid(jnp.concatenate(rs, axis=1) + ba_ref[...])
    i = _sigmoid(jnp.concatenate(is_, axis=1) + bx_ref[...])
    log_a = LRU_C * r * _log_sigmoid(lam_ref[...])
    a = jnp.exp(log_a)
    b = jnp.sqrt(-jnp.tanh(log_a) * (a * a + 1.0)) * (i * xc)
    return a, b


def _lru_seq_kernel(xr_ref, gr_ref, cw_ref, cb_ref, wa_ref, wx_ref, ba_ref, bx_ref, lam_ref, *rest, side_job):
    if side_job:
        wt_ref, gt_ref, rec_ref, hlast_ref, cnew_ref, w_out, g_out, xbuf_ref, a_ref, b_ref, h_ref, hc_ref = rest
        w_out[...] = wt_ref[...].T.astype(BF16)
        gt = gt_ref[...]
        pad = jnp.zeros((GATE_PAD - gt.shape[0], gt.shape[1]), F32)
        g_out[...] = jnp.concatenate([gt, pad], axis=0).T.astype(BF16)
    else:
        rec_ref, hlast_ref, cnew_ref, xbuf_ref, a_ref, b_ref, h_ref, hc_ref = rest
    t = pl.program_id(1)
    tt = xr_ref.shape[0]

    @pl.when(t == 0)
    def _():
        xbuf_ref[0:SUBLANES, :] = jnp.zeros((SUBLANES, D_RNN), F32)
        hc_ref[...] = jnp.zeros_like(hc_ref)

    xr = xr_ref[...]
    xbuf_ref[SUBLANES:SUBLANES + tt, :] = xr
    cw = cw_ref[...]
    xc = xr * cw[CONV_B - 1:CONV_B, :] + cb_ref[...]
    for j in range(CONV_B - 1):
        off = SUBLANES - (CONV_B - 1) + j
        xc = xc + xbuf_ref[off:off + tt, :] * cw[j:j + 1, :]
    xbuf_ref[0:SUBLANES, :] = xr[tt - SUBLANES:tt, :]
    cnew_ref[...] = xr[tt - (CONV_B - 1):tt, :]

    a, b = _lru_gates(xc, wa_ref, wx_ref, ba_ref, bx_ref, lam_ref)
    a_ref[...] = a
    b_ref[...] = b
    row = lax.broadcasted_iota(jnp.int32, (SUBLANES, D_RNN), 0)

    def body(g, h):
        base = pl.multiple_of(g * SUBLANES, SUBLANES)
        a8 = a_ref[pl.ds(base, SUBLANES), :]
        b8 = b_ref[pl.ds(base, SUBLANES), :]
        for s in (1, 2, 4):
            a_sh = pltpu.roll(a8, s, axis=0)
            b_sh = pltpu.roll(b8, s, axis=0)
            keep = row >= s
            b8 = jnp.where(keep, a8 * b_sh + b8, b8)
            a8 = jnp.where(keep, a8 * a_sh, a8)
        h8 = b8 + a8 * h
        h_ref[pl.ds(base, SUBLANES), :] = h8
        return h8[SUBLANES - 1:SUBLANES, :]

    h_last = lax.fori_loop(0, tt // SUBLANES, body, hc_ref[...])
    hc_ref[...] = h_last
    hlast_ref[...] = h_last
    rec_ref[...] = (h_ref[...] * _gelu_tanh(gr_ref[...])).astype(BF16)


def _lru_blockdiag(w):
    per = LRU_GROUP // BS_B
    w4 = w.reshape(D_RNN // LRU_GROUP, per, BS_B, BS_B)
    eye = jnp.eye(per, dtype=w.dtype)
    return jnp.einsum('cpij,pq->cpiqj', w4, eye).reshape(D_RNN // LRU_GROUP, LRU_GROUP, LRU_GROUP).astype(BF16)


def lru_seq(proj, nbatch, seq_len, cw, cb, wa_bd, wx_bd, ba, bx, lam, wt_job=None):
    p3 = proj.reshape(nbatch, seq_len, IN_A)
    tt = min(LRU_TT, seq_len)
    nt = seq_len // tt
    vec = lambda o: _layer(o, (1, D_RNN), lambda b, t: (0, 0))
    wspec = lambda o: _layer(o, o[0].shape[1:], lambda b, t: (0, 0, 0))
    in_specs, args = _unzip([
        (pl.BlockSpec((None, tt, D_RNN), lambda b, t: (b, t, QA // D_RNN)), p3),
        (pl.BlockSpec((None, tt, D_RNN), lambda b, t: (b, t, QA // D_RNN + 1)), p3),
        _layer(cw, (CONV_B, D_RNN), lambda b, t: (0, 0)),
        vec(cb), wspec(wa_bd), wspec(wx_bd), vec(ba), vec(bx), vec(lam)])
    out_specs = [pl.BlockSpec((None, tt, D_RNN), lambda b, t: (b, t, 0)),
                 pl.BlockSpec((None, 1, D_RNN), lambda b, t: (b, 0, 0)),
                 pl.BlockSpec((None, CONV_B - 1, D_RNN), lambda b, t: (b, 0, 0))]
    out_shape = [jax.ShapeDtypeStruct((nbatch, seq_len, D_RNN), BF16),
                 jax.ShapeDtypeStruct((nbatch, 1, D_RNN), F32),
                 jax.ShapeDtypeStruct((nbatch, CONV_B - 1, D_RNN), F32)]
    if wt_job is not None:
        wt, n_main = wt_job
        nl, n_all, kin = wt.shape
        spl, rem = divmod(nbatch * nt, nl)
        rp, rem2 = divmod(n_main, spl)
        ng = n_all - n_main
        assert rem == 0 and rem2 == 0 and rp % LANES == 0 and n_main % ng == 0 and ng % SUBLANES == 0
        step = lambda b, t: b * nt + t
        in_specs += [pl.BlockSpec((None, rp, kin), lambda b, t: (step(b, t) // spl, step(b, t) % spl, 0)),
                     pl.BlockSpec((None, ng, kin), lambda b, t: (step(b, t) // spl, n_main // ng, 0))]
        args += [wt, wt]
        out_specs += [pl.BlockSpec((None, kin, rp), lambda b, t: (step(b, t) // spl, 0, step(b, t) % spl)),
                      pl.BlockSpec((None, kin, GATE_PAD), lambda b, t: (step(b, t) // spl, 0, 0))]
        out_shape += [jax.ShapeDtypeStruct((nl, kin, n_main), BF16), jax.ShapeDtypeStruct((nl, kin, GATE_PAD), BF16)]
    outs = pl.pallas_call(
        functools.partial(_lru_seq_kernel, side_job=wt_job is not None),
        grid=(nbatch, nt),
        in_specs=in_specs,
        out_specs=out_specs,
        out_shape=out_shape,
        scratch_shapes=[pltpu.VMEM((tt + SUBLANES, D_RNN), F32),
                        pltpu.VMEM((tt, D_RNN), F32),
                        pltpu.VMEM((tt, D_RNN), F32),
                        pltpu.VMEM((tt, D_RNN), F32),
                        pltpu.VMEM((1, D_RNN), F32)],
        compiler_params=_cparams(2),
        name="lru_seq",
    )(*args)
    rec, hlast, cnew = outs[:3]
    return rec.reshape(nbatch * seq_len, D_RNN), hlast.reshape(nbatch, D_RNN), cnew, tuple(outs[3:])


def _lru_step_kernel(xr_ref, gr_ref, past_ref, h0_ref, cw_ref, cb_ref,
                     wa_ref, wx_ref, ba_ref, bx_ref, lam_ref, rec_ref, h_ref, cnew_ref):
    cw = cw_ref[...]
    xr = xr_ref[...]
    xc = xr * cw[CONV_B - 1:CONV_B, :] + cb_ref[...]
    for r in range(CONV_B - 1):
        p = past_ref[:, r, :]
        xc = xc + p * cw[r:r + 1, :]
        if r > 0:
            cnew_ref[:, r - 1, :] = p
    cnew_ref[:, CONV_B - 2, :] = xr
    a, b = _lru_gates(xc, wa_ref, wx_ref, ba_ref, bx_ref, lam_ref)
    h = b + a * h0_ref[...]
    h_ref[...] = h
    rec_ref[...] = (h * _gelu_tanh(gr_ref[...])).astype(BF16)


def lru_step(proj, h0, conv0, cw, cb, wa_bd, wx_bd, ba, bx, lam):
    n = proj.shape[0]
    col = lambda j: pl.BlockSpec((n, D_RNN), lambda i: (0, j))
    vec = lambda o: _layer(o, (1, D_RNN), lambda i: (0, 0))
    wspec = lambda o: _layer(o, o[0].shape[1:], lambda i: (0, 0, 0))
    hist = pl.BlockSpec((n, CONV_B - 1, D_RNN), lambda i: (0, 0, 0))
    in_specs, args = _unzip([
        (col(QA // D_RNN), proj), (col(QA // D_RNN + 1), proj),
        _layer(conv0, (n, CONV_B - 1, D_RNN), lambda i: (0, 0, 0)),
        _layer(h0, (n, D_RNN), lambda i: (0, 0)),
        _layer(cw, (CONV_B, D_RNN), lambda i: (0, 0)),
        vec(cb), wspec(wa_bd), wspec(wx_bd), vec(ba), vec(bx), vec(lam)])
    rec, h, conv_new = pl.pallas_call(
        _lru_step_kernel,
        grid=(1,),
        in_specs=in_specs,
        out_specs=[col(0), col(0), hist],
        out_shape=[jax.ShapeDtypeStruct((n, D_RNN), BF16), jax.ShapeDtypeStruct((n, D_RNN), F32),
                   jax.ShapeDtypeStruct((n, CONV_B - 1, D_RNN), F32)],
        compiler_params=_cparams(1),
        name="lru_step",
    )(*args)
    return rec, h, conv_new


def _cumsum_lanes(x):
    n = x.shape[1]
    lane = lax.broadcasted_iota(jnp.int32, x.shape, 1)
    s = 1
    while s < n:
        x = x + jnp.where(lane >= s, pltpu.roll(x, s, axis=1), 0.0)
        s *= 2
    return x


def _cummax_lanes(x):
    n = x.shape[1]
    lane = lax.broadcasted_iota(jnp.int32, x.shape, 1)
    s = 1
    while s < n:
        x = jnp.maximum(x, jnp.where(lane >= s, pltpu.roll(x, s, axis=1), -jnp.inf))
        s *= 2
    return x


def _mlstm_seq_kernel(q_ref, k_ref, v_ref, o_ref, gate_ref, gbias_ref, bo_ref, gn_ref,
                      out_ref, c_ref, n_ref, m_ref, cx_ref):
    ci = pl.program_id(1)
    nh = N_HEADS_C
    ln = q_ref.shape[0]

    @pl.when(ci == 0)
    def _():
        cx_ref[...] = jnp.zeros_like(cx_ref)
        m_ref[...] = jnp.zeros_like(m_ref)

    gate_t = (gate_ref[...] + gbias_ref[...]).T
    i_rows = gate_t[0:nh, :]
    b_rows = _cumsum_lanes(_log_sigmoid(gate_t[nh:2 * nh, :]))
    m_prev = m_ref[:, 0:1]
    r_rows = i_rows - b_rows
    mt_rows = b_rows + jnp.maximum(m_prev, _cummax_lanes(r_rows))
    m_last = mt_rows[:, ln - 1:ln]
    b_last = b_rows[:, ln - 1:ln]
    decay = jnp.exp(b_last + m_prev - m_last)
    per_token = jnp.concatenate([
        b_rows - mt_rows,
        jnp.exp(b_rows + m_prev - mt_rows),
        jnp.exp(-mt_rows),
        jnp.exp(b_last - b_rows + i_rows - m_last),
        jnp.zeros((LANES - 4 * nh, ln), F32)], axis=0)
    cols = per_token.T
    m_ref[...] = jnp.broadcast_to(m_last, m_ref.shape)

    ri = lax.broadcasted_iota(jnp.int32, (ln, ln), 0)
    cj = lax.broadcasted_iota(jnp.int32, (ln, ln), 1)
    causal = cj <= ri
    ones_l = jnp.ones((ln, LANES), BF16)
    ones_v = jnp.ones((DV_C, LANES), BF16)
    two = lambda a: jnp.concatenate([a] * (DV_C // LANES), axis=1)

    for h in range(nh):
        qterm = cols[:, h:h + 1]
        inter_b = jnp.broadcast_to(cols[:, nh + h:nh + h + 1], (ln, LANES))
        emt_b = jnp.broadcast_to(cols[:, 2 * nh + h:2 * nh + h + 1], (ln, LANES))
        w_col = cols[:, 3 * nh + h:3 * nh + h + 1]
        qb = q_ref[:, h * DK_C:(h + 1) * DK_C].astype(BF16)
        kh = k_ref[:, h * DK_C:(h + 1) * DK_C] * (DK_C ** -0.5)
        vx = jnp.concatenate([v_ref[:, h * DV_C:(h + 1) * DV_C].astype(BF16), ones_l], axis=1)

        s = lax.dot_general(qb, kh.astype(BF16), (((1,), (1,)), ((), ())), preferred_element_type=F32)
        amat = (jnp.exp(jnp.where(causal, qterm + r_rows[h:h + 1, :], -jnp.inf)) * s).astype(BF16)
        cx = cx_ref[h]
        qc = jnp.dot(qb, cx.astype(BF16), preferred_element_type=F32)
        av = jnp.dot(amat, vx, preferred_element_type=F32)
        num = two(inter_b) * qc[:, :DV_C] + av[:, :DV_C]
        den_b = inter_b * qc[:, DV_C:] + av[:, DV_C:]
        hh = num * two(1.0 / jnp.maximum(jnp.abs(den_b), emt_b))

        kw = (kh * w_col).astype(BF16)
        cx_ref[h] = decay[h:h + 1, :] * cx + lax.dot_general(kw, vx, (((0,), (0,)), ((), ())),
                                                             preferred_element_type=F32)

        vsl = slice(h * DV_C, (h + 1) * DV_C)
        ssq_b = jnp.dot((hh * hh).astype(BF16), ones_v, preferred_element_type=F32)
        hn = hh * two(lax.rsqrt(ssq_b * (1.0 / DV_C) + EPS)) * gn_ref[:, vsl]
        og = _sigmoid(o_ref[:, vsl] + bo_ref[:, vsl])
        out_ref[:, vsl] = (og * hn).astype(BF16)

    @pl.when(ci == pl.num_programs(1) - 1)
    def _():
        for h in range(nh):
            cx = cx_ref[h]
            c_ref[h] = cx[:, :DV_C]
            n_ref[h:h + 1, :] = cx[:, DV_C:].T[0:1, :]


def mlstm_seq(proj, gates, nbatch, seq_len, gbias, b_o, g_norm):
    ln = min(MLSTM_L, seq_len)
    p3 = proj.reshape(nbatch, seq_len, IN_C_MAIN)
    g3 = gates.reshape(nbatch, seq_len, GATE_PAD)
    in_specs, args = _unzip([
        (pl.BlockSpec((None, ln, QC), lambda b, c: (b, c, 0)), p3),
        (pl.BlockSpec((None, ln, QC), lambda b, c: (b, c, 1)), p3),
        (pl.BlockSpec((None, ln, VC), lambda b, c: (b, c, 2 * QC // VC)), p3),
        (pl.BlockSpec((None, ln, VC), lambda b, c: (b, c, 2 * QC // VC + 1)), p3),
        (pl.BlockSpec((None, ln, GATE_PAD), lambda b, c: (b, c, 0)), g3),
        _layer(gbias, (1, GATE_PAD), lambda b, c: (0, 0)),
        _layer(b_o, (1, VC), lambda b, c: (0, 0)),
        _layer(g_norm, (1, VC), lambda b, c: (0, 0))])
    out, c, n, m = pl.pallas_call(
        _mlstm_seq_kernel,
        grid=(nbatch, seq_len // ln),
        in_specs=in_specs,
        out_specs=[pl.BlockSpec((None, ln, VC), lambda b, c: (b, c, 0)),
                   pl.BlockSpec((None, N_HEADS_C, DK_C, DV_C), lambda b, c: (b, 0, 0, 0)),
                   pl.BlockSpec((None, N_HEADS_C, DK_C), lambda b, c: (b, 0, 0)),
                   pl.BlockSpec((None, N_HEADS_C, LANES), lambda b, c: (b, 0, 0))],
        out_shape=[jax.ShapeDtypeStruct((nbatch, seq_len, VC), BF16),
                   jax.ShapeDtypeStruct((nbatch, N_HEADS_C, DK_C, DV_C), F32),
                   jax.ShapeDtypeStruct((nbatch, N_HEADS_C, DK_C), F32),
                   jax.ShapeDtypeStruct((nbatch, N_HEADS_C, LANES), F32)],
        scratch_shapes=[pltpu.VMEM((N_HEADS_C, DK_C, DV_C + LANES), F32)],
        compiler_params=_cparams(2),
        name="mlstm_seq",
    )(*args)
    return out.reshape(nbatch * seq_len, VC), c, n, m[:, :, 0]


def _mlstm_step_kernel(q_ref, k_ref, v_ref, o_ref, gate_ref, gbias_ref, bo_ref, gn_ref, c0_ref, n0_ref, m0_ref,
                       out_ref, c_ref, n_ref, m_ref):
    q8 = q_ref[...]
    k8 = k_ref[...] * (DK_C ** -0.5)
    v8 = v_ref[...]
    gate = gate_ref[...] + gbias_ref[...]
    i8 = gate[:, 0:1]
    lf8 = _log_sigmoid(gate[:, 1:2])
    g8 = lf8 + m0_ref[...]
    mt = jnp.maximum(g8, i8)
    inter = jnp.exp(g8 - mt)
    wl = jnp.exp(i8 - mt)
    a8 = wl * jnp.sum(q8 * k8, axis=1, keepdims=True)
    row = lax.broadcasted_iota(jnp.int32, (N_HEADS_C, 1), 0)
    qb = q8.astype(BF16)
    kw = k8 * wl
    qc = jnp.zeros((N_HEADS_C, DV_C), F32)
    for h in range(N_HEADS_C):
        qc = qc + jnp.where(row == h, jnp.dot(qb, c0_ref[h].astype(BF16), preferred_element_type=F32), 0.0)
    vb = v8.astype(BF16)
    for h in range(N_HEADS_C):
        kz = jnp.where(row == h, kw, 0.0).astype(BF16)
        c_ref[h] = inter[h:h + 1, :] * c0_ref[h] + lax.dot_general(kz, vb, (((0,), (0,)), ((), ())),
                                                                   preferred_element_type=F32)
    n0 = n0_ref[...]
    num = inter * qc + a8 * v8
    den = inter * jnp.sum(q8 * n0, axis=1, keepdims=True) + a8
    hh = num / jnp.maximum(jnp.abs(den), jnp.exp(-mt))
    n_ref[...] = inter * n0 + kw
    m_ref[...] = mt
    hn = _rms(hh, gn_ref[...])
    og = _sigmoid(o_ref[...] + bo_ref[...])
    out_ref[...] = (og * hn).astype(BF16)


def mlstm_step(proj, gates, gbias, b_o, g_norm, c0, n0, m0):
    n = proj.shape[0]
    h = N_HEADS_C
    q = proj[:, :QC].reshape(n, h, DK_C)
    k = proj[:, QC:2 * QC].reshape(n, h, DK_C)
    v = proj[:, 2 * QC:2 * QC + VC].reshape(n, h, DV_C)
    o = proj[:, 2 * QC + VC:].reshape(n, h, DV_C)
    gcols = jnp.swapaxes(gates[:, :2 * h].reshape(n, 2, h), 1, 2)
    gb = jnp.swapaxes(gbias[0][gbias[1], 0, :2 * h].reshape(2, h), 0, 1)
    nl = c0[0].shape[0]
    per = lambda *tail: pl.BlockSpec((None,) + tail, lambda z: (z,) + (0,) * len(tail))
    lper = lambda o_, *tail: _layer(o_, (None,) + tail, lambda z: (z,) + (0,) * len(tail))
    lshared = lambda o_, *shape: _layer(o_, shape, lambda z: (0,) * len(shape))
    in_specs, args = _unzip([
        (per(h, DK_C), q), (per(h, DK_C), k), (per(h, DV_C), v), (per(h, DV_C), o), (per(h, 2), gcols),
        (pl.BlockSpec((h, 2), lambda z: (0, 0)), gb),
        lshared((b_o[0].reshape(nl, h, DV_C), b_o[1]), h, DV_C),
        lshared((g_norm[0].reshape(nl, h, DV_C), g_norm[1]), h, DV_C),
        lper(c0, h, DK_C, DV_C), lper(n0, h, DK_C), lper((m0[0].reshape(nl, n, h, 1), m0[1]), h, 1)])
    out, c, nn, m = pl.pallas_call(
        _mlstm_step_kernel,
        grid=(n,),
        in_specs=in_specs,
        out_specs=[per(h, DV_C), per(h, DK_C, DV_C), per(h, DK_C), per(h, 1)],
        out_shape=[jax.ShapeDtypeStruct((n, h, DV_C), BF16),
                   jax.ShapeDtypeStruct((n, h, DK_C, DV_C), F32),
                   jax.ShapeDtypeStruct((n, h, DK_C), F32),
                   jax.ShapeDtypeStruct((n, h, 1), F32)],
        compiler_params=_cparams(1),
        name="mlstm_step",
    )(*args)
    return out.reshape(n, VC), c, nn, m.reshape(n, h)


def _prep_weights(w):
    a_in = w['a_w_in']
    q, k, v, xr, gr = jnp.split(a_in, [QA, QA + KA, QA + 2 * KA, QA + 2 * KA + D_RNN], axis=-1)
    nc = w['c_w_in'].shape[0]
    gbias = jnp.concatenate([w['c_b_i'], w['c_b_f'], jnp.zeros((nc, GATE_PAD - 2 * N_HEADS_C), F32)], axis=-1)
    return {
        'a_w_in': jnp.concatenate([q, xr, gr, k, v], axis=-1).astype(BF16),
        'a_w_out': w['a_w_out'].astype(BF16),
        'c_w_in': None,
        'c_w_gate': None,
        'c_gbias': gbias.reshape(nc, 1, GATE_PAD),
        'c_w_out': w['c_w_out'].astype(BF16),
        'ffn_w_up': [],
        'ffn_w_down': [],
        'ffn_cp': _ffn_conv_params(w['ffn_conv_w'], w['ffn_conv_b']),
        'lru_wa': jax.vmap(_lru_blockdiag)(w['a_lru_wa']),
        'lru_wx': jax.vmap(_lru_blockdiag)(w['a_lru_wx']),
    }


def _trunk(x, seq_len, state, w, pw):
    m = x.shape[0]
    nbatch = m // seq_len
    outs = {'k': [], 'v': [], 'h': [], 'conv': [], 'C': [], 'n': [], 'm': [], 'f': []}

    def vec(name, idx):
        a = w[name]
        return a.reshape(a.shape[0], 1, a.shape[1]), idx

    for l in range(DEPTH):
        j = l // 2
        if l % 2 == 0:
            proj = norm_matmul(x, vec('norm_mix_pre', l), (pw['a_w_in'], j), IN_A, _proj_tn(m, IN_A))
            lru_args = ((w['a_conv_w'], j), vec('a_conv_b', j), (pw['lru_wa'], j), (pw['lru_wx'], j),
                        vec('a_lru_ba', j), vec('a_lru_bx', j), vec('a_lru_lambda', j))
            if state is None:
                first = ((w['ffn_w_up'], 0, 2 * D_FF, 0), (w['ffn_w_down'], 0, D_MODEL, 0)) if l == 0 else None
                att, k_new, v_new, cast = swa_prompt(proj, w['a_sinks'][j], nbatch, seq_len, cast_f32=first,
                                                     cast_valid=('interleave', None) if l == 0 else ())
                if cast:
                    pw['ffn_w_up'].append(cast[0])
                    pw['ffn_w_down'].append(cast[1])
                job = (jnp.swapaxes(w['c_w_in'], 1, 2), IN_C_MAIN) if l == 0 else None
                rec, h_new, c_new, cast = lru_seq(proj, nbatch, seq_len, *lru_args, wt_job=job)
                if cast:
                    pw['c_w_in'], pw['c_w_gate'] = cast
            else:
                att, k_new, v_new = swa_sample(proj, (state['cache_k'], j), (state['cache_v'], j), w['a_sinks'][j])
                rec, h_new, c_new = lru_step(proj, (state['lru_h'], j), (state['lru_conv'], j), *lru_args)
            outs['k'].append(k_new.reshape(nbatch, -1, N_KV_A, HD_A))
            outs['v'].append(v_new.reshape(nbatch, -1, N_KV_A, HD_A))
            outs['h'].append(h_new)
            outs['conv'].append(c_new)
            x = out_proj([att, rec], (pw['a_w_out'], j), x, vec('norm_mix_post', l))
        else:
            proj, gates = norm_matmul(x, vec('norm_mix_pre', l), (pw['c_w_in'], j), IN_C_MAIN,
                                      _proj_tn(m, IN_C_MAIN), w_side=(pw['c_w_gate'], j))
            mix_args = ((pw['c_gbias'], j), vec('c_b_o', j), vec('c_norm', j))
            if state is None:
                hg, c_new, n_new, m_new = mlstm_seq(proj, gates, nbatch, seq_len, *mix_args)
            else:
                hg, c_new, n_new, m_new = mlstm_step(proj, gates, *mix_args, (state['mlstm_C'], j),
                                                     (state['mlstm_n'], j), (state['mlstm_m'], j))
            outs['C'].append(c_new)
            outs['n'].append(n_new)
            outs['m'].append(m_new)
            x = out_proj([hg], (pw['c_w_out'], j), x, vec('norm_mix_post', l))
        if state is None:
            nxt = ((w['ffn_w_up'], l + 1), (w['ffn_w_down'], l + 1)) if l + 1 < DEPTH else None
            x, f_new, cast = ffn_seq(x, seq_len, vec('norm_ffn_pre', l), pw['ffn_w_up'][l], (pw['ffn_cp'], l),
                                     pw['ffn_w_down'][l], vec('norm_ffn_post', l), next_f32=nxt)
            if cast:
                pw['ffn_w_up'].append(cast[0])
                pw['ffn_w_down'].append(cast[1])
        else:
            x, f_new = ffn_step(x, (state['ffn_conv'], l), vec('norm_ffn_pre', l), pw['ffn_w_up'][l],
                                (w['ffn_conv_w'], l), vec('ffn_conv_b', l), pw['ffn_w_down'][l],
                                vec('norm_ffn_post', l))
        outs['f'].append(f_new)
    return x, {name: jnp.stack(vals) for name, vals in outs.items()}


def kernel(x_prompt, x_sample, cache_k, cache_v, state_lru_h, state_lru_conv, state_mlstm_C, state_mlstm_n, state_mlstm_m, state_ffn_conv, norm_mix_pre, norm_mix_post, norm_ffn_pre, norm_ffn_post, a_w_in, a_sinks, a_conv_w, a_conv_b, a_lru_wa, a_lru_ba, a_lru_wx, a_lru_bx, a_lru_lambda, a_w_out, c_w_in, c_b_i, c_b_f, c_b_o, c_norm, c_w_out, ffn_w_up, ffn_conv_w, ffn_conv_b, ffn_w_down):
    w = {
        'norm_mix_pre': norm_mix_pre, 'norm_mix_post': norm_mix_post,
        'norm_ffn_pre': norm_ffn_pre, 'norm_ffn_post': norm_ffn_post,
        'a_w_in': a_w_in, 'a_sinks': a_sinks, 'a_conv_w': a_conv_w, 'a_conv_b': a_conv_b,
        'a_lru_wa': a_lru_wa, 'a_lru_ba': a_lru_ba, 'a_lru_wx': a_lru_wx, 'a_lru_bx': a_lru_bx,
        'a_lru_lambda': a_lru_lambda, 'a_w_out': a_w_out,
        'c_w_in': c_w_in, 'c_b_i': c_b_i, 'c_b_f': c_b_f, 'c_b_o': c_b_o, 'c_norm': c_norm, 'c_w_out': c_w_out,
        'ffn_w_up': ffn_w_up, 'ffn_conv_w': ffn_conv_w, 'ffn_conv_b': ffn_conv_b, 'ffn_w_down': ffn_w_down,
    }
    pw = _prep_weights(w)
    bp, sp, d = x_prompt.shape
    bs, ss, _ = x_sample.shape
    assert ss == 1, "the sample group advances one token per step"
    y_p, st_p = _trunk(x_prompt.reshape(bp * sp, d), sp, None, w, pw)
    state = {'cache_k': cache_k, 'cache_v': cache_v, 'lru_h': state_lru_h, 'lru_conv': state_lru_conv,
             'mlstm_C': state_mlstm_C, 'mlstm_n': state_mlstm_n, 'mlstm_m': state_mlstm_m,
             'ffn_conv': state_ffn_conv}
    y_s, st_s = _trunk(x_sample.reshape(bs * ss, d), 1, state, w, pw)
    return (y_p.reshape(bp, sp, d), y_s.reshape(bs, ss, d),
            st_p['k'], st_s['k'], st_p['v'], st_s['v'], st_p['h'], st_s['h'],
            st_p['conv'], st_s['conv'], st_p['C'], st_s['C'], st_p['n'], st_s['n'],
            st_p['m'], st_s['m'], st_p['f'], st_s['f'])
```

```python
import functools
import math

import jax
import jax.numpy as jnp
from jax import lax
from jax.experimental import pallas as pl
from jax.experimental.pallas import tpu as pltpu

F32 = jnp.float32
BF16 = jnp.bfloat16

D_MODEL = 2048
DEPTH = 4
PAST_LEN = 16384
N_HEADS_A = 16
N_KV_A = 4
HD_A = 64
GROUP_A = N_HEADS_A // N_KV_A
WINDOW = 128
BLOCK_A = 128
ROPE_THETA = 10000.0
D_RNN = 1024
N_BLOCKS_B = 16
BS_B = D_RNN // N_BLOCKS_B
CONV_B = 4
LRU_C = 8.0
N_HEADS_C = 8
DK_C = 128
DV_C = 256
D_FF = 6144
CONV_F = 3
EPS = 1e-6
QA = N_HEADS_A * HD_A
KA = N_KV_A * HD_A
IN_A = QA + 2 * KA + 2 * D_RNN
QC = N_HEADS_C * DK_C
VC = N_HEADS_C * DV_C
IN_C_MAIN = 2 * QC + 2 * VC

LANES = 128
SUBLANES = 8
VMEM_LIMIT_BYTES = 60 * 1024 * 1024

TM = 512
TM_PROJ = 1024
TF = 1024
FFN_SUB = 1024
LRU_TT = 256
MLSTM_L = 256
LRU_GROUP = 256
GATE_PAD = LANES
SAMPLES_PER_STEP = 4


def _cparams(n_axes):
    return pltpu.CompilerParams(dimension_semantics=("arbitrary",) * n_axes,
                                vmem_limit_bytes=VMEM_LIMIT_BYTES)


def _rms(x, g):
    ms = jnp.mean(x * x, axis=-1, keepdims=True)
    return x * lax.rsqrt(ms + EPS) * g


def _gelu_tanh(x):
    c = math.sqrt(2.0 / math.pi)
    hx = 0.5 * x
    return hx + hx * jnp.tanh(x * (c + (c * 0.044715) * (x * x)))


def _sigmoid(x):
    return 0.5 + 0.5 * jnp.tanh(0.5 * x)


def _log_sigmoid(x):
    return jnp.minimum(x, 0.0) - jnp.log1p(jnp.exp(-jnp.abs(x)))


def _norm_matmul_kernel(x_ref, g_ref, w_ref, *rest, with_side):
    o_ref, hn_ref = rest[1 if with_side else 0], rest[-1]

    @pl.when(pl.program_id(1) == 0)
    def _():
        hn_ref[...] = _rms(x_ref[...], g_ref[...]).astype(BF16)
        if with_side:
            rest[2][...] = jnp.dot(hn_ref[...], rest[0][...], preferred_element_type=F32)

    o_ref[...] = jnp.dot(hn_ref[...], w_ref[...], preferred_element_type=F32)


def _proj_tn(m, n):
    del m
    budget = 8 * 1024 * 1024
    best = LANES
    for tn in range(LANES, n + 1, LANES):
        if n % tn == 0 and D_MODEL * tn * 2 <= budget:
            best = tn
    return best


def _layer(opnd, block, imap):
    arr, idx = opnd
    return pl.BlockSpec((None,) + tuple(block), lambda *g: (idx,) + tuple(imap(*g))), arr


def _unzip(pairs):
    return [p[0] for p in pairs], [p[1] for p in pairs]


def _cast_slabs(srcs, dsts, valid):
    for src, dst, nv in zip(srcs, dsts, valid):
        if nv == 'interleave':
            for c in range(D_FF // TF):
                dst[:, 2 * c * TF:(2 * c + 1) * TF] = src[:, c * TF:(c + 1) * TF].astype(BF16)
                dst[:, (2 * c + 1) * TF:(2 * c + 2) * TF] = src[:, D_FF + c * TF:D_FF + (c + 1) * TF].astype(BF16)
            continue
        x = src[...]
        if nv is not None:
            x = jnp.where(lax.broadcasted_iota(jnp.int32, x.shape, 1) < nv, x, 0.0)
        dst[...] = x.astype(BF16)


def _add_cast_job(jobs, nsteps, step_of, in_specs, args, out_specs, out_shape):
    for arr, layer, ncols, cblk in jobs:
        rows = arr.shape[-2]
        rw, rem = divmod(rows, nsteps)
        assert rem == 0 and rw % (2 * SUBLANES) == 0 and ncols % LANES == 0, (rows, nsteps, ncols)
        if layer is None:
            in_specs.append(pl.BlockSpec((rw, ncols), lambda *g, _c=cblk: (step_of(*g), _c)))
        else:
            in_specs.append(pl.BlockSpec((None, rw, ncols), lambda *g, _l=layer, _c=cblk: (_l, step_of(*g), _c)))
        args.append(arr)
        out_specs.append(pl.BlockSpec((rw, ncols), lambda *g: (step_of(*g), 0)))
        out_shape.append(jax.ShapeDtypeStruct((rows, ncols), BF16))


def norm_matmul(x, g, w, n, tn, w_side=None):
    m, k = x.shape
    assert n % tn == 0 and n <= w[0].shape[-1]
    tm = min(TM_PROJ, m)
    specs, args = _unzip([(pl.BlockSpec((tm, k), lambda i, j: (i, 0)), x),
                          _layer(g, (1, k), lambda i, j: (0, 0)),
                          _layer(w, (k, tn), lambda i, j: (0, j))])
    in_specs = list(specs)
    out_specs = [pl.BlockSpec((tm, tn), lambda i, j: (i, j))]
    out_shape = [jax.ShapeDtypeStruct((m, n), F32)]
    if w_side is not None:
        ns = w_side[0].shape[-1]
        spec, arr = _layer(w_side, (k, ns), lambda i, j: (0, 0))
        in_specs.append(spec)
        out_specs.append(pl.BlockSpec((tm, ns), lambda i, j: (i, 0)))
        out_shape.append(jax.ShapeDtypeStruct((m, ns), F32))
        args.append(arr)
    outs = pl.pallas_call(
        functools.partial(_norm_matmul_kernel, with_side=w_side is not None),
        grid=(m // tm, n // tn),
        in_specs=in_specs,
        out_specs=out_specs,
        out_shape=out_shape,
        scratch_shapes=[pltpu.VMEM((tm, k), BF16)],
        compiler_params=_cparams(2),
        name="norm_matmul",
    )(*args)
    return outs[0] if w_side is None else outs


def _out_proj_kernel(*refs, n_in):
    a_refs = refs[:n_in]
    w_ref, x_ref, g_ref, o_ref = refs[n_in:]
    acc = None
    row = 0
    for a_ref in a_refs:
        ka = a_ref.shape[1]
        part = jnp.dot(a_ref[...], w_ref[row:row + ka, :], preferred_element_type=F32)
        acc = part if acc is None else acc + part
        row += ka
    o_ref[...] = x_ref[...] + _rms(acc, g_ref[...])


def out_proj(inputs, w, x, g):
    m, d = x.shape
    tm = min(TM, m)
    w_spec, w_arr = _layer(w, w[0].shape[1:], lambda i: (0, 0))
    g_spec, g_arr = _layer(g, (1, d), lambda i: (0, 0))
    in_specs = [pl.BlockSpec((tm, a.shape[1]), lambda i: (i, 0)) for a in inputs]
    in_specs += [w_spec, pl.BlockSpec((tm, d), lambda i: (i, 0)), g_spec]
    return pl.pallas_call(
        functools.partial(_out_proj_kernel, n_in=len(inputs)),
        grid=(m // tm,),
        in_specs=in_specs,
        out_specs=pl.BlockSpec((tm, d), lambda i: (i, 0)),
        out_shape=jax.ShapeDtypeStruct((m, d), F32),
        compiler_params=_cparams(1),
        name="out_proj",
    )(*inputs, w_arr, x, g_arr)


def _ffn_conv_params(convw, convb):
    nl = convw.shape[0]
    p = jnp.concatenate([convw, convb[:, None, :], jnp.zeros((nl, SUBLANES - CONV_F - 1, 2 * D_FF), F32)], axis=1)
    return jnp.swapaxes(p.reshape(nl, SUBLANES, 2 * D_FF // TF, TF), 1, 2)


def _ffn_conv(u, prev, cp):
    cw = cp[0:CONV_F, :]
    cb = cp[CONV_F:CONV_F + 1, :]
    y = pltpu.roll(u, 2, axis=0) * cw[0:1, :] + pltpu.roll(u, 1, axis=0) * cw[1:2, :] + u * cw[2:3, :] + cb
    head = jnp.concatenate([prev, u[0:SUBLANES, :]], axis=0)
    y_head = (head[SUBLANES - 2:2 * SUBLANES - 2, :] * cw[0:1, :] + head[SUBLANES - 1:2 * SUBLANES - 1, :] * cw[1:2, :]
              + head[SUBLANES:, :] * cw[2:3, :] + cb)
    return jnp.concatenate([y_head, y[SUBLANES:, :]], axis=0)


def _ffn_seq_kernel(x_ref, gpre_ref, wgv_ref, cp_ref, wd_ref, gpost_ref,
                    *rest, tiles_per_seq, cast_next, n_chunks):
    if cast_next:
        wupn_ref, wdn_ref, o_ref, nu_ref, wupn_out, wdn_out, hn_ref, carry_ref = rest
        wupn_out[...] = wupn_ref[...].astype(BF16)
        wdn_out[...] = wdn_ref[...].astype(BF16)
    else:
        o_ref, nu_ref, hn_ref, carry_ref = rest
    i = pl.program_id(0)
    f = pl.program_id(1)
    nf = n_chunks
    tm = x_ref.shape[0]

    @pl.when(f == 0)
    def _():
        hn_ref[...] = _rms(x_ref[...], gpre_ref[...]).astype(BF16)
        o_ref[...] = jnp.zeros_like(o_ref)

    @pl.when(i % tiles_per_seq == 0)
    def _():
        carry_ref[f] = jnp.zeros(carry_ref.shape[1:], F32)

    u = jnp.dot(hn_ref[...], wgv_ref[...], preferred_element_type=F32)
    tf = u.shape[1] // 2
    ug, uv = u[:, :tf], u[:, tf:]
    cg = _ffn_conv(ug, carry_ref[f, 0], cp_ref[f])
    cv = _ffn_conv(uv, carry_ref[f, 1], cp_ref[f + nf])
    carry_ref[f, 0] = ug[tm - SUBLANES:tm, :]
    carry_ref[f, 1] = uv[tm - SUBLANES:tm, :]
    nu_ref[0] = ug[tm - (CONV_F - 1):tm, :]
    nu_ref[1] = uv[tm - (CONV_F - 1):tm, :]
    act = (_gelu_tanh(cg) * cv).astype(BF16)
    o_ref[...] += jnp.dot(act, wd_ref[...], preferred_element_type=F32)

    @pl.when(f == nf - 1)
    def _():
        o_ref[...] = x_ref[...] + _rms(o_ref[...], gpost_ref[...])


def ffn_seq(x, seq_len, gpre, wup, cp, wdown, gpost, next_f32=None):
    m, d = x.shape
    tm = min(TM, seq_len)
    tps = seq_len // tm
    nb = m // seq_len
    nf = D_FF // TF
    nsteps = (m // tm) * nf
    cast_next = next_f32 is not None
    kern = functools.partial(_ffn_seq_kernel, tiles_per_seq=tps, cast_next=cast_next, n_chunks=nf)
    up = dn = lambda f: f
    in_specs, args = _unzip([
        (pl.BlockSpec((tm, d), lambda i, f: (i, 0)), x),
        _layer(gpre, (1, d), lambda i, f: (0, 0)),
        (pl.BlockSpec((d, 2 * TF), lambda i, f: (0, up(f))), wup),
        _layer(cp, cp[0].shape[1:], lambda i, f: (0, 0, 0)),
        (pl.BlockSpec((TF, d), lambda i, f: (dn(f), 0)), wdown),
        _layer(gpost, (1, d), lambda i, f: (0, 0))])
    out_specs = [pl.BlockSpec((tm, d), lambda i, f: (i, 0)),
                 pl.BlockSpec((None, 2, CONV_F - 1, TF), lambda i, f: (i, 0, 0, up(f)))]
    out_shape = [jax.ShapeDtypeStruct((m, d), F32),
                 jax.ShapeDtypeStruct((m // tm, 2, CONV_F - 1, D_FF), F32)]
    if cast_next:
        cw, rem_c = divmod(2 * D_FF, nsteps)
        rw, rem_r = divmod(D_FF, nsteps)
        assert rem_c == 0 and rem_r == 0 and cw % LANES == 0 and rw % (2 * SUBLANES) == 0, (nsteps, cw, rw)
        up_in, up_arr = _layer(next_f32[0], (d, cw), lambda i, f: (0, i * nf + up(f)))
        dn_in, dn_arr = _layer(next_f32[1], (rw, d), lambda i, f: (i * nf + up(f), 0))
        in_specs += [up_in, dn_in]
        args += [up_arr, dn_arr]
        per, half = TF // cw, D_FF // cw
        assert TF % cw == 0

        def up_dst(i, f):
            s = i * nf + up(f)
            return 0, ((s % half) // per) * (2 * per) + (s // half) * per + s % per

        out_specs += [pl.BlockSpec((d, cw), up_dst),
                      pl.BlockSpec((rw, d), lambda i, f: (i * nf + up(f), 0))]
        out_shape += [jax.ShapeDtypeStruct((d, 2 * D_FF), BF16), jax.ShapeDtypeStruct((D_FF, d), BF16)]
    outs = pl.pallas_call(
        kern,
        grid=(m // tm, nf),
        in_specs=in_specs,
        out_specs=out_specs,
        out_shape=out_shape,
        scratch_shapes=[pltpu.VMEM((tm, d), BF16),
                        pltpu.VMEM((nf, 2, SUBLANES, TF), F32)],
        compiler_params=_cparams(2),
        name="ffn_seq",
    )(*args)
    y, nu = outs[:2]
    last = nu.reshape(nb, tps, 2, CONV_F - 1, D_FF)[:, tps - 1]
    last = jnp.swapaxes(last, 1, 2).reshape(nb, CONV_F - 1, 2 * D_FF)
    return y, last, tuple(outs[2:])


def _ffn_step_kernel(x_ref, gpre_ref, wg_ref, wv_ref, cwg_ref, cwv_ref, cbg_ref, cbv_ref, wd_ref, gpost_ref,
                     pg_ref, pv_ref,
                     o_ref, ng_ref, nv_ref, hn_ref, acc_ref):
    f = pl.program_id(0)
    nf = pl.num_programs(0)

    @pl.when(f == 0)
    def _():
        hn_ref[...] = _rms(x_ref[...], gpre_ref[...]).astype(BF16)
        acc_ref[...] = jnp.zeros_like(acc_ref)

    hn = hn_ref[...]
    ug = jnp.dot(hn, wg_ref[...], preferred_element_type=F32)
    uv = jnp.dot(hn, wv_ref[...], preferred_element_type=F32)
    p0g, p1g = pg_ref[:, 0, :], pg_ref[:, 1, :]
    p0v, p1v = pv_ref[:, 0, :], pv_ref[:, 1, :]
    ng_ref[:, 0, :] = p1g
    ng_ref[:, 1, :] = ug
    nv_ref[:, 0, :] = p1v
    nv_ref[:, 1, :] = uv
    cwg = cwg_ref[...]
    cwv = cwv_ref[...]
    cg = p0g * cwg[0:1, :] + p1g * cwg[1:2, :] + ug * cwg[2:3, :] + cbg_ref[...]
    cv = p0v * cwv[0:1, :] + p1v * cwv[1:2, :] + uv * cwv[2:3, :] + cbv_ref[...]
    act = (_gelu_tanh(cg) * cv).astype(BF16)
    acc_ref[...] += jnp.dot(act, wd_ref[...], preferred_element_type=F32)

    @pl.when(f == nf - 1)
    def _():
        o_ref[...] = x_ref[...] + _rms(acc_ref[...], gpost_ref[...])


def ffn_step(x, past, gpre, wup, convw, convb, wdown, gpost):
    n, d = x.shape
    nf = D_FF // TF
    phalf = lambda c: _layer(past, (n, CONV_F - 1, TF), lambda f: (0, 0, f + c * nf))
    in_specs, args = _unzip([
        (pl.BlockSpec((n, d), lambda f: (0, 0)), x),
        _layer(gpre, (1, d), lambda f: (0, 0)),
        (pl.BlockSpec((d, TF), lambda f: (0, 2 * f)), wup),
        (pl.BlockSpec((d, TF), lambda f: (0, 2 * f + 1)), wup),
        _layer(convw, (CONV_F, TF), lambda f: (0, f)),
        _layer(convw, (CONV_F, TF), lambda f: (0, f + nf)),
        _layer(convb, (1, TF), lambda f: (0, f)),
        _layer(convb, (1, TF), lambda f: (0, f + nf)),
        (pl.BlockSpec((TF, d), lambda f: (f, 0)), wdown),
        _layer(gpost, (1, d), lambda f: (0, 0)),
        phalf(0), phalf(1)])
    y, ng, nv = pl.pallas_call(
        _ffn_step_kernel,
        grid=(nf,),
        in_specs=in_specs,
        out_specs=[pl.BlockSpec((n, d), lambda f: (0, 0)),
                   pl.BlockSpec((n, CONV_F - 1, TF), lambda f: (0, 0, f)),
                   pl.BlockSpec((n, CONV_F - 1, TF), lambda f: (0, 0, f))],
        out_shape=[jax.ShapeDtypeStruct((n, d), F32),
                   jax.ShapeDtypeStruct((n, CONV_F - 1, D_FF), F32),
                   jax.ShapeDtypeStruct((n, CONV_F - 1, D_FF), F32)],
        scratch_shapes=[pltpu.VMEM((n, d), BF16), pltpu.VMEM((n, d), F32)],
        compiler_params=_cparams(1),
        name="ffn_step",
    )(*args)
    return y, jnp.concatenate([ng, nv], axis=-1)


def _rope_tables(pos0, t):
    half = HD_A // 2
    pos = (pos0 + jnp.arange(t)).astype(F32)
    inv = ROPE_THETA ** (-jnp.arange(half, dtype=F32) / half)
    ang = pos[:, None] * inv[None, :]
    cos, sin = jnp.cos(ang), jnp.sin(ang)
    zero = jnp.zeros_like(sin)
    cos_h = jnp.concatenate([cos, cos], axis=-1)
    sin_up = jnp.concatenate([-sin, zero], axis=-1)
    sin_dn = jnp.concatenate([zero, sin], axis=-1)
    two = lambda a: jnp.concatenate([a, a], axis=-1)
    return two(cos_h), two(sin_up), two(sin_dn)


def _rope(x, cos, sin_up, sin_dn):
    w = x.shape[1]
    reps = w // LANES
    tile = lambda a: jnp.concatenate([a] * reps, axis=1) if reps > 1 else a
    x_up = pltpu.roll(x, w - HD_A // 2, axis=1)
    x_dn = pltpu.roll(x, HD_A // 2, axis=1)
    return x * tile(cos) + x_up * tile(sin_up) + x_dn * tile(sin_dn)


def _dup_half(a, half):
    lane = lax.broadcasted_iota(jnp.int32, a.shape, 1)
    sw = pltpu.roll(a, HD_A, axis=1)
    lo = lane < HD_A
    return jnp.where(lo, a, sw) if half == 0 else jnp.where(lo, sw, a)


def _sink_softmax(s, sk_rows):
    m = jnp.maximum(jnp.max(s, axis=-1, keepdims=True), sk_rows)
    p = jnp.exp(s - m)
    return p, jnp.sum(p, axis=-1, keepdims=True) + jnp.exp(sk_rows - m)


def _swa_prompt_kernel(sinks_ref, q_ref, kc_ref, kp_ref, vc_ref, vp_ref,
                       cc_ref, suc_ref, sdc_ref, cp_ref, sup_ref, sdp_ref, *rest, cast_valid):
    nc = len(cast_valid)
    att_ref, knew_ref, vnew_ref = rest[nc:nc + 3]
    _cast_slabs(rest[:nc], rest[nc + 3:], cast_valid)
    c = pl.program_id(1)
    nb = pl.num_programs(1)
    bq = q_ref.shape[0]
    q = _rope(q_ref[...], cc_ref[...], suc_ref[...], sdc_ref[...])
    kc = _rope(kc_ref[...], cc_ref[...], suc_ref[...], sdc_ref[...])
    kp = _rope(kp_ref[...], cp_ref[...], sup_ref[...], sdp_ref[...])
    vc = vc_ref[...]
    vp = vp_ref[...]

    @pl.when(c == nb - 1)
    def _():
        knew_ref[...] = kc
        vnew_ref[...] = vc

    rows = GROUP_A * bq
    r = lax.broadcasted_iota(jnp.int32, (rows, 2 * bq), 0) % bq
    col = lax.broadcasted_iota(jnp.int32, (rows, 2 * bq), 1)
    valid = (col >= r) & (col <= r + WINDOW) & ((c > 0) | (col >= bq))
    lane = lax.broadcasted_iota(jnp.int32, (bq, LANES), 1)
    lo = lane < HD_A
    row_head = lax.broadcasted_iota(jnp.int32, (rows, 1), 0) // bq

    for h in range(N_KV_A):
        t, half = divmod(h, 2)
        ksl = slice(t * LANES, (t + 1) * LANES)
        kd = _dup_half(jnp.concatenate([kp[:, ksl], kc[:, ksl]], axis=0), half).astype(BF16)
        vd = _dup_half(jnp.concatenate([vp[:, ksl], vc[:, ksl]], axis=0), half).astype(BF16)
        parts = []
        for p in range(2):
            q2 = q[:, (2 * h + p) * LANES:(2 * h + p + 1) * LANES]
            parts.append(jnp.where(lo, q2, 0.0))
            parts.append(jnp.where(lo, 0.0, q2))
        qs = (jnp.concatenate(parts, axis=0) * (HD_A ** -0.5)).astype(BF16)
        s = lax.dot_general(qs, kd, (((1,), (1,)), ((), ())), preferred_element_type=F32)
        s = jnp.where(valid, s, -jnp.inf)
        sk = jnp.zeros((rows, 1), F32)
        for g in range(GROUP_A):
            sk = jnp.where(row_head == g, sinks_ref[h * GROUP_A + g], sk)
        pr, den = _sink_softmax(s, sk)
        o = jnp.dot(pr.astype(BF16), vd, preferred_element_type=F32) / den
        for p in range(2):
            o_lo = o[(2 * p) * bq:(2 * p + 1) * bq, :]
            o_hi = o[(2 * p + 1) * bq:(2 * p + 2) * bq, :]
            att_ref[:, (2 * h + p) * LANES:(2 * h + p + 1) * LANES] = jnp.where(lo, o_lo, o_hi).astype(BF16)


def swa_prompt(proj, sinks, nbatch, seq_len, cast_f32=None, cast_valid=()):
    nb = seq_len // BLOCK_A
    p3 = proj.reshape(nbatch, seq_len, IN_A)
    kcol = (QA + 2 * D_RNN) // KA
    cos, sup, sdn = _rope_tables(0, seq_len)
    prev = lambda c: jnp.maximum(c - 1, 0)
    tab_c = pl.BlockSpec((BLOCK_A, LANES), lambda b, c: (c, 0))
    tab_p = pl.BlockSpec((BLOCK_A, LANES), lambda b, c: (prev(c), 0))
    in_specs = [pl.BlockSpec(memory_space=pltpu.SMEM),
                pl.BlockSpec((None, BLOCK_A, QA), lambda b, c: (b, c, 0)),
                pl.BlockSpec((None, BLOCK_A, KA), lambda b, c: (b, c, kcol)),
                pl.BlockSpec((None, BLOCK_A, KA), lambda b, c: (b, prev(c), kcol)),
                pl.BlockSpec((None, BLOCK_A, KA), lambda b, c: (b, c, kcol + 1)),
                pl.BlockSpec((None, BLOCK_A, KA), lambda b, c: (b, prev(c), kcol + 1)),
                tab_c, tab_c, tab_c, tab_p, tab_p, tab_p]
    args = [sinks, p3, p3, p3, p3, p3, cos, sup, sdn, cos, sup, sdn]
    out_specs = [pl.BlockSpec((None, BLOCK_A, QA), lambda b, c: (b, c, 0)),
                 pl.BlockSpec((None, BLOCK_A, KA), lambda b, c: (b, 0, 0)),
                 pl.BlockSpec((None, BLOCK_A, KA), lambda b, c: (b, 0, 0))]
    out_shape = [jax.ShapeDtypeStruct((nbatch, seq_len, QA), BF16),
                 jax.ShapeDtypeStruct((nbatch, BLOCK_A, KA), F32),
                 jax.ShapeDtypeStruct((nbatch, BLOCK_A, KA), F32)]
    if cast_f32 is not None:
        _add_cast_job(cast_f32, nbatch * nb, lambda b, c: b * nb + c, in_specs, args, out_specs, out_shape)
    outs = pl.pallas_call(
        functools.partial(_swa_prompt_kernel, cast_valid=tuple(cast_valid)),
        grid=(nbatch, nb),
        in_specs=in_specs,
        out_specs=out_specs,
        out_shape=out_shape,
        compiler_params=_cparams(2),
        name="swa_prompt",
    )(*args)
    att, knew, vnew = outs[:3]
    return att.reshape(nbatch * seq_len, QA), knew, vnew, tuple(outs[3:])


def _swa_sample_kernel(sinks_ref, q_ref, k_ref, v_ref, kc_ref, vc_ref, cos_ref, sup_ref, sdn_ref,
                       att_ref, knew_ref, vnew_ref):
    w = kc_ref.shape[1]
    lane = lax.broadcasted_iota(jnp.int32, (1, LANES), 1)
    lo = lane < HD_A
    row = lax.broadcasted_iota(jnp.int32, (SUBLANES, 1), 0)
    scale = HD_A ** -0.5
    todo = []
    for z in range(q_ref.shape[0]):
        q = _rope(q_ref[z], cos_ref[...], sup_ref[...], sdn_ref[...])
        k = _rope(k_ref[z], cos_ref[...], sup_ref[...], sdn_ref[...])
        v = v_ref[z]
        knew_ref[z, 0:w - 1, :] = kc_ref[z, 1:w, :]
        knew_ref[z, w - 1:w, :] = k
        vnew_ref[z, 0:w - 1, :] = vc_ref[z, 1:w, :]
        vnew_ref[z, w - 1:w, :] = v
        for h in range(N_KV_A):
            t, half = divmod(h, 2)
            ksl = slice(t * LANES, (t + 1) * LANES)
            kd = _dup_half(kc_ref[z, :, ksl], half)
            kd_new = _dup_half(k[:, ksl], half)
            parts = []
            for p in range(2):
                q2 = q[:, (2 * h + p) * LANES:(2 * h + p + 1) * LANES]
                parts.append(jnp.where(lo, q2, 0.0))
                parts.append(jnp.where(lo, 0.0, q2))
            parts.append(jnp.zeros((SUBLANES - GROUP_A, LANES), F32))
            qs = jnp.concatenate(parts, axis=0)
            s = lax.dot_general(qs.astype(BF16), kd.astype(BF16), (((1,), (1,)), ((), ())),
                                preferred_element_type=F32) * scale
            s_new = jnp.sum(qs * kd_new, axis=-1, keepdims=True) * scale
            todo.append((z, h, s, s_new, v))
    probs = []
    for z, h, s, s_new, v in todo:
        sk = jnp.zeros((SUBLANES, 1), F32)
        for g in range(GROUP_A):
            sk = jnp.where(row == g, sinks_ref[h * GROUP_A + g], sk)
        m = jnp.maximum(jnp.maximum(jnp.max(s, axis=-1, keepdims=True), s_new), sk)
        pr = jnp.exp(s - m)
        p_new = jnp.exp(s_new - m)
        den = jnp.sum(pr, axis=-1, keepdims=True) + p_new + jnp.exp(sk - m)
        probs.append((z, h, (pr / den).astype(BF16), p_new / den, v))
    for z, h, pr, p_new, v in probs:
        t, half = divmod(h, 2)
        ksl = slice(t * LANES, (t + 1) * LANES)
        vd = _dup_half(vc_ref[z, :, ksl], half)
        vd_new = _dup_half(v[:, ksl], half)
        o = jnp.dot(pr, vd.astype(BF16), preferred_element_type=F32) + p_new * vd_new
        for p in range(2):
            att_ref[z, :, (2 * h + p) * LANES:(2 * h + p + 1) * LANES] = jnp.where(
                lo, o[2 * p:2 * p + 1, :], o[2 * p + 1:2 * p + 2, :]).astype(BF16)


def swa_sample(proj, cache_k, cache_v, sinks):
    n = proj.shape[0]
    nl, _, w = cache_k[0].shape[:3]
    p3 = proj.reshape(n, 1, IN_A)
    kcol = (QA + 2 * D_RNN) // KA
    ck_spec, ck = _layer((cache_k[0].reshape(nl, n, w, KA), cache_k[1]), (SAMPLES_PER_STEP, w, KA),
                         lambda z: (z, 0, 0))
    cv_spec, cv = _layer((cache_v[0].reshape(nl, n, w, KA), cache_v[1]), (SAMPLES_PER_STEP, w, KA),
                         lambda z: (z, 0, 0))
    cos, sup, sdn = _rope_tables(PAST_LEN, 1)
    tab = pl.BlockSpec((1, LANES), lambda z: (0, 0))
    gs = SAMPLES_PER_STEP
    assert n % gs == 0
    att, knew, vnew = pl.pallas_call(
        _swa_sample_kernel,
        grid=(n // gs,),
        in_specs=[pl.BlockSpec(memory_space=pltpu.SMEM),
                  pl.BlockSpec((gs, 1, QA), lambda z: (z, 0, 0)),
                  pl.BlockSpec((gs, 1, KA), lambda z: (z, 0, kcol)),
                  pl.BlockSpec((gs, 1, KA), lambda z: (z, 0, kcol + 1)),
                  ck_spec, cv_spec,
                  tab, tab, tab],
        out_specs=[pl.BlockSpec((gs, 1, QA), lambda z: (z, 0, 0)),
                   pl.BlockSpec((gs, w, KA), lambda z: (z, 0, 0)),
                   pl.BlockSpec((gs, w, KA), lambda z: (z, 0, 0))],
        out_shape=[jax.ShapeDtypeStruct((n, 1, QA), BF16),
                   jax.ShapeDtypeStruct((n, w, KA), F32),
                   jax.ShapeDtypeStruct((n, w, KA), F32)],
        compiler_params=_cparams(1),
        name="swa_sample",
    )(sinks, p3, p3, p3, ck, cv, cos, sup, sdn)
    return att.reshape(n, QA), knew, vnew


def _lru_gates(xc, wa_ref, wx_ref, ba_ref, bx_ref, lam_ref):
    rs, is_ = [], []
    for c in range(D_RNN // LRU_GROUP):
        xg = xc[:, c * LRU_GROUP:(c + 1) * LRU_GROUP].astype(BF16)
        rs.append(jnp.dot(xg, wa_ref[c], preferred_element_type=F32))
        is_.append(jnp.dot(xg, wx_ref[c], preferred_element_type=F32))
    r = _sigmoid(jnp.concatenate(rs, axis=1) + ba_ref[...])
    i = _sigmoid(jnp.concatenate(is_, axis=1) + bx_ref[...])
    log_a = LRU_C * r * _log_sigmoid(lam_ref[...])
    a = jnp.exp(log_a)
    s = -jnp.tanh(log_a) * (a * a + 1.0)
    b = jnp.where(s > 0.0, s * lax.rsqrt(s), 0.0) * (i * xc)
    return a, b


def _lru_seq_kernel(xr_ref, gr_ref, cw_ref, cb_ref, wa_ref, wx_ref, ba_ref, bx_ref, lam_ref, *rest, side_job):
    if side_job:
        wt_ref, gt_ref, rec_ref, hlast_ref, cnew_ref, w_out, g_out, xbuf_ref, a_ref, b_ref, h_ref, hc_ref = rest
        w_out[...] = wt_ref[...].T.astype(BF16)
        gt = gt_ref[...]
        pad = jnp.zeros((GATE_PAD - gt.shape[0], gt.shape[1]), F32)
        g_out[...] = jnp.concatenate([gt, pad], axis=0).T.astype(BF16)
    else:
        rec_ref, hlast_ref, cnew_ref, xbuf_ref, a_ref, b_ref, h_ref, hc_ref = rest
    t = pl.program_id(1)
    tt = xr_ref.shape[0]

    @pl.when(t == 0)
    def _():
        xbuf_ref[0:SUBLANES, :] = jnp.zeros((SUBLANES, D_RNN), F32)
        hc_ref[...] = jnp.zeros_like(hc_ref)

    xr = xr_ref[...]
    xbuf_ref[SUBLANES:SUBLANES + tt, :] = xr
    cw = cw_ref[...]
    xc = xr * cw[CONV_B - 1:CONV_B, :] + cb_ref[...]
    for j in range(CONV_B - 1):
        off = SUBLANES - (CONV_B - 1) + j
        xc = xc + xbuf_ref[off:off + tt, :] * cw[j:j + 1, :]
    xbuf_ref[0:SUBLANES, :] = xr[tt - SUBLANES:tt, :]
    cnew_ref[...] = xr[tt - (CONV_B - 1):tt, :]

    a, b = _lru_gates(xc, wa_ref, wx_ref, ba_ref, bx_ref, lam_ref)
    a_ref[...] = a
    b_ref[...] = b
    row = lax.broadcasted_iota(jnp.int32, (SUBLANES, D_RNN), 0)

    def body(g, h):
        base = pl.multiple_of(g * SUBLANES, SUBLANES)
        a8 = a_ref[pl.ds(base, SUBLANES), :]
        b8 = b_ref[pl.ds(base, SUBLANES), :]
        for s in (1, 2, 4):
            a_sh = pltpu.roll(a8, s, axis=0)
            b_sh = pltpu.roll(b8, s, axis=0)
            keep = row >= s
            b8 = jnp.where(keep, a8 * b_sh + b8, b8)
            a8 = jnp.where(keep, a8 * a_sh, a8)
        h8 = b8 + a8 * h
        h_ref[pl.ds(base, SUBLANES), :] = h8
        return h8[SUBLANES - 1:SUBLANES, :]

    h_last = lax.fori_loop(0, tt // SUBLANES, body, hc_ref[...])
    hc_ref[...] = h_last
    hlast_ref[...] = h_last
    rec_ref[...] = (h_ref[...] * _gelu_tanh(gr_ref[...])).astype(BF16)


def _lru_blockdiag(w):
    per = LRU_GROUP // BS_B
    w4 = w.reshape(D_RNN // LRU_GROUP, per, BS_B, BS_B)
    eye = jnp.eye(per, dtype=w.dtype)
    return jnp.einsum('cpij,pq->cpiqj', w4, eye).reshape(D_RNN // LRU_GROUP, LRU_GROUP, LRU_GROUP).astype(BF16)


def lru_seq(proj, nbatch, seq_len, cw, cb, wa_bd, wx_bd, ba, bx, lam, wt_job=None):
    p3 = proj.reshape(nbatch, seq_len, IN_A)
    tt = min(LRU_TT, seq_len)
    nt = seq_len // tt
    vec = lambda o: _layer(o, (1, D_RNN), lambda b, t: (0, 0))
    wspec = lambda o: _layer(o, o[0].shape[1:], lambda b, t: (0, 0, 0))
    in_specs, args = _unzip([
        (pl.BlockSpec((None, tt, D_RNN), lambda b, t: (b, t, QA // D_RNN)), p3),
        (pl.BlockSpec((None, tt, D_RNN), lambda b, t: (b, t, QA // D_RNN + 1)), p3),
        _layer(cw, (CONV_B, D_RNN), lambda b, t: (0, 0)),
        vec(cb), wspec(wa_bd), wspec(wx_bd), vec(ba), vec(bx), vec(lam)])
    out_specs = [pl.BlockSpec((None, tt, D_RNN), lambda b, t: (b, t, 0)),
                 pl.BlockSpec((None, 1, D_RNN), lambda b, t: (b, 0, 0)),
                 pl.BlockSpec((None, CONV_B - 1, D_RNN), lambda b, t: (b, 0, 0))]
    out_shape = [jax.ShapeDtypeStruct((nbatch, seq_len, D_RNN), BF16),
                 jax.ShapeDtypeStruct((nbatch, 1, D_RNN), F32),
                 jax.ShapeDtypeStruct((nbatch, CONV_B - 1, D_RNN), F32)]
    if wt_job is not None:
        wt, n_main = wt_job
        nl, n_all, kin = wt.shape
        spl, rem = divmod(nbatch * nt, nl)
        rp, rem2 = divmod(n_main, spl)
        ng = n_all - n_main
        assert rem == 0 and rem2 == 0 and rp % LANES == 0 and n_main % ng == 0 and ng % SUBLANES == 0
        step = lambda b, t: b * nt + t
        in_specs += [pl.BlockSpec((None, rp, kin), lambda b, t: (step(b, t) // spl, step(b, t) % spl, 0)),
                     pl.BlockSpec((None, ng, kin), lambda b, t: (step(b, t) // spl, n_main // ng, 0))]
        args += [wt, wt]
        out_specs += [pl.BlockSpec((None, kin, rp), lambda b, t: (step(b, t) // spl, 0, step(b, t) % spl)),
                      pl.BlockSpec((None, kin, GATE_PAD), lambda b, t: (step(b, t) // spl, 0, 0))]
        out_shape += [jax.ShapeDtypeStruct((nl, kin, n_main), BF16), jax.ShapeDtypeStruct((nl, kin, GATE_PAD), BF16)]
    outs = pl.pallas_call(
        functools.partial(_lru_seq_kernel, side_job=wt_job is not None),
        grid=(nbatch, nt),
        in_specs=in_specs,
        out_specs=out_specs,
        out_shape=out_shape,
        scratch_shapes=[pltpu.VMEM((tt + SUBLANES, D_RNN), F32),
                        pltpu.VMEM((tt, D_RNN), F32),
                        pltpu.VMEM((tt, D_RNN), F32),
                        pltpu.VMEM((tt, D_RNN), F32),
                        pltpu.VMEM((1, D_RNN), F32)],
        compiler_params=_cparams(2),
        name="lru_seq",
    )(*args)
    rec, hlast, cnew = outs[:3]
    return rec.reshape(nbatch * seq_len, D_RNN), hlast.reshape(nbatch, D_RNN), cnew, tuple(outs[3:])


def _lru_step_kernel(xr_ref, gr_ref, past_ref, h0_ref, cw_ref, cb_ref,
                     wa_ref, wx_ref, ba_ref, bx_ref, lam_ref, rec_ref, h_ref, cnew_ref):
    cw = cw_ref[...]
    xr = xr_ref[...]
    xc = xr * cw[CONV_B - 1:CONV_B, :] + cb_ref[...]
    for r in range(CONV_B - 1):
        p = past_ref[:, r, :]
        xc = xc + p * cw[r:r + 1, :]
        if r > 0:
            cnew_ref[:, r - 1, :] = p
    cnew_ref[:, CONV_B - 2, :] = xr
    a, b = _lru_gates(xc, wa_ref, wx_ref, ba_ref, bx_ref, lam_ref)
    h = b + a * h0_ref[...]
    h_ref[...] = h
    rec_ref[...] = (h * _gelu_tanh(gr_ref[...])).astype(BF16)


def lru_step(proj, h0, conv0, cw, cb, wa_bd, wx_bd, ba, bx, lam):
    n = proj.shape[0]
    col = lambda j: pl.BlockSpec((n, D_RNN), lambda i: (0, j))
    vec = lambda o: _layer(o, (1, D_RNN), lambda i: (0, 0))
    wspec = lambda o: _layer(o, o[0].shape[1:], lambda i: (0, 0, 0))
    hist = pl.BlockSpec((n, CONV_B - 1, D_RNN), lambda i: (0, 0, 0))
    in_specs, args = _unzip([
        (col(QA // D_RNN), proj), (col(QA // D_RNN + 1), proj),
        _layer(conv0, (n, CONV_B - 1, D_RNN), lambda i: (0, 0, 0)),
        _layer(h0, (n, D_RNN), lambda i: (0, 0)),
        _layer(cw, (CONV_B, D_RNN), lambda i: (0, 0)),
        vec(cb), wspec(wa_bd), wspec(wx_bd), vec(ba), vec(bx), vec(lam)])
    rec, h, conv_new = pl.pallas_call(
        _lru_step_kernel,
        grid=(1,),
        in_specs=in_specs,
        out_specs=[col(0), col(0), hist],
        out_shape=[jax.ShapeDtypeStruct((n, D_RNN), BF16), jax.ShapeDtypeStruct((n, D_RNN), F32),
                   jax.ShapeDtypeStruct((n, CONV_B - 1, D_RNN), F32)],
        compiler_params=_cparams(1),
        name="lru_step",
    )(*args)
    return rec, h, conv_new


def _cumsum_lanes(x):
    n = x.shape[1]
    lane = lax.broadcasted_iota(jnp.int32, x.shape, 1)
    s = 1
    while s < n:
        x = x + jnp.where(lane >= s, pltpu.roll(x, s, axis=1), 0.0)
        s *= 2
    return x


def _cummax_lanes(x):
    n = x.shape[1]
    lane = lax.broadcasted_iota(jnp.int32, x.shape, 1)
    s = 1
    while s < n:
        x = jnp.maximum(x, jnp.where(lane >= s, pltpu.roll(x, s, axis=1), -jnp.inf))
        s *= 2
    return x


def _mlstm_seq_kernel(q_ref, k_ref, v_ref, o_ref, gate_ref, gbias_ref, bo_ref, gn_ref,
                      out_ref, c_ref, n_ref, m_ref, cx_ref):
    ci = pl.program_id(1)
    nh = N_HEADS_C
    ln = q_ref.shape[0]

    @pl.when(ci == 0)
    def _():
        cx_ref[...] = jnp.zeros_like(cx_ref)
        m_ref[...] = jnp.zeros_like(m_ref)

    gate_t = (gate_ref[...] + gbias_ref[...]).T
    i_rows = gate_t[0:nh, :]
    b_rows = _cumsum_lanes(_log_sigmoid(gate_t[nh:2 * nh, :]))
    m_prev = m_ref[:, 0:1]
    r_rows = i_rows - b_rows
    mt_rows = b_rows + jnp.maximum(m_prev, _cummax_lanes(r_rows))
    m_last = mt_rows[:, ln - 1:ln]
    b_last = b_rows[:, ln - 1:ln]
    decay = jnp.exp(b_last + m_prev - m_last)
    per_token = jnp.concatenate([
        b_rows - mt_rows,
        jnp.exp(b_rows + m_prev - mt_rows),
        jnp.exp(-mt_rows),
        jnp.exp(b_last - b_rows + i_rows - m_last),
        jnp.zeros((LANES - 4 * nh, ln), F32)], axis=0)
    cols = per_token.T
    m_ref[...] = jnp.broadcast_to(m_last, m_ref.shape)

    ri = lax.broadcasted_iota(jnp.int32, (ln, ln), 0)
    cj = lax.broadcasted_iota(jnp.int32, (ln, ln), 1)
    causal = cj <= ri
    ones_l = jnp.ones((ln, LANES), BF16)
    ones_v = jnp.ones((DV_C, LANES), BF16)
    two = lambda a: jnp.concatenate([a] * (DV_C // LANES), axis=1)

    for h in range(nh):
        qterm = cols[:, h:h + 1]
        inter_b = jnp.broadcast_to(cols[:, nh + h:nh + h + 1], (ln, LANES))
        emt_b = jnp.broadcast_to(cols[:, 2 * nh + h:2 * nh + h + 1], (ln, LANES))
        w_col = cols[:, 3 * nh + h:3 * nh + h + 1]
        qb = q_ref[:, h * DK_C:(h + 1) * DK_C].astype(BF16)
        kh = k_ref[:, h * DK_C:(h + 1) * DK_C] * (DK_C ** -0.5)
        vx = jnp.concatenate([v_ref[:, h * DV_C:(h + 1) * DV_C].astype(BF16), ones_l], axis=1)

        s = lax.dot_general(qb, kh.astype(BF16), (((1,), (1,)), ((), ())), preferred_element_type=F32)
        amat = (jnp.exp(jnp.where(causal, qterm + r_rows[h:h + 1, :], -jnp.inf)) * s).astype(BF16)
        cx = cx_ref[h]
        qc = jnp.dot(qb, cx.astype(BF16), preferred_element_type=F32)
        av = jnp.dot(amat, vx, preferred_element_type=F32)
        num = two(inter_b) * qc[:, :DV_C] + av[:, :DV_C]
        den_b = inter_b * qc[:, DV_C:] + av[:, DV_C:]
        hh = num * two(1.0 / jnp.maximum(jnp.abs(den_b), emt_b))

        kw = (kh * w_col).astype(BF16)
        cx_ref[h] = decay[h:h + 1, :] * cx + lax.dot_general(kw, vx, (((0,), (0,)), ((), ())),
                                                             preferred_element_type=F32)

        vsl = slice(h * DV_C, (h + 1) * DV_C)
        ssq_b = jnp.dot((hh * hh).astype(BF16), ones_v, preferred_element_type=F32)
        hn = hh * two(lax.rsqrt(ssq_b * (1.0 / DV_C) + EPS)) * gn_ref[:, vsl]
        og = _sigmoid(o_ref[:, vsl] + bo_ref[:, vsl])
        out_ref[:, vsl] = (og * hn).astype(BF16)

    @pl.when(ci == pl.num_programs(1) - 1)
    def _():
        for h in range(nh):
            cx = cx_ref[h]
            c_ref[h] = cx[:, :DV_C]
            n_ref[h:h + 1, :] = cx[:, DV_C:].T[0:1, :]


def mlstm_seq(proj, gates, nbatch, seq_len, gbias, b_o, g_norm):
    ln = min(MLSTM_L, seq_len)
    p3 = proj.reshape(nbatch, seq_len, IN_C_MAIN)
    g3 = gates.reshape(nbatch, seq_len, GATE_PAD)
    in_specs, args = _unzip([
        (pl.BlockSpec((None, ln, QC), lambda b, c: (b, c, 0)), p3),
        (pl.BlockSpec((None, ln, QC), lambda b, c: (b, c, 1)), p3),
        (pl.BlockSpec((None, ln, VC), lambda b, c: (b, c, 2 * QC // VC)), p3),
        (pl.BlockSpec((None, ln, VC), lambda b, c: (b, c, 2 * QC // VC + 1)), p3),
        (pl.BlockSpec((None, ln, GATE_PAD), lambda b, c: (b, c, 0)), g3),
        _layer(gbias, (1, GATE_PAD), lambda b, c: (0, 0)),
        _layer(b_o, (1, VC), lambda b, c: (0, 0)),
        _layer(g_norm, (1, VC), lambda b, c: (0, 0))])
    out, c, n, m = pl.pallas_call(
        _mlstm_seq_kernel,
        grid=(nbatch, seq_len // ln),
        in_specs=in_specs,
        out_specs=[pl.BlockSpec((None, ln, VC), lambda b, c: (b, c, 0)),
                   pl.BlockSpec((None, N_HEADS_C, DK_C, DV_C), lambda b, c: (b, 0, 0, 0)),
                   pl.BlockSpec((None, N_HEADS_C, DK_C), lambda b, c: (b, 0, 0)),
                   pl.BlockSpec((None, N_HEADS_C, LANES), lambda b, c: (b, 0, 0))],
        out_shape=[jax.ShapeDtypeStruct((nbatch, seq_len, VC), BF16),
                   jax.ShapeDtypeStruct((nbatch, N_HEADS_C, DK_C, DV_C), F32),
                   jax.ShapeDtypeStruct((nbatch, N_HEADS_C, DK_C), F32),
                   jax.ShapeDtypeStruct((nbatch, N_HEADS_C, LANES), F32)],
        scratch_shapes=[pltpu.VMEM((N_HEADS_C, DK_C, DV_C + LANES), F32)],
        compiler_params=_cparams(2),
        name="mlstm_seq",
    )(*args)
    return out.reshape(nbatch * seq_len, VC), c, n, m[:, :, 0]


def _mlstm_step_kernel(q_ref, k_ref, v_ref, o_ref, gate_ref, gbias_ref, bo_ref, gn_ref, c0_ref, n0_ref, m0_ref,
                       out_ref, c_ref, n_ref, m_ref):
    q8 = q_ref[...]
    k8 = k_ref[...] * (DK_C ** -0.5)
    v8 = v_ref[...]
    gate = gate_ref[...] + gbias_ref[...]
    i8 = gate[:, 0:1]
    lf8 = _log_sigmoid(gate[:, 1:2])
    g8 = lf8 + m0_ref[...]
    mt = jnp.maximum(g8, i8)
    inter = jnp.exp(g8 - mt)
    wl = jnp.exp(i8 - mt)
    a8 = wl * jnp.sum(q8 * k8, axis=1, keepdims=True)
    row = lax.broadcasted_iota(jnp.int32, (N_HEADS_C, 1), 0)
    qb = q8.astype(BF16)
    kw = k8 * wl
    qc = jnp.zeros((N_HEADS_C, DV_C), F32)
    for h in range(N_HEADS_C):
        qc = qc + jnp.where(row == h, jnp.dot(qb, c0_ref[h].astype(BF16), preferred_element_type=F32), 0.0)
    vb = v8.astype(BF16)
    for h in range(N_HEADS_C):
        kz = jnp.where(row == h, kw, 0.0).astype(BF16)
        c_ref[h] = inter[h:h + 1, :] * c0_ref[h] + lax.dot_general(kz, vb, (((0,), (0,)), ((), ())),
                                                                   preferred_element_type=F32)
    n0 = n0_ref[...]
    num = inter * qc + a8 * v8
    den = inter * jnp.sum(q8 * n0, axis=1, keepdims=True) + a8
    hh = num / jnp.maximum(jnp.abs(den), jnp.exp(-mt))
    n_ref[...] = inter * n0 + kw
    m_ref[...] = mt
    hn = _rms(hh, gn_ref[...])
    og = _sigmoid(o_ref[...] + bo_ref[...])
    out_ref[...] = (og * hn).astype(BF16)


def mlstm_step(proj, gates, gbias, b_o, g_norm, c0, n0, m0):
    n = proj.shape[0]
    h = N_HEADS_C
    q = proj[:, :QC].reshape(n, h, DK_C)
    k = proj[:, QC:2 * QC].reshape(n, h, DK_C)
    v = proj[:, 2 * QC:2 * QC + VC].reshape(n, h, DV_C)
    o = proj[:, 2 * QC + VC:].reshape(n, h, DV_C)
    gcols = jnp.swapaxes(gates[:, :2 * h].reshape(n, 2, h), 1, 2)
    gb = jnp.swapaxes(gbias[0][gbias[1], 0, :2 * h].reshape(2, h), 0, 1)
    nl = c0[0].shape[0]
    per = lambda *tail: pl.BlockSpec((None,) + tail, lambda z: (z,) + (0,) * len(tail))
    lper = lambda o_, *tail: _layer(o_, (None,) + tail, lambda z: (z,) + (0,) * len(tail))
    lshared = lambda o_, *shape: _layer(o_, shape, lambda z: (0,) * len(shape))
    in_specs, args = _unzip([
        (per(h, DK_C), q), (per(h, DK_C), k), (per(h, DV_C), v), (per(h, DV_C), o), (per(h, 2), gcols),
        (pl.BlockSpec((h, 2), lambda z: (0, 0)), gb),
        lshared((b_o[0].reshape(nl, h, DV_C), b_o[1]), h, DV_C),
        lshared((g_norm[0].reshape(nl, h, DV_C), g_norm[1]), h, DV_C),
        lper(c0, h, DK_C, DV_C), lper(n0, h, DK_C), lper((m0[0].reshape(nl, n, h, 1), m0[1]), h, 1)])
    out, c, nn, m = pl.pallas_call(
        _mlstm_step_kernel,
        grid=(n,),
        in_specs=in_specs,
        out_specs=[per(h, DV_C), per(h, DK_C, DV_C), per(h, DK_C), per(h, 1)],
        out_shape=[jax.ShapeDtypeStruct((n, h, DV_C), BF16),
                   jax.ShapeDtypeStruct((n, h, DK_C, DV_C), F32),
                   jax.ShapeDtypeStruct((n, h, DK_C), F32),
                   jax.ShapeDtypeStruct((n, h, 1), F32)],
        compiler_params=_cparams(1),
        name="mlstm_step",
    )(*args)
    return out.reshape(n, VC), c, nn, m.reshape(n, h)


def _prep_weights(w):
    a_in = w['a_w_in']
    q, k, v, xr, gr = jnp.split(a_in, [QA, QA + KA, QA + 2 * KA, QA + 2 * KA + D_RNN], axis=-1)
    nc = w['c_w_in'].shape[0]
    gbias = jnp.concatenate([w['c_b_i'], w['c_b_f'], jnp.zeros((nc, GATE_PAD - 2 * N_HEADS_C), F32)], axis=-1)
    return {
        'a_w_in': jnp.concatenate([q, xr, gr, k, v], axis=-1).astype(BF16),
        'a_w_out': None,
        'c_w_in': None,
        'c_w_gate': None,
        'c_gbias': gbias.reshape(nc, 1, GATE_PAD),
        'c_w_out': None,
        'ffn_w_up': [],
        'ffn_w_down': [],
        'ffn_cp': _ffn_conv_params(w['ffn_conv_w'], w['ffn_conv_b']),
        'lru_wa': jax.vmap(_lru_blockdiag)(w['a_lru_wa']),
        'lru_wx': jax.vmap(_lru_blockdiag)(w['a_lru_wx']),
    }


def _trunk(x, seq_len, state, w, pw):
    m = x.shape[0]
    nbatch = m // seq_len
    outs = {'k': [], 'v': [], 'h': [], 'conv': [], 'C': [], 'n': [], 'm': [], 'f': []}

    def vec(name, idx):
        a = w[name]
        return a.reshape(a.shape[0], 1, a.shape[1]), idx

    for l in range(DEPTH):
        j = l // 2
        if l % 2 == 0:
            proj = norm_matmul(x, vec('norm_mix_pre', l), (pw['a_w_in'], j), IN_A, _proj_tn(m, IN_A))
            lru_args = ((w['a_conv_w'], j), vec('a_conv_b', j), (pw['lru_wa'], j), (pw['lru_wx'], j),
                        vec('a_lru_ba', j), vec('a_lru_bx', j), vec('a_lru_lambda', j))
            if state is None:
                first = ((w['ffn_w_up'], 0, 2 * D_FF, 0), (w['ffn_w_down'], 0, D_MODEL, 0),
                         (w['a_w_out'].reshape(-1, D_MODEL), None, D_MODEL, 0),
                         (w['c_w_out'].reshape(-1, D_MODEL), None, D_MODEL, 0)) if l == 0 else None
                att, k_new, v_new, cast = swa_prompt(proj, w['a_sinks'][j], nbatch, seq_len, cast_f32=first,
                                                     cast_valid=('interleave', None, None, None) if l == 0 else ())
                if cast:
                    pw['ffn_w_up'].append(cast[0])
                    pw['ffn_w_down'].append(cast[1])
                    pw['a_w_out'] = cast[2].reshape(w['a_w_out'].shape)
                    pw['c_w_out'] = cast[3].reshape(w['c_w_out'].shape)
                job = (jnp.swapaxes(w['c_w_in'], 1, 2), IN_C_MAIN) if l == 0 else None
                rec, h_new, c_new, cast = lru_seq(proj, nbatch, seq_len, *lru_args, wt_job=job)
                if cast:
                    pw['c_w_in'], pw['c_w_gate'] = cast
            else:
                att, k_new, v_new = swa_sample(proj, (state['cache_k'], j), (state['cache_v'], j), w['a_sinks'][j])
                rec, h_new, c_new = lru_step(proj, (state['lru_h'], j), (state['lru_conv'], j), *lru_args)
            outs['k'].append(k_new.reshape(nbatch, -1, N_KV_A, HD_A))
            outs['v'].append(v_new.reshape(nbatch, -1, N_KV_A, HD_A))
            outs['h'].append(h_new)
            outs['conv'].append(c_new)
            x = out_proj([att, rec], (pw['a_w_out'], j), x, vec('norm_mix_post', l))
        else:
            proj, gates = norm_matmul(x, vec('norm_mix_pre', l), (pw['c_w_in'], j), IN_C_MAIN,
                                      _proj_tn(m, IN_C_MAIN), w_side=(pw['c_w_gate'], j))
            mix_args = ((pw['c_gbias'], j), vec('c_b_o', j), vec('c_norm', j))
            if state is None:
                hg, c_new, n_new, m_new = mlstm_seq(proj, gates, nbatch, seq_len, *mix_args)
            else:
                hg, c_new, n_new, m_new = mlstm_step(proj, gates, *mix_args, (state['mlstm_C'], j),
                                                     (state['mlstm_n'], j), (state['mlstm_m'], j))
            outs['C'].append(c_new)
            outs['n'].append(n_new)
            outs['m'].append(m_new)
            x = out_proj([hg], (pw['c_w_out'], j), x, vec('norm_mix_post', l))
        if state is None:
            nxt = ((w['ffn_w_up'], l + 1), (w['ffn_w_down'], l + 1)) if l + 1 < DEPTH else None
            x, f_new, cast = ffn_seq(x, seq_len, vec('norm_ffn_pre', l), pw['ffn_w_up'][l], (pw['ffn_cp'], l),
                                     pw['ffn_w_down'][l], vec('norm_ffn_post', l), next_f32=nxt)
            if cast:
                pw['ffn_w_up'].append(cast[0])
                pw['ffn_w_down'].append(cast[1])
        else:
            x, f_new = ffn_step(x, (state['ffn_conv'], l), vec('norm_ffn_pre', l), pw['ffn_w_up'][l],
                                (w['ffn_conv_w'], l), vec('ffn_conv_b', l), pw['ffn_w_down'][l],
                                vec('norm_ffn_post', l))
        outs['f'].append(f_new)
    return x, {name: jnp.stack(vals) for name, vals in outs.items()}


def kernel(x_prompt, x_sample, cache_k, cache_v, state_lru_h, state_lru_conv, state_mlstm_C, state_mlstm_n, state_mlstm_m, state_ffn_conv, norm_mix_pre, norm_mix_post, norm_ffn_pre, norm_ffn_post, a_w_in, a_sinks, a_conv_w, a_conv_b, a_lru_wa, a_lru_ba, a_lru_wx, a_lru_bx, a_lru_lambda, a_w_out, c_w_in, c_b_i, c_b_f, c_b_o, c_norm, c_w_out, ffn_w_up, ffn_conv_w, ffn_conv_b, ffn_w_down):
    w = {
        'norm_mix_pre': norm_mix_pre, 'norm_mix_post': norm_mix_post,
        'norm_ffn_pre': norm_ffn_pre, 'norm_ffn_post': norm_ffn_post,
        'a_w_in': a_w_in, 'a_sinks': a_sinks, 'a_conv_w': a_conv_w, 'a_conv_b': a_conv_b,
        'a_lru_wa': a_lru_wa, 'a_lru_ba': a_lru_ba, 'a_lru_wx': a_lru_wx, 'a_lru_bx': a_lru_bx,
        'a_lru_lambda': a_lru_lambda, 'a_w_out': a_w_out,
        'c_w_in': c_w_in, 'c_b_i': c_b_i, 'c_b_f': c_b_f, 'c_b_o': c_b_o, 'c_norm': c_norm, 'c_w_out': c_w_out,
        'ffn_w_up': ffn_w_up, 'ffn_conv_w': ffn_conv_w, 'ffn_conv_b': ffn_conv_b, 'ffn_w_down': ffn_w_down,
    }
    pw = _prep_weights(w)
    bp, sp, d = x_prompt.shape
    bs, ss, _ = x_sample.shape
    assert ss == 1, "the sample group advances one token per step"
    y_p, st_p = _trunk(x_prompt.reshape(bp * sp, d), sp, None, w, pw)
    state = {'cache_k': cache_k, 'cache_v': cache_v, 'lru_h': state_lru_h, 'lru_conv': state_lru_conv,
             'mlstm_C': state_mlstm_C, 'mlstm_n': state_mlstm_n, 'mlstm_m': state_mlstm_m,
             'ffn_conv': state_ffn_conv}
    y_s, st_s = _trunk(x_sample.reshape(bs * ss, d), 1, state, w, pw)
    return (y_p.reshape(bp, sp, d), y_s.reshape(bs, ss, d),
            st_p['k'], st_s['k'], st_p['v'], st_s['v'], st_p['h'], st_s['h'],
            st_p['conv'], st_s['conv'], st_p['C'], st_s['C'], st_p['n'], st_s['n'],
            st_p['m'], st_s['m'], st_p['f'], st_s['f'])
```

```python
import functools
import math

import jax
import jax.numpy as jnp
from jax import lax
from jax.experimental import pallas as pl
from jax.experimental.pallas import tpu as pltpu

F32 = jnp.float32
BF16 = jnp.bfloat16

D_MODEL = 2048
DEPTH = 4
PAST_LEN = 16384
N_HEADS_A = 16
N_KV_A = 4
HD_A = 64
GROUP_A = N_HEADS_A // N_KV_A
WINDOW = 128
BLOCK_A = 128
ROPE_THETA = 10000.0
D_RNN = 1024
N_BLOCKS_B = 16
BS_B = D_RNN // N_BLOCKS_B
CONV_B = 4
LRU_C = 8.0
N_HEADS_C = 8
DK_C = 128
DV_C = 256
D_FF = 6144
CONV_F = 3
EPS = 1e-6
QA = N_HEADS_A * HD_A
KA = N_KV_A * HD_A
IN_A = QA + 2 * KA + 2 * D_RNN
QC = N_HEADS_C * DK_C
VC = N_HEADS_C * DV_C
IN_C_MAIN = 2 * QC + 2 * VC

LANES = 128
SUBLANES = 8
VMEM_LIMIT_BYTES = 60 * 1024 * 1024

TM = 512
TM_PROJ = 1024
TF = 1024
LRU_TT = 256
MLSTM_L = 256
LRU_GROUP = 256
GATE_PAD = LANES
SAMPLES_PER_STEP = 8


def _cparams(n_axes):
    return pltpu.CompilerParams(dimension_semantics=("arbitrary",) * n_axes,
                                vmem_limit_bytes=VMEM_LIMIT_BYTES)


def _rms(x, g):
    ms = jnp.mean(x * x, axis=-1, keepdims=True)
    return x * lax.rsqrt(ms + EPS) * g


def _gelu_tanh(x):
    c = math.sqrt(2.0 / math.pi)
    hx = 0.5 * x
    return hx + hx * jnp.tanh(x * (c + (c * 0.044715) * (x * x)))


def _sigmoid(x):
    return 0.5 + 0.5 * jnp.tanh(0.5 * x)


def _log_sigmoid(x):
    return jnp.minimum(x, 0.0) - jnp.log1p(jnp.exp(-jnp.abs(x)))


def _norm_matmul_kernel(x_ref, g_ref, w_ref, *rest, with_side):
    o_ref, hn_ref = rest[1 if with_side else 0], rest[-1]

    @pl.when(pl.program_id(1) == 0)
    def _():
        hn_ref[...] = _rms(x_ref[...], g_ref[...]).astype(BF16)
        if with_side:
            rest[2][...] = jnp.dot(hn_ref[...], rest[0][...], preferred_element_type=F32)

    o_ref[...] = jnp.dot(hn_ref[...], w_ref[...], preferred_element_type=F32)


def _proj_tn(m, n):
    del m
    budget = 8 * 1024 * 1024
    best = LANES
    for tn in range(LANES, n + 1, LANES):
        if n % tn == 0 and D_MODEL * tn * 2 <= budget:
            best = tn
    return best


def _layer(opnd, block, imap):
    arr, idx = opnd
    return pl.BlockSpec((None,) + tuple(block), lambda *g: (idx,) + tuple(imap(*g))), arr


def _unzip(pairs):
    return [p[0] for p in pairs], [p[1] for p in pairs]


def _cast_slabs(srcs, dsts, interleave):
    for src, dst, il in zip(srcs, dsts, interleave):
        if il:
            for c in range(D_FF // TF):
                dst[:, 2 * c * TF:(2 * c + 1) * TF] = src[:, c * TF:(c + 1) * TF].astype(BF16)
                dst[:, (2 * c + 1) * TF:(2 * c + 2) * TF] = src[:, D_FF + c * TF:D_FF + (c + 1) * TF].astype(BF16)
        else:
            dst[...] = src[...].astype(BF16)


def _add_cast_job(jobs, nsteps, step_of, in_specs, args, out_specs, out_shape):
    for arr, layer in jobs:
        rows, ncols = arr.shape[-2:]
        rw, rem = divmod(rows, nsteps)
        assert rem == 0 and rw % (2 * SUBLANES) == 0 and ncols % LANES == 0, (rows, nsteps, ncols)
        if layer is None:
            in_specs.append(pl.BlockSpec((rw, ncols), lambda *g: (step_of(*g), 0)))
        else:
            in_specs.append(pl.BlockSpec((None, rw, ncols), lambda *g, _l=layer: (_l, step_of(*g), 0)))
        args.append(arr)
        out_specs.append(pl.BlockSpec((rw, ncols), lambda *g: (step_of(*g), 0)))
        out_shape.append(jax.ShapeDtypeStruct((rows, ncols), BF16))


def norm_matmul(x, g, w, n, tn, w_side=None):
    m, k = x.shape
    assert n % tn == 0 and n <= w[0].shape[-1]
    tm = min(TM_PROJ, m)
    specs, args = _unzip([(pl.BlockSpec((tm, k), lambda i, j: (i, 0)), x),
                          _layer(g, (1, k), lambda i, j: (0, 0)),
                          _layer(w, (k, tn), lambda i, j: (0, j))])
    in_specs = list(specs)
    out_specs = [pl.BlockSpec((tm, tn), lambda i, j: (i, j))]
    out_shape = [jax.ShapeDtypeStruct((m, n), F32)]
    if w_side is not None:
        ns = w_side[0].shape[-1]
        spec, arr = _layer(w_side, (k, ns), lambda i, j: (0, 0))
        in_specs.append(spec)
        out_specs.append(pl.BlockSpec((tm, ns), lambda i, j: (i, 0)))
        out_shape.append(jax.ShapeDtypeStruct((m, ns), F32))
        args.append(arr)
    outs = pl.pallas_call(
        functools.partial(_norm_matmul_kernel, with_side=w_side is not None),
        grid=(m // tm, n // tn),
        in_specs=in_specs,
        out_specs=out_specs,
        out_shape=out_shape,
        scratch_shapes=[pltpu.VMEM((tm, k), BF16)],
        compiler_params=_cparams(2),
        name="norm_matmul",
    )(*args)
    return outs[0] if w_side is None else outs


def _out_proj_kernel(*refs, n_in):
    a_refs = refs[:n_in]
    w_ref, x_ref, g_ref, o_ref = refs[n_in:]
    acc = None
    row = 0
    for a_ref in a_refs:
        ka = a_ref.shape[1]
        part = jnp.dot(a_ref[...], w_ref[row:row + ka, :], preferred_element_type=F32)
        acc = part if acc is None else acc + part
        row += ka
    o_ref[...] = x_ref[...] + _rms(acc, g_ref[...])


def out_proj(inputs, w, x, g):
    m, d = x.shape
    tm = min(TM, m)
    w_spec, w_arr = _layer(w, w[0].shape[1:], lambda i: (0, 0))
    g_spec, g_arr = _layer(g, (1, d), lambda i: (0, 0))
    in_specs = [pl.BlockSpec((tm, a.shape[1]), lambda i: (i, 0)) for a in inputs]
    in_specs += [w_spec, pl.BlockSpec((tm, d), lambda i: (i, 0)), g_spec]
    return pl.pallas_call(
        functools.partial(_out_proj_kernel, n_in=len(inputs)),
        grid=(m // tm,),
        in_specs=in_specs,
        out_specs=pl.BlockSpec((tm, d), lambda i: (i, 0)),
        out_shape=jax.ShapeDtypeStruct((m, d), F32),
        compiler_params=_cparams(1),
        name="out_proj",
    )(*inputs, w_arr, x, g_arr)


def _ffn_conv_params(convw, convb):
    nl = convw.shape[0]
    p = jnp.concatenate([convw, convb[:, None, :], jnp.zeros((nl, SUBLANES - CONV_F - 1, 2 * D_FF), F32)], axis=1)
    return jnp.swapaxes(p.reshape(nl, SUBLANES, 2 * D_FF // TF, TF), 1, 2)


def _ffn_conv(u, prev, cp):
    cw = cp[0:CONV_F, :]
    cb = cp[CONV_F:CONV_F + 1, :]
    y = pltpu.roll(u, 2, axis=0) * cw[0:1, :] + pltpu.roll(u, 1, axis=0) * cw[1:2, :] + u * cw[2:3, :] + cb
    head = jnp.concatenate([prev, u[0:SUBLANES, :]], axis=0)
    y_head = (head[SUBLANES - 2:2 * SUBLANES - 2, :] * cw[0:1, :] + head[SUBLANES - 1:2 * SUBLANES - 1, :] * cw[1:2, :]
              + head[SUBLANES:, :] * cw[2:3, :] + cb)
    return jnp.concatenate([y_head, y[SUBLANES:, :]], axis=0)


def _ffn_seq_kernel(x_ref, gpre_ref, wgv_ref, cp_ref, wd_ref, gpost_ref,
                    *rest, tiles_per_seq, cast_next, n_chunks):
    if cast_next:
        wupn_ref, wdn_ref, o_ref, nu_ref, wupn_out, wdn_out, hn_ref, carry_ref = rest
        wupn_out[...] = wupn_ref[...].astype(BF16)
        wdn_out[...] = wdn_ref[...].astype(BF16)
    else:
        o_ref, nu_ref, hn_ref, carry_ref = rest
    i = pl.program_id(0)
    f = pl.program_id(1)
    nf = n_chunks
    tm = x_ref.shape[0]

    @pl.when(f == 0)
    def _():
        hn_ref[...] = _rms(x_ref[...], gpre_ref[...]).astype(BF16)
        o_ref[...] = jnp.zeros_like(o_ref)

    @pl.when(i % tiles_per_seq == 0)
    def _():
        carry_ref[f] = jnp.zeros(carry_ref.shape[1:], F32)

    u = jnp.dot(hn_ref[...], wgv_ref[...], preferred_element_type=F32)
    tf = u.shape[1] // 2
    ug, uv = u[:, :tf], u[:, tf:]
    cg = _ffn_conv(ug, carry_ref[f, 0], cp_ref[f])
    cv = _ffn_conv(uv, carry_ref[f, 1], cp_ref[f + nf])
    carry_ref[f, 0] = ug[tm - SUBLANES:tm, :]
    carry_ref[f, 1] = uv[tm - SUBLANES:tm, :]
    nu_ref[0] = ug[tm - (CONV_F - 1):tm, :]
    nu_ref[1] = uv[tm - (CONV_F - 1):tm, :]
    act = (_gelu_tanh(cg) * cv).astype(BF16)
    o_ref[...] += jnp.dot(act, wd_ref[...], preferred_element_type=F32)

    @pl.when(f == nf - 1)
    def _():
        o_ref[...] = x_ref[...] + _rms(o_ref[...], gpost_ref[...])


def ffn_seq(x, seq_len, gpre, wup, cp, wdown, gpost, next_f32=None):
    m, d = x.shape
    tm = min(TM, seq_len)
    tps = seq_len // tm
    nb = m // seq_len
    nf = D_FF // TF
    nsteps = (m // tm) * nf
    cast_next = next_f32 is not None
    kern = functools.partial(_ffn_seq_kernel, tiles_per_seq=tps, cast_next=cast_next, n_chunks=nf)
    in_specs, args = _unzip([
        (pl.BlockSpec((tm, d), lambda i, f: (i, 0)), x),
        _layer(gpre, (1, d), lambda i, f: (0, 0)),
        (pl.BlockSpec((d, 2 * TF), lambda i, f: (0, f)), wup),
        _layer(cp, cp[0].shape[1:], lambda i, f: (0, 0, 0)),
        (pl.BlockSpec((TF, d), lambda i, f: (f, 0)), wdown),
        _layer(gpost, (1, d), lambda i, f: (0, 0))])
    out_specs = [pl.BlockSpec((tm, d), lambda i, f: (i, 0)),
                 pl.BlockSpec((None, 2, CONV_F - 1, TF), lambda i, f: (i, 0, 0, f))]
    out_shape = [jax.ShapeDtypeStruct((m, d), F32),
                 jax.ShapeDtypeStruct((m // tm, 2, CONV_F - 1, D_FF), F32)]
    if cast_next:
        cw, rem_c = divmod(2 * D_FF, nsteps)
        rw, rem_r = divmod(D_FF, nsteps)
        assert rem_c == 0 and rem_r == 0 and cw % LANES == 0 and rw % (2 * SUBLANES) == 0, (nsteps, cw, rw)
        up_in, up_arr = _layer(next_f32[0], (d, cw), lambda i, f: (0, i * nf + f))
        dn_in, dn_arr = _layer(next_f32[1], (rw, d), lambda i, f: (i * nf + f, 0))
        in_specs += [up_in, dn_in]
        args += [up_arr, dn_arr]
        per, half = TF // cw, D_FF // cw
        assert TF % cw == 0

        def up_dst(i, f):
            s = i * nf + f
            return 0, ((s % half) // per) * (2 * per) + (s // half) * per + s % per

        out_specs += [pl.BlockSpec((d, cw), up_dst),
                      pl.BlockSpec((rw, d), lambda i, f: (i * nf + f, 0))]
        out_shape += [jax.ShapeDtypeStruct((d, 2 * D_FF), BF16), jax.ShapeDtypeStruct((D_FF, d), BF16)]
    outs = pl.pallas_call(
        kern,
        grid=(m // tm, nf),
        in_specs=in_specs,
        out_specs=out_specs,
        out_shape=out_shape,
        scratch_shapes=[pltpu.VMEM((tm, d), BF16),
                        pltpu.VMEM((nf, 2, SUBLANES, TF), F32)],
        compiler_params=_cparams(2),
        name="ffn_seq",
    )(*args)
    y, nu = outs[:2]
    last = nu.reshape(nb, tps, 2, CONV_F - 1, D_FF)[:, tps - 1]
    last = jnp.swapaxes(last, 1, 2).reshape(nb, CONV_F - 1, 2 * D_FF)
    return y, last, tuple(outs[2:])


def _ffn_step_kernel(x_ref, gpre_ref, wg_ref, wv_ref, cwg_ref, cwv_ref, cbg_ref, cbv_ref, wd_ref, gpost_ref,
                     pg_ref, pv_ref,
                     o_ref, ng_ref, nv_ref, hn_ref, acc_ref):
    f = pl.program_id(0)
    nf = pl.num_programs(0)

    @pl.when(f == 0)
    def _():
        hn_ref[...] = _rms(x_ref[...], gpre_ref[...]).astype(BF16)
        acc_ref[...] = jnp.zeros_like(acc_ref)

    hn = hn_ref[...]
    ug = jnp.dot(hn, wg_ref[...], preferred_element_type=F32)
    uv = jnp.dot(hn, wv_ref[...], preferred_element_type=F32)
    p0g, p1g = pg_ref[:, 0, :], pg_ref[:, 1, :]
    p0v, p1v = pv_ref[:, 0, :], pv_ref[:, 1, :]
    ng_ref[:, 0, :] = p1g
    ng_ref[:, 1, :] = ug
    nv_ref[:, 0, :] = p1v
    nv_ref[:, 1, :] = uv
    cwg = cwg_ref[...]
    cwv = cwv_ref[...]
    cg = p0g * cwg[0:1, :] + p1g * cwg[1:2, :] + ug * cwg[2:3, :] + cbg_ref[...]
    cv = p0v * cwv[0:1, :] + p1v * cwv[1:2, :] + uv * cwv[2:3, :] + cbv_ref[...]
    act = (_gelu_tanh(cg) * cv).astype(BF16)
    acc_ref[...] += jnp.dot(act, wd_ref[...], preferred_element_type=F32)

    @pl.when(f == nf - 1)
    def _():
        o_ref[...] = x_ref[...] + _rms(acc_ref[...], gpost_ref[...])


def ffn_step(x, past, gpre, wup, convw, convb, wdown, gpost):
    n, d = x.shape
    nf = D_FF // TF
    phalf = lambda c: _layer(past, (n, CONV_F - 1, TF), lambda f: (0, 0, f + c * nf))
    in_specs, args = _unzip([
        (pl.BlockSpec((n, d), lambda f: (0, 0)), x),
        _layer(gpre, (1, d), lambda f: (0, 0)),
        (pl.BlockSpec((d, TF), lambda f: (0, 2 * f)), wup),
        (pl.BlockSpec((d, TF), lambda f: (0, 2 * f + 1)), wup),
        _layer(convw, (CONV_F, TF), lambda f: (0, f)),
        _layer(convw, (CONV_F, TF), lambda f: (0, f + nf)),
        _layer(convb, (1, TF), lambda f: (0, f)),
        _layer(convb, (1, TF), lambda f: (0, f + nf)),
        (pl.BlockSpec((TF, d), lambda f: (f, 0)), wdown),
        _layer(gpost, (1, d), lambda f: (0, 0)),
        phalf(0), phalf(1)])
    y, ng, nv = pl.pallas_call(
        _ffn_step_kernel,
        grid=(nf,),
        in_specs=in_specs,
        out_specs=[pl.BlockSpec((n, d), lambda f: (0, 0)),
                   pl.BlockSpec((n, CONV_F - 1, TF), lambda f: (0, 0, f)),
                   pl.BlockSpec((n, CONV_F - 1, TF), lambda f: (0, 0, f))],
        out_shape=[jax.ShapeDtypeStruct((n, d), F32),
                   jax.ShapeDtypeStruct((n, CONV_F - 1, D_FF), F32),
                   jax.ShapeDtypeStruct((n, CONV_F - 1, D_FF), F32)],
        scratch_shapes=[pltpu.VMEM((n, d), BF16), pltpu.VMEM((n, d), F32)],
        compiler_params=_cparams(1),
        name="ffn_step",
    )(*args)
    return y, jnp.concatenate([ng, nv], axis=-1)


def _rope_tables(pos0, t):
    half = HD_A // 2
    pos = (pos0 + jnp.arange(t)).astype(F32)
    inv = ROPE_THETA ** (-jnp.arange(half, dtype=F32) / half)
    ang = pos[:, None] * inv[None, :]
    cos, sin = jnp.cos(ang), jnp.sin(ang)
    zero = jnp.zeros_like(sin)
    cos_h = jnp.concatenate([cos, cos], axis=-1)
    sin_up = jnp.concatenate([-sin, zero], axis=-1)
    sin_dn = jnp.concatenate([zero, sin], axis=-1)
    two = lambda a: jnp.concatenate([a, a], axis=-1)
    return two(cos_h), two(sin_up), two(sin_dn)


def _rope(x, cos, sin_up, sin_dn):
    w = x.shape[1]
    reps = w // LANES
    tile = lambda a: jnp.concatenate([a] * reps, axis=1) if reps > 1 else a
    x_up = pltpu.roll(x, w - HD_A // 2, axis=1)
    x_dn = pltpu.roll(x, HD_A // 2, axis=1)
    return x * tile(cos) + x_up * tile(sin_up) + x_dn * tile(sin_dn)


def _dup_half(a, half):
    lane = lax.broadcasted_iota(jnp.int32, a.shape, 1)
    sw = pltpu.roll(a, HD_A, axis=1)
    lo = lane < HD_A
    return jnp.where(lo, a, sw) if half == 0 else jnp.where(lo, sw, a)


def _sink_softmax(s, sk_rows):
    m = jnp.maximum(jnp.max(s, axis=-1, keepdims=True), sk_rows)
    p = jnp.exp(s - m)
    return p, jnp.sum(p, axis=-1, keepdims=True) + jnp.exp(sk_rows - m)


def _swa_prompt_kernel(sinks_ref, q_ref, kc_ref, kp_ref, vc_ref, vp_ref,
                       cc_ref, suc_ref, sdc_ref, cp_ref, sup_ref, sdp_ref, *rest, cast_interleave):
    nc = len(cast_interleave)
    att_ref, knew_ref, vnew_ref = rest[nc:nc + 3]
    _cast_slabs(rest[:nc], rest[nc + 3:], cast_interleave)
    c = pl.program_id(1)
    nb = pl.num_programs(1)
    bq = q_ref.shape[0]
    q = _rope(q_ref[...], cc_ref[...], suc_ref[...], sdc_ref[...])
    kc = _rope(kc_ref[...], cc_ref[...], suc_ref[...], sdc_ref[...])
    kp = _rope(kp_ref[...], cp_ref[...], sup_ref[...], sdp_ref[...])
    vc = vc_ref[...]
    vp = vp_ref[...]

    @pl.when(c == nb - 1)
    def _():
        knew_ref[...] = kc
        vnew_ref[...] = vc

    rows = GROUP_A * bq
    r = lax.broadcasted_iota(jnp.int32, (rows, 2 * bq), 0) % bq
    col = lax.broadcasted_iota(jnp.int32, (rows, 2 * bq), 1)
    valid = (col >= r) & (col <= r + WINDOW) & ((c > 0) | (col >= bq))
    lane = lax.broadcasted_iota(jnp.int32, (bq, LANES), 1)
    lo = lane < HD_A
    row_head = lax.broadcasted_iota(jnp.int32, (rows, 1), 0) // bq

    for h in range(N_KV_A):
        t, half = divmod(h, 2)
        ksl = slice(t * LANES, (t + 1) * LANES)
        kd = _dup_half(jnp.concatenate([kp[:, ksl], kc[:, ksl]], axis=0), half).astype(BF16)
        vd = _dup_half(jnp.concatenate([vp[:, ksl], vc[:, ksl]], axis=0), half).astype(BF16)
        parts = []
        for p in range(2):
            q2 = q[:, (2 * h + p) * LANES:(2 * h + p + 1) * LANES]
            parts.append(jnp.where(lo, q2, 0.0))
            parts.append(jnp.where(lo, 0.0, q2))
        qs = (jnp.concatenate(parts, axis=0) * (HD_A ** -0.5)).astype(BF16)
        s = lax.dot_general(qs, kd, (((1,), (1,)), ((), ())), preferred_element_type=F32)
        s = jnp.where(valid, s, -jnp.inf)
        sk = jnp.zeros((rows, 1), F32)
        for g in range(GROUP_A):
            sk = jnp.where(row_head == g, sinks_ref[h * GROUP_A + g], sk)
        pr, den = _sink_softmax(s, sk)
        o = jnp.dot(pr.astype(BF16), vd, preferred_element_type=F32) / den
        for p in range(2):
            o_lo = o[(2 * p) * bq:(2 * p + 1) * bq, :]
            o_hi = o[(2 * p + 1) * bq:(2 * p + 2) * bq, :]
            att_ref[:, (2 * h + p) * LANES:(2 * h + p + 1) * LANES] = jnp.where(lo, o_lo, o_hi).astype(BF16)


def swa_prompt(proj, sinks, nbatch, seq_len, cast_f32=(), cast_interleave=()):
    nb = seq_len // BLOCK_A
    p3 = proj.reshape(nbatch, seq_len, IN_A)
    kcol = (QA + 2 * D_RNN) // KA
    cos, sup, sdn = _rope_tables(0, seq_len)
    prev = lambda c: jnp.maximum(c - 1, 0)
    tab_c = pl.BlockSpec((BLOCK_A, LANES), lambda b, c: (c, 0))
    tab_p = pl.BlockSpec((BLOCK_A, LANES), lambda b, c: (prev(c), 0))
    in_specs = [pl.BlockSpec(memory_space=pltpu.SMEM),
                pl.BlockSpec((None, BLOCK_A, QA), lambda b, c: (b, c, 0)),
                pl.BlockSpec((None, BLOCK_A, KA), lambda b, c: (b, c, kcol)),
                pl.BlockSpec((None, BLOCK_A, KA), lambda b, c: (b, prev(c), kcol)),
                pl.BlockSpec((None, BLOCK_A, KA), lambda b, c: (b, c, kcol + 1)),
                pl.BlockSpec((None, BLOCK_A, KA), lambda b, c: (b, prev(c), kcol + 1)),
                tab_c, tab_c, tab_c, tab_p, tab_p, tab_p]
    args = [sinks, p3, p3, p3, p3, p3, cos, sup, sdn, cos, sup, sdn]
    out_specs = [pl.BlockSpec((None, BLOCK_A, QA), lambda b, c: (b, c, 0)),
                 pl.BlockSpec((None, BLOCK_A, KA), lambda b, c: (b, 0, 0)),
                 pl.BlockSpec((None, BLOCK_A, KA), lambda b, c: (b, 0, 0))]
    out_shape = [jax.ShapeDtypeStruct((nbatch, seq_len, QA), BF16),
                 jax.ShapeDtypeStruct((nbatch, BLOCK_A, KA), F32),
                 jax.ShapeDtypeStruct((nbatch, BLOCK_A, KA), F32)]
    assert len(cast_f32) == len(cast_interleave)
    _add_cast_job(cast_f32, nbatch * nb, lambda b, c: b * nb + c, in_specs, args, out_specs, out_shape)
    outs = pl.pallas_call(
        functools.partial(_swa_prompt_kernel, cast_interleave=tuple(cast_interleave)),
        grid=(nbatch, nb),
        in_specs=in_specs,
        out_specs=out_specs,
        out_shape=out_shape,
        compiler_params=_cparams(2),
        name="swa_prompt",
    )(*args)
    att, knew, vnew = outs[:3]
    return att.reshape(nbatch * seq_len, QA), knew, vnew, tuple(outs[3:])


def _swa_sample_kernel(sinks_ref, q_ref, k_ref, v_ref, kc_ref, vc_ref, cos_ref, sup_ref, sdn_ref,
                       att_ref, knew_ref, vnew_ref):
    w = kc_ref.shape[1]
    lane = lax.broadcasted_iota(jnp.int32, (1, LANES), 1)
    lo = lane < HD_A
    row = lax.broadcasted_iota(jnp.int32, (SUBLANES, 1), 0)
    scale = HD_A ** -0.5
    todo = []
    for z in range(q_ref.shape[0]):
        q = _rope(q_ref[z], cos_ref[...], sup_ref[...], sdn_ref[...])
        k = _rope(k_ref[z], cos_ref[...], sup_ref[...], sdn_ref[...])
        v = v_ref[z]
        knew_ref[z, 0:w - 1, :] = kc_ref[z, 1:w, :]
        knew_ref[z, w - 1:w, :] = k
        vnew_ref[z, 0:w - 1, :] = vc_ref[z, 1:w, :]
        vnew_ref[z, w - 1:w, :] = v
        for h in range(N_KV_A):
            t, half = divmod(h, 2)
            ksl = slice(t * LANES, (t + 1) * LANES)
            kd = _dup_half(kc_ref[z, :, ksl], half)
            kd_new = _dup_half(k[:, ksl], half)
            parts = []
            for p in range(2):
                q2 = q[:, (2 * h + p) * LANES:(2 * h + p + 1) * LANES]
                parts.append(jnp.where(lo, q2, 0.0))
                parts.append(jnp.where(lo, 0.0, q2))
            parts.append(jnp.zeros((SUBLANES - GROUP_A, LANES), F32))
            qs = jnp.concatenate(parts, axis=0)
            s = lax.dot_general(qs.astype(BF16), kd.astype(BF16), (((1,), (1,)), ((), ())),
                                preferred_element_type=F32) * scale
            s_new = jnp.sum(qs * kd_new, axis=-1, keepdims=True) * scale
            todo.append((z, h, s, s_new, v))
    probs = []
    for z, h, s, s_new, v in todo:
        sk = jnp.zeros((SUBLANES, 1), F32)
        for g in range(GROUP_A):
            sk = jnp.where(row == g, sinks_ref[h * GROUP_A + g], sk)
        m = jnp.maximum(jnp.maximum(jnp.max(s, axis=-1, keepdims=True), s_new), sk)
        pr = jnp.exp(s - m)
        p_new = jnp.exp(s_new - m)
        den = jnp.sum(pr, axis=-1, keepdims=True) + p_new + jnp.exp(sk - m)
        probs.append((z, h, (pr / den).astype(BF16), p_new / den, v))
    for z, h, pr, p_new, v in probs:
        t, half = divmod(h, 2)
        ksl = slice(t * LANES, (t + 1) * LANES)
        vd = _dup_half(vc_ref[z, :, ksl], half)
        vd_new = _dup_half(v[:, ksl], half)
        o = jnp.dot(pr, vd.astype(BF16), preferred_element_type=F32) + p_new * vd_new
        for p in range(2):
            att_ref[z, :, (2 * h + p) * LANES:(2 * h + p + 1) * LANES] = jnp.where(
                lo, o[2 * p:2 * p + 1, :], o[2 * p + 1:2 * p + 2, :]).astype(BF16)


def swa_sample(proj, cache_k, cache_v, sinks):
    n = proj.shape[0]
    nl, _, w = cache_k[0].shape[:3]
    p3 = proj.reshape(n, 1, IN_A)
    kcol = (QA + 2 * D_RNN) // KA
    ck_spec, ck = _layer((cache_k[0].reshape(nl, n, w, KA), cache_k[1]), (SAMPLES_PER_STEP, w, KA),
                         lambda z: (z, 0, 0))
    cv_spec, cv = _layer((cache_v[0].reshape(nl, n, w, KA), cache_v[1]), (SAMPLES_PER_STEP, w, KA),
                         lambda z: (z, 0, 0))
    cos, sup, sdn = _rope_tables(PAST_LEN, 1)
    tab = pl.BlockSpec((1, LANES), lambda z: (0, 0))
    gs = SAMPLES_PER_STEP
    assert n % gs == 0
    att, knew, vnew = pl.pallas_call(
        _swa_sample_kernel,
        grid=(n // gs,),
        in_specs=[pl.BlockSpec(memory_space=pltpu.SMEM),
                  pl.BlockSpec((gs, 1, QA), lambda z: (z, 0, 0)),
                  pl.BlockSpec((gs, 1, KA), lambda z: (z, 0, kcol)),
                  pl.BlockSpec((gs, 1, KA), lambda z: (z, 0, kcol + 1)),
                  ck_spec, cv_spec,
                  tab, tab, tab],
        out_specs=[pl.BlockSpec((gs, 1, QA), lambda z: (z, 0, 0)),
                   pl.BlockSpec((gs, w, KA), lambda z: (z, 0, 0)),
                   pl.BlockSpec((gs, w, KA), lambda z: (z, 0, 0))],
        out_shape=[jax.ShapeDtypeStruct((n, 1, QA), BF16),
                   jax.ShapeDtypeStruct((n, w, KA), F32),
                   jax.ShapeDtypeStruct((n, w, KA), F32)],
        compiler_params=_cparams(1),
        name="swa_sample",
    )(sinks, p3, p3, p3, ck, cv, cos, sup, sdn)
    return att.reshape(n, QA), knew, vnew


def _lru_gates(xc, wa_ref, wx_ref, ba_ref, bx_ref, lam_ref):
    rs, is_ = [], []
    for c in range(D_RNN // LRU_GROUP):
        xg = xc[:, c * LRU_GROUP:(c + 1) * LRU_GROUP].astype(BF16)
        rs.append(jnp.dot(xg, wa_ref[c], preferred_element_type=F32))
        is_.append(jnp.dot(xg, wx_ref[c], preferred_element_type=F32))
    r = _sigmoid(jnp.concatenate(rs, axis=1) + ba_ref[...])
    i = _sigmoid(jnp.concatenate(is_, axis=1) + bx_ref[...])
    log_a = LRU_C * r * _log_sigmoid(lam_ref[...])
    a = jnp.exp(log_a)
    s = -jnp.tanh(log_a) * (a * a + 1.0)
    b = jnp.where(s > 0.0, s * lax.rsqrt(s), 0.0) * (i * xc)
    return a, b


def _lru_seq_kernel(xr_ref, gr_ref, cw_ref, cb_ref, wa_ref, wx_ref, ba_ref, bx_ref, lam_ref, *rest, side_job):
    if side_job:
        wt_ref, gt_ref, rec_ref, hlast_ref, cnew_ref, w_out, g_out, xbuf_ref, a_ref, b_ref, h_ref, hc_ref = rest
        w_out[...] = wt_ref[...].T.astype(BF16)
        gt = gt_ref[...]
        pad = jnp.zeros((GATE_PAD - gt.shape[0], gt.shape[1]), F32)
        g_out[...] = jnp.concatenate([gt, pad], axis=0).T.astype(BF16)
    else:
        rec_ref, hlast_ref, cnew_ref, xbuf_ref, a_ref, b_ref, h_ref, hc_ref = rest
    t = pl.program_id(1)
    tt = xr_ref.shape[0]

    @pl.when(t == 0)
    def _():
        xbuf_ref[0:SUBLANES, :] = jnp.zeros((SUBLANES, D_RNN), F32)
        hc_ref[...] = jnp.zeros_like(hc_ref)

    xr = xr_ref[...]
    xbuf_ref[SUBLANES:SUBLANES + tt, :] = xr
    cw = cw_ref[...]
    xc = xr * cw[CONV_B - 1:CONV_B, :] + cb_ref[...]
    for j in range(CONV_B - 1):
        off = SUBLANES - (CONV_B - 1) + j
        xc = xc + xbuf_ref[off:off + tt, :] * cw[j:j + 1, :]
    xbuf_ref[0:SUBLANES, :] = xr[tt - SUBLANES:tt, :]
    cnew_ref[...] = xr[tt - (CONV_B - 1):tt, :]

    a, b = _lru_gates(xc, wa_ref, wx_ref, ba_ref, bx_ref, lam_ref)
    a_ref[...] = a
    b_ref[...] = b
    row = lax.broadcasted_iota(jnp.int32, (SUBLANES, D_RNN), 0)

    def body(g, h):
        base = pl.multiple_of(g * SUBLANES, SUBLANES)
        a8 = a_ref[pl.ds(base, SUBLANES), :]
        b8 = b_ref[pl.ds(base, SUBLANES), :]
        for s in (1, 2, 4):
            a_sh = pltpu.roll(a8, s, axis=0)
            b_sh = pltpu.roll(b8, s, axis=0)
            keep = row >= s
            b8 = jnp.where(keep, a8 * b_sh + b8, b8)
            a8 = jnp.where(keep, a8 * a_sh, a8)
        h8 = b8 + a8 * h
        h_ref[pl.ds(base, SUBLANES), :] = h8
        return h8[SUBLANES - 1:SUBLANES, :]

    h_last = lax.fori_loop(0, tt // SUBLANES, body, hc_ref[...])
    hc_ref[...] = h_last
    hlast_ref[...] = h_last
    rec_ref[...] = (h_ref[...] * _gelu_tanh(gr_ref[...])).astype(BF16)


def _lru_blockdiag(w):
    per = LRU_GROUP // BS_B
    w4 = w.reshape(D_RNN // LRU_GROUP, per, BS_B, BS_B)
    eye = jnp.eye(per, dtype=w.dtype)
    return jnp.einsum('cpij,pq->cpiqj', w4, eye).reshape(D_RNN // LRU_GROUP, LRU_GROUP, LRU_GROUP).astype(BF16)


def lru_seq(proj, nbatch, seq_len, cw, cb, wa_bd, wx_bd, ba, bx, lam, wt_job=None):
    p3 = proj.reshape(nbatch, seq_len, IN_A)
    tt = min(LRU_TT, seq_len)
    nt = seq_len // tt
    vec = lambda o: _layer(o, (1, D_RNN), lambda b, t: (0, 0))
    wspec = lambda o: _layer(o, o[0].shape[1:], lambda b, t: (0, 0, 0))
    in_specs, args = _unzip([
        (pl.BlockSpec((None, tt, D_RNN), lambda b, t: (b, t, QA // D_RNN)), p3),
        (pl.BlockSpec((None, tt, D_RNN), lambda b, t: (b, t, QA // D_RNN + 1)), p3),
        _layer(cw, (CONV_B, D_RNN), lambda b, t: (0, 0)),
        vec(cb), wspec(wa_bd), wspec(wx_bd), vec(ba), vec(bx), vec(lam)])
    out_specs = [pl.BlockSpec((None, tt, D_RNN), lambda b, t: (b, t, 0)),
                 pl.BlockSpec((None, 1, D_RNN), lambda b, t: (b, 0, 0)),
                 pl.BlockSpec((None, CONV_B - 1, D_RNN), lambda b, t: (b, 0, 0))]
    out_shape = [jax.ShapeDtypeStruct((nbatch, seq_len, D_RNN), BF16),
                 jax.ShapeDtypeStruct((nbatch, 1, D_RNN), F32),
                 jax.ShapeDtypeStruct((nbatch, CONV_B - 1, D_RNN), F32)]
    if wt_job is not None:
        wt, n_main = wt_job
        nl, n_all, kin = wt.shape
        spl, rem = divmod(nbatch * nt, nl)
        rp, rem2 = divmod(n_main, spl)
        ng = n_all - n_main
        assert rem == 0 and rem2 == 0 and rp % LANES == 0 and n_main % ng == 0 and ng % SUBLANES == 0
        step = lambda b, t: b * nt + t
        in_specs += [pl.BlockSpec((None, rp, kin), lambda b, t: (step(b, t) // spl, step(b, t) % spl, 0)),
                     pl.BlockSpec((None, ng, kin), lambda b, t: (step(b, t) // spl, n_main // ng, 0))]
        args += [wt, wt]
        out_specs += [pl.BlockSpec((None, kin, rp), lambda b, t: (step(b, t) // spl, 0, step(b, t) % spl)),
                      pl.BlockSpec((None, kin, GATE_PAD), lambda b, t: (step(b, t) // spl, 0, 0))]
        out_shape += [jax.ShapeDtypeStruct((nl, kin, n_main), BF16), jax.ShapeDtypeStruct((nl, kin, GATE_PAD), BF16)]
    outs = pl.pallas_call(
        functools.partial(_lru_seq_kernel, side_job=wt_job is not None),
        grid=(nbatch, nt),
        in_specs=in_specs,
        out_specs=out_specs,
        out_shape=out_shape,
        scratch_shapes=[pltpu.VMEM((tt + SUBLANES, D_RNN), F32),
                        pltpu.VMEM((tt, D_RNN), F32),
                        pltpu.VMEM((tt, D_RNN), F32),
                        pltpu.VMEM((tt, D_RNN), F32),
                        pltpu.VMEM((1, D_RNN), F32)],
        compiler_params=_cparams(2),
        name="lru_seq",
    )(*args)
    rec, hlast, cnew = outs[:3]
    return rec.reshape(nbatch * seq_len, D_RNN), hlast.reshape(nbatch, D_RNN), cnew, tuple(outs[3:])


def _lru_step_kernel(xr_ref, gr_ref, past_ref, h0_ref, cw_ref, cb_ref,
                     wa_ref, wx_ref, ba_ref, bx_ref, lam_ref, rec_ref, h_ref, cnew_ref):
    cw = cw_ref[...]
    xr = xr_ref[...]
    xc = xr * cw[CONV_B - 1:CONV_B, :] + cb_ref[...]
    for r in range(CONV_B - 1):
        p = past_ref[:, r, :]
        xc = xc + p * cw[r:r + 1, :]
        if r > 0:
            cnew_ref[:, r - 1, :] = p
    cnew_ref[:, CONV_B - 2, :] = xr
    a, b = _lru_gates(xc, wa_ref, wx_ref, ba_ref, bx_ref, lam_ref)
    h = b + a * h0_ref[...]
    h_ref[...] = h
    rec_ref[...] = (h * _gelu_tanh(gr_ref[...])).astype(BF16)


def lru_step(proj, h0, conv0, cw, cb, wa_bd, wx_bd, ba, bx, lam):
    n = proj.shape[0]
    col = lambda j: pl.BlockSpec((n, D_RNN), lambda i: (0, j))
    vec = lambda o: _layer(o, (1, D_RNN), lambda i: (0, 0))
    wspec = lambda o: _layer(o, o[0].shape[1:], lambda i: (0, 0, 0))
    hist = pl.BlockSpec((n, CONV_B - 1, D_RNN), lambda i: (0, 0, 0))
    in_specs, args = _unzip([
        (col(QA // D_RNN), proj), (col(QA // D_RNN + 1), proj),
        _layer(conv0, (n, CONV_B - 1, D_RNN), lambda i: (0, 0, 0)),
        _layer(h0, (n, D_RNN), lambda i: (0, 0)),
        _layer(cw, (CONV_B, D_RNN), lambda i: (0, 0)),
        vec(cb), wspec(wa_bd), wspec(wx_bd), vec(ba), vec(bx), vec(lam)])
    rec, h, conv_new = pl.pallas_call(
        _lru_step_kernel,
        grid=(1,),
        in_specs=in_specs,
        out_specs=[col(0), col(0), hist],
        out_shape=[jax.ShapeDtypeStruct((n, D_RNN), BF16), jax.ShapeDtypeStruct((n, D_RNN), F32),
                   jax.ShapeDtypeStruct((n, CONV_B - 1, D_RNN), F32)],
        compiler_params=_cparams(1),
        name="lru_step",
    )(*args)
    return rec, h, conv_new


def _cumsum_lanes(x):
    n = x.shape[1]
    lane = lax.broadcasted_iota(jnp.int32, x.shape, 1)
    s = 1
    while s < n:
        x = x + jnp.where(lane >= s, pltpu.roll(x, s, axis=1), 0.0)
        s *= 2
    return x


def _cummax_lanes(x):
    n = x.shape[1]
    lane = lax.broadcasted_iota(jnp.int32, x.shape, 1)
    s = 1
    while s < n:
        x = jnp.maximum(x, jnp.where(lane >= s, pltpu.roll(x, s, axis=1), -jnp.inf))
        s *= 2
    return x


def _mlstm_seq_kernel(q_ref, k_ref, v_ref, o_ref, gate_ref, gbias_ref, bo_ref, gn_ref,
                      out_ref, c_ref, n_ref, m_ref, cx_ref):
    ci = pl.program_id(1)
    nh = N_HEADS_C
    ln = q_ref.shape[0]

    @pl.when(ci == 0)
    def _():
        cx_ref[...] = jnp.zeros_like(cx_ref)
        m_ref[...] = jnp.zeros_like(m_ref)

    gate_t = (gate_ref[...] + gbias_ref[...]).T
    i_rows = gate_t[0:nh, :]
    b_rows = _cumsum_lanes(_log_sigmoid(gate_t[nh:2 * nh, :]))
    m_prev = m_ref[:, 0:1]
    r_rows = i_rows - b_rows
    mt_rows = b_rows + jnp.maximum(m_prev, _cummax_lanes(r_rows))
    m_last = mt_rows[:, ln - 1:ln]
    b_last = b_rows[:, ln - 1:ln]
    decay = jnp.exp(b_last + m_prev - m_last)
    per_token = jnp.concatenate([
        b_rows - mt_rows,
        jnp.exp(b_rows + m_prev - mt_rows),
        jnp.exp(-mt_rows),
        jnp.exp(b_last - b_rows + i_rows - m_last),
        jnp.zeros((LANES - 4 * nh, ln), F32)], axis=0)
    cols = per_token.T
    m_ref[...] = jnp.broadcast_to(m_last, m_ref.shape)

    ri = lax.broadcasted_iota(jnp.int32, (ln, ln), 0)
    cj = lax.broadcasted_iota(jnp.int32, (ln, ln), 1)
    causal = cj <= ri
    ones_l = jnp.ones((ln, LANES), BF16)
    ones_v = jnp.ones((DV_C, LANES), BF16)
    two = lambda a: jnp.concatenate([a] * (DV_C // LANES), axis=1)

    for h in range(nh):
        qterm = cols[:, h:h + 1]
        inter_b = jnp.broadcast_to(cols[:, nh + h:nh + h + 1], (ln, LANES))
        emt_b = jnp.broadcast_to(cols[:, 2 * nh + h:2 * nh + h + 1], (ln, LANES))
        w_col = cols[:, 3 * nh + h:3 * nh + h + 1]
        qb = q_ref[:, h * DK_C:(h + 1) * DK_C].astype(BF16)
        kh = k_ref[:, h * DK_C:(h + 1) * DK_C] * (DK_C ** -0.5)
        vx = jnp.concatenate([v_ref[:, h * DV_C:(h + 1) * DV_C].astype(BF16), ones_l], axis=1)

        s = lax.dot_general(qb, kh.astype(BF16), (((1,), (1,)), ((), ())), preferred_element_type=F32)
        amat = (jnp.exp(jnp.where(causal, qterm + r_rows[h:h + 1, :], -jnp.inf)) * s).astype(BF16)
        cx = cx_ref[h]
        qc = jnp.dot(qb, cx.astype(BF16), preferred_element_type=F32)
        av = jnp.dot(amat, vx, preferred_element_type=F32)
        num = two(inter_b) * qc[:, :DV_C] + av[:, :DV_C]
        den_b = inter_b * qc[:, DV_C:] + av[:, DV_C:]
        hh = num * two(1.0 / jnp.maximum(jnp.abs(den_b), emt_b))

        kw = (kh * w_col).astype(BF16)
        cx_ref[h] = decay[h:h + 1, :] * cx + lax.dot_general(kw, vx, (((0,), (0,)), ((), ())),
                                                             preferred_element_type=F32)

        vsl = slice(h * DV_C, (h + 1) * DV_C)
        ssq_b = jnp.dot((hh * hh).astype(BF16), ones_v, preferred_element_type=F32)
        hn = hh * two(lax.rsqrt(ssq_b * (1.0 / DV_C) + EPS)) * gn_ref[:, vsl]
        og = _sigmoid(o_ref[:, vsl] + bo_ref[:, vsl])
        out_ref[:, vsl] = (og * hn).astype(BF16)

    @pl.when(ci == pl.num_programs(1) - 1)
    def _():
        for h in range(nh):
            cx = cx_ref[h]
            c_ref[h] = cx[:, :DV_C]
            n_ref[h:h + 1, :] = cx[:, DV_C:].T[0:1, :]


def mlstm_seq(proj, gates, nbatch, seq_len, gbias, b_o, g_norm):
    ln = min(MLSTM_L, seq_len)
    p3 = proj.reshape(nbatch, seq_len, IN_C_MAIN)
    g3 = gates.reshape(nbatch, seq_len, GATE_PAD)
    in_specs, args = _unzip([
        (pl.BlockSpec((None, ln, QC), lambda b, c: (b, c, 0)), p3),
        (pl.BlockSpec((None, ln, QC), lambda b, c: (b, c, 1)), p3),
        (pl.BlockSpec((None, ln, VC), lambda b, c: (b, c, 2 * QC // VC)), p3),
        (pl.BlockSpec((None, ln, VC), lambda b, c: (b, c, 2 * QC // VC + 1)), p3),
        (pl.BlockSpec((None, ln, GATE_PAD), lambda b, c: (b, c, 0)), g3),
        _layer(gbias, (1, GATE_PAD), lambda b, c: (0, 0)),
        _layer(b_o, (1, VC), lambda b, c: (0, 0)),
        _layer(g_norm, (1, VC), lambda b, c: (0, 0))])
    out, c, n, m = pl.pallas_call(
        _mlstm_seq_kernel,
        grid=(nbatch, seq_len // ln),
        in_specs=in_specs,
        out_specs=[pl.BlockSpec((None, ln, VC), lambda b, c: (b, c, 0)),
                   pl.BlockSpec((None, N_HEADS_C, DK_C, DV_C), lambda b, c: (b, 0, 0, 0)),
                   pl.BlockSpec((None, N_HEADS_C, DK_C), lambda b, c: (b, 0, 0)),
                   pl.BlockSpec((None, N_HEADS_C, LANES), lambda b, c: (b, 0, 0))],
        out_shape=[jax.ShapeDtypeStruct((nbatch, seq_len, VC), BF16),
                   jax.ShapeDtypeStruct((nbatch, N_HEADS_C, DK_C, DV_C), F32),
                   jax.ShapeDtypeStruct((nbatch, N_HEADS_C, DK_C), F32),
                   jax.ShapeDtypeStruct((nbatch, N_HEADS_C, LANES), F32)],
        scratch_shapes=[pltpu.VMEM((N_HEADS_C, DK_C, DV_C + LANES), F32)],
        compiler_params=_cparams(2),
        name="mlstm_seq",
    )(*args)
    return out.reshape(nbatch * seq_len, VC), c, n, m[:, :, 0]


def _mlstm_step_kernel(q_ref, k_ref, v_ref, o_ref, gate_ref, gbias_ref, bo_ref, gn_ref, c0_ref, n0_ref, m0_ref,
                       out_ref, c_ref, n_ref, m_ref):
    q8 = q_ref[...]
    k8 = k_ref[...] * (DK_C ** -0.5)
    v8 = v_ref[...]
    gate = gate_ref[...] + gbias_ref[...]
    i8 = gate[:, 0:1]
    lf8 = _log_sigmoid(gate[:, 1:2])
    g8 = lf8 + m0_ref[...]
    mt = jnp.maximum(g8, i8)
    inter = jnp.exp(g8 - mt)
    wl = jnp.exp(i8 - mt)
    a8 = wl * jnp.sum(q8 * k8, axis=1, keepdims=True)
    row = lax.broadcasted_iota(jnp.int32, (N_HEADS_C, 1), 0)
    qb = q8.astype(BF16)
    kw = k8 * wl
    qc = jnp.zeros((N_HEADS_C, DV_C), F32)
    for h in range(N_HEADS_C):
        qc = qc + jnp.where(row == h, jnp.dot(qb, c0_ref[h].astype(BF16), preferred_element_type=F32), 0.0)
    vb = v8.astype(BF16)
    for h in range(N_HEADS_C):
        kz = jnp.where(row == h, kw, 0.0).astype(BF16)
        c_ref[h] = inter[h:h + 1, :] * c0_ref[h] + lax.dot_general(kz, vb, (((0,), (0,)), ((), ())),
                                                                   preferred_element_type=F32)
    n0 = n0_ref[...]
    num = inter * qc + a8 * v8
    den = inter * jnp.sum(q8 * n0, axis=1, keepdims=True) + a8
    hh = num / jnp.maximum(jnp.abs(den), jnp.exp(-mt))
    n_ref[...] = inter * n0 + kw
    m_ref[...] = mt
    hn = _rms(hh, gn_ref[...])
    og = _sigmoid(o_ref[...] + bo_ref[...])
    out_ref[...] = (og * hn).astype(BF16)


def mlstm_step(proj, gates, gbias, b_o, g_norm, c0, n0, m0):
    n = proj.shape[0]
    h = N_HEADS_C
    q = proj[:, :QC].reshape(n, h, DK_C)
    k = proj[:, QC:2 * QC].reshape(n, h, DK_C)
    v = proj[:, 2 * QC:2 * QC + VC].reshape(n, h, DV_C)
    o = proj[:, 2 * QC + VC:].reshape(n, h, DV_C)
    gcols = jnp.swapaxes(gates[:, :2 * h].reshape(n, 2, h), 1, 2)
    gb = jnp.swapaxes(gbias[0][gbias[1], 0, :2 * h].reshape(2, h), 0, 1)
    nl = c0[0].shape[0]
    per = lambda *tail: pl.BlockSpec((None,) + tail, lambda z: (z,) + (0,) * len(tail))
    lper = lambda o_, *tail: _layer(o_, (None,) + tail, lambda z: (z,) + (0,) * len(tail))
    lshared = lambda o_, *shape: _layer(o_, shape, lambda z: (0,) * len(shape))
    in_specs, args = _unzip([
        (per(h, DK_C), q), (per(h, DK_C), k), (per(h, DV_C), v), (per(h, DV_C), o), (per(h, 2), gcols),
        (pl.BlockSpec((h, 2), lambda z: (0, 0)), gb),
        lshared((b_o[0].reshape(nl, h, DV_C), b_o[1]), h, DV_C),
        lshared((g_norm[0].reshape(nl, h, DV_C), g_norm[1]), h, DV_C),
        lper(c0, h, DK_C, DV_C), lper(n0, h, DK_C), lper((m0[0].reshape(nl, n, h, 1), m0[1]), h, 1)])
    out, c, nn, m = pl.pallas_call(
        _mlstm_step_kernel,
        grid=(n,),
        in_specs=in_specs,
        out_specs=[per(h, DV_C), per(h, DK_C, DV_C), per(h, DK_C), per(h, 1)],
        out_shape=[jax.ShapeDtypeStruct((n, h, DV_C), BF16),
                   jax.ShapeDtypeStruct((n, h, DK_C, DV_C), F32),
                   jax.ShapeDtypeStruct((n, h, DK_C), F32),
                   jax.ShapeDtypeStruct((n, h, 1), F32)],
        compiler_params=_cparams(1),
        name="mlstm_step",
    )(*args)
    return out.reshape(n, VC), c, nn, m.reshape(n, h)


def _prep_weights(w):
    a_in = w['a_w_in']
    q, k, v, xr, gr = jnp.split(a_in, [QA, QA + KA, QA + 2 * KA, QA + 2 * KA + D_RNN], axis=-1)
    nc = w['c_w_in'].shape[0]
    gbias = jnp.concatenate([w['c_b_i'], w['c_b_f'], jnp.zeros((nc, GATE_PAD - 2 * N_HEADS_C), F32)], axis=-1)
    return {
        'a_w_in': jnp.concatenate([q, xr, gr, k, v], axis=-1).astype(BF16),
        'a_w_out': None,
        'c_w_in': None,
        'c_w_gate': None,
        'c_gbias': gbias.reshape(nc, 1, GATE_PAD),
        'c_w_out': None,
        'ffn_w_up': [],
        'ffn_w_down': [],
        'ffn_cp': _ffn_conv_params(w['ffn_conv_w'], w['ffn_conv_b']),
        'lru_wa': jax.vmap(_lru_blockdiag)(w['a_lru_wa']),
        'lru_wx': jax.vmap(_lru_blockdiag)(w['a_lru_wx']),
    }


def _trunk(x, seq_len, state, w, pw):
    m = x.shape[0]
    nbatch = m // seq_len
    outs = {'k': [], 'v': [], 'h': [], 'conv': [], 'C': [], 'n': [], 'm': [], 'f': []}

    def vec(name, idx):
        a = w[name]
        return a.reshape(a.shape[0], 1, a.shape[1]), idx

    for l in range(DEPTH):
        j = l // 2
        if l % 2 == 0:
            proj = norm_matmul(x, vec('norm_mix_pre', l), (pw['a_w_in'], j), IN_A, _proj_tn(m, IN_A))
            lru_args = ((w['a_conv_w'], j), vec('a_conv_b', j), (pw['lru_wa'], j), (pw['lru_wx'], j),
                        vec('a_lru_ba', j), vec('a_lru_bx', j), vec('a_lru_lambda', j))
            if state is None:
                first = ((w['ffn_w_up'], 0), (w['ffn_w_down'], 0), (w['a_w_out'].reshape(-1, D_MODEL), None),
                         (w['c_w_out'].reshape(-1, D_MODEL), None)) if l == 0 else ()
                att, k_new, v_new, cast = swa_prompt(proj, w['a_sinks'][j], nbatch, seq_len, cast_f32=first,
                                                     cast_interleave=(True, False, False, False) if l == 0 else ())
                if cast:
                    pw['ffn_w_up'].append(cast[0])
                    pw['ffn_w_down'].append(cast[1])
                    pw['a_w_out'] = cast[2].reshape(w['a_w_out'].shape)
                    pw['c_w_out'] = cast[3].reshape(w['c_w_out'].shape)
                job = (jnp.swapaxes(w['c_w_in'], 1, 2), IN_C_MAIN) if l == 0 else None
                rec, h_new, c_new, cast = lru_seq(proj, nbatch, seq_len, *lru_args, wt_job=job)
                if cast:
                    pw['c_w_in'], pw['c_w_gate'] = cast
            else:
                att, k_new, v_new = swa_sample(proj, (state['cache_k'], j), (state['cache_v'], j), w['a_sinks'][j])
                rec, h_new, c_new = lru_step(proj, (state['lru_h'], j), (state['lru_conv'], j), *lru_args)
            outs['k'].append(k_new.reshape(nbatch, -1, N_KV_A, HD_A))
            outs['v'].append(v_new.reshape(nbatch, -1, N_KV_A, HD_A))
            outs['h'].append(h_new)
            outs['conv'].append(c_new)
            x = out_proj([att, rec], (pw['a_w_out'], j), x, vec('norm_mix_post', l))
        else:
            proj, gates = norm_matmul(x, vec('norm_mix_pre', l), (pw['c_w_in'], j), IN_C_MAIN,
                                      _proj_tn(m, IN_C_MAIN), w_side=(pw['c_w_gate'], j))
            mix_args = ((pw['c_gbias'], j), vec('c_b_o', j), vec('c_norm', j))
            if state is None:
                hg, c_new, n_new, m_new = mlstm_seq(proj, gates, nbatch, seq_len, *mix_args)
            else:
                hg, c_new, n_new, m_new = mlstm_step(proj, gates, *mix_args, (state['mlstm_C'], j),
                                                     (state['mlstm_n'], j), (state['mlstm_m'], j))
            outs['C'].append(c_new)
            outs['n'].append(n_new)
            outs['m'].append(m_new)
            x = out_proj([hg], (pw['c_w_out'], j), x, vec('norm_mix_post', l))
        if state is None:
            nxt = ((w['ffn_w_up'], l + 1), (w['ffn_w_down'], l + 1)) if l + 1 < DEPTH else None
            x, f_new, cast = ffn_seq(x, seq_len, vec('norm_ffn_pre', l), pw['ffn_w_up'][l], (pw['ffn_cp'], l),
                                     pw['ffn_w_down'][l], vec('norm_ffn_post', l), next_f32=nxt)
            if cast:
                pw['ffn_w_up'].append(cast[0])
                pw['ffn_w_down'].append(cast[1])
        else:
            x, f_new = ffn_step(x, (state['ffn_conv'], l), vec('norm_ffn_pre', l), pw['ffn_w_up'][l],
                                (w['ffn_conv_w'], l), vec('ffn_conv_b', l), pw['ffn_w_down'][l],
                                vec('norm_ffn_post', l))
        outs['f'].append(f_new)
    return x, {name: jnp.stack(vals) for name, vals in outs.items()}


def kernel(x_prompt, x_sample, cache_k, cache_v, state_lru_h, state_lru_conv, state_mlstm_C, state_mlstm_n, state_mlstm_m, state_ffn_conv, norm_mix_pre, norm_mix_post, norm_ffn_pre, norm_ffn_post, a_w_in, a_sinks, a_conv_w, a_conv_b, a_lru_wa, a_lru_ba, a_lru_wx, a_lru_bx, a_lru_lambda, a_w_out, c_w_in, c_b_i, c_b_f, c_b_o, c_norm, c_w_out, ffn_w_up, ffn_conv_w, ffn_conv_b, ffn_w_down):
    w = {
        'norm_mix_pre': norm_mix_pre, 'norm_mix_post': norm_mix_post,
        'norm_ffn_pre': norm_ffn_pre, 'norm_ffn_post': norm_ffn_post,
        'a_w_in': a_w_in, 'a_sinks': a_sinks, 'a_conv_w': a_conv_w, 'a_conv_b': a_conv_b,
        'a_lru_wa': a_lru_wa, 'a_lru_ba': a_lru_ba, 'a_lru_wx': a_lru_wx, 'a_lru_bx': a_lru_bx,
        'a_lru_lambda': a_lru_lambda, 'a_w_out': a_w_out,
        'c_w_in': c_w_in, 'c_b_i': c_b_i, 'c_b_f': c_b_f, 'c_b_o': c_b_o, 'c_norm': c_norm, 'c_w_out': c_w_out,
        'ffn_w_up': ffn_w_up, 'ffn_conv_w': ffn_conv_w, 'ffn_conv_b': ffn_conv_b, 'ffn_w_down': ffn_w_down,
    }
    pw = _prep_weights(w)
    bp, sp, d = x_prompt.shape
    bs, ss, _ = x_sample.shape
    assert ss == 1, "the sample group advances one token per step"
    y_p, st_p = _trunk(x_prompt.reshape(bp * sp, d), sp, None, w, pw)
    state = {'cache_k': cache_k, 'cache_v': cache_v, 'lru_h': state_lru_h, 'lru_conv': state_lru_conv,
             'mlstm_C': state_mlstm_C, 'mlstm_n': state_mlstm_n, 'mlstm_m': state_mlstm_m,
             'ffn_conv': state_ffn_conv}
    y_s, st_s = _trunk(x_sample.reshape(bs * ss, d), 1, state, w, pw)
    return (y_p.reshape(bp, sp, d), y_s.reshape(bs, ss, d),
            st_p['k'], st_s['k'], st_p['v'], st_s['v'], st_p['h'], st_s['h'],
            st_p['conv'], st_s['conv'], st_p['C'], st_s['C'], st_p['n'], st_s['n'],
            st_p['m'], st_s['m'], st_p['f'], st_s['f'])
```

```python
import functools
import math

import jax
import jax.numpy as jnp
from jax import lax
from jax.experimental import pallas as pl
from jax.experimental.pallas import tpu as pltpu

F32 = jnp.float32
BF16 = jnp.bfloat16

D_MODEL = 2048
DEPTH = 4
PAST_LEN = 16384
N_HEADS_A = 16
N_KV_A = 4
HD_A = 64
GROUP_A = N_HEADS_A // N_KV_A
WINDOW = 128
BLOCK_A = 128
ROPE_THETA = 10000.0
D_RNN = 1024
N_BLOCKS_B = 16
BS_B = D_RNN // N_BLOCKS_B
CONV_B = 4
LRU_C = 8.0
N_HEADS_C = 8
DK_C = 128
DV_C = 256
D_FF = 6144
CONV_F = 3
EPS = 1e-6
QA = N_HEADS_A * HD_A
KA = N_KV_A * HD_A
IN_A = QA + 2 * KA + 2 * D_RNN
QC = N_HEADS_C * DK_C
VC = N_HEADS_C * DV_C
IN_C_MAIN = 2 * QC + 2 * VC

LANES = 128
SUBLANES = 8
VMEM_LIMIT_BYTES = 60 * 1024 * 1024

TM = 512
TM_PROJ = 1024
TF = 1024
LRU_TT = 256
MLSTM_L = 256
LRU_GROUP = 256
GATE_PAD = LANES
SAMPLES_PER_STEP = 8


def _cparams(n_axes):
    return pltpu.CompilerParams(dimension_semantics=("arbitrary",) * n_axes,
                                vmem_limit_bytes=VMEM_LIMIT_BYTES)


def _rms(x, g):
    ms = jnp.mean(x * x, axis=-1, keepdims=True)
    return x * lax.rsqrt(ms + EPS) * g


def _gelu_tanh(x):
    c = math.sqrt(2.0 / math.pi)
    hx = 0.5 * x
    return hx + hx * jnp.tanh(x * (c + (c * 0.044715) * (x * x)))


def _sigmoid(x):
    return 0.5 + 0.5 * jnp.tanh(0.5 * x)


def _log_sigmoid(x):
    return jnp.minimum(x, 0.0) - jnp.log1p(jnp.exp(-jnp.abs(x)))


def _norm_matmul_kernel(x_ref, g_ref, w_ref, *rest, with_side):
    o_ref, hn_ref = rest[1 if with_side else 0], rest[-1]

    @pl.when(pl.program_id(1) == 0)
    def _():
        hn_ref[...] = _rms(x_ref[...], g_ref[...]).astype(BF16)
        if with_side:
            rest[2][...] = jnp.dot(hn_ref[...], rest[0][...], preferred_element_type=F32)

    o_ref[...] = jnp.dot(hn_ref[...], w_ref[...], preferred_element_type=F32)


def _proj_tn(m, n):
    del m
    budget = 8 * 1024 * 1024
    best = LANES
    for tn in range(LANES, n + 1, LANES):
        if n % tn == 0 and D_MODEL * tn * 2 <= budget:
            best = tn
    return best


def _layer(opnd, block, imap):
    arr, idx = opnd
    return pl.BlockSpec((None,) + tuple(block), lambda *g: (idx,) + tuple(imap(*g))), arr


def _unzip(pairs):
    return [p[0] for p in pairs], [p[1] for p in pairs]


def _cast_slabs(srcs, dsts, interleave):
    for src, dst, il in zip(srcs, dsts, interleave):
        if il:
            for c in range(D_FF // TF):
                dst[:, 2 * c * TF:(2 * c + 1) * TF] = src[:, c * TF:(c + 1) * TF].astype(BF16)
                dst[:, (2 * c + 1) * TF:(2 * c + 2) * TF] = src[:, D_FF + c * TF:D_FF + (c + 1) * TF].astype(BF16)
        else:
            dst[...] = src[...].astype(BF16)


def _add_cast_job(jobs, nsteps, step_of, in_specs, args, out_specs, out_shape):
    for arr, layer in jobs:
        rows, ncols = arr.shape[-2:]
        rw, rem = divmod(rows, nsteps)
        assert rem == 0 and rw % (2 * SUBLANES) == 0 and ncols % LANES == 0, (rows, nsteps, ncols)
        if layer is None:
            in_specs.append(pl.BlockSpec((rw, ncols), lambda *g: (step_of(*g), 0)))
        else:
            in_specs.append(pl.BlockSpec((None, rw, ncols), lambda *g, _l=layer: (_l, step_of(*g), 0)))
        args.append(arr)
        out_specs.append(pl.BlockSpec((rw, ncols), lambda *g: (step_of(*g), 0)))
        out_shape.append(jax.ShapeDtypeStruct((rows, ncols), BF16))


def norm_matmul(x, g, w, n, tn, w_side=None):
    m, k = x.shape
    assert n % tn == 0 and n <= w[0].shape[-1]
    tm = min(TM_PROJ, m)
    specs, args = _unzip([(pl.BlockSpec((tm, k), lambda i, j: (i, 0)), x),
                          _layer(g, (1, k), lambda i, j: (0, 0)),
                          _layer(w, (k, tn), lambda i, j: (0, j))])
    in_specs = list(specs)
    out_specs = [pl.BlockSpec((tm, tn), lambda i, j: (i, j))]
    out_shape = [jax.ShapeDtypeStruct((m, n), F32)]
    if w_side is not None:
        ns = w_side[0].shape[-1]
        spec, arr = _layer(w_side, (k, ns), lambda i, j: (0, 0))
        in_specs.append(spec)
        out_specs.append(pl.BlockSpec((tm, ns), lambda i, j: (i, 0)))
        out_shape.append(jax.ShapeDtypeStruct((m, ns), F32))
        args.append(arr)
    outs = pl.pallas_call(
        functools.partial(_norm_matmul_kernel, with_side=w_side is not None),
        grid=(m // tm, n // tn),
        in_specs=in_specs,
        out_specs=out_specs,
        out_shape=out_shape,
        scratch_shapes=[pltpu.VMEM((tm, k), BF16)],
        compiler_params=_cparams(2),
        name="norm_matmul",
    )(*args)
    return outs[0] if w_side is None else outs


def _out_proj_kernel(*refs, n_in):
    a_refs = refs[:n_in]
    w_ref, x_ref, g_ref, o_ref = refs[n_in:]
    tm = x_ref.shape[0]
    halves = 2 if tm % (2 * LANES) == 0 else 1
    hr = tm // halves
    for s in range(halves):
        rows = slice(s * hr, (s + 1) * hr)
        acc = None
        row = 0
        for a_ref in a_refs:
            ka = a_ref.shape[1]
            part = jnp.dot(a_ref[rows, :], w_ref[row:row + ka, :], preferred_element_type=F32)
            acc = part if acc is None else acc + part
            row += ka
        o_ref[rows, :] = x_ref[rows, :] + _rms(acc, g_ref[...])


def out_proj(inputs, w, x, g):
    m, d = x.shape
    tm = min(TM, m)
    w_spec, w_arr = _layer(w, w[0].shape[1:], lambda i: (0, 0))
    g_spec, g_arr = _layer(g, (1, d), lambda i: (0, 0))
    in_specs = [pl.BlockSpec((tm, a.shape[1]), lambda i: (i, 0)) for a in inputs]
    in_specs += [w_spec, pl.BlockSpec((tm, d), lambda i: (i, 0)), g_spec]
    return pl.pallas_call(
        functools.partial(_out_proj_kernel, n_in=len(inputs)),
        grid=(m // tm,),
        in_specs=in_specs,
        out_specs=pl.BlockSpec((tm, d), lambda i: (i, 0)),
        out_shape=jax.ShapeDtypeStruct((m, d), F32),
        compiler_params=_cparams(1),
        name="out_proj",
    )(*inputs, w_arr, x, g_arr)


def _ffn_conv_params(convw, convb):
    nl = convw.shape[0]
    p = jnp.concatenate([convw, convb[:, None, :], jnp.zeros((nl, SUBLANES - CONV_F - 1, 2 * D_FF), F32)], axis=1)
    return jnp.swapaxes(p.reshape(nl, SUBLANES, 2 * D_FF // TF, TF), 1, 2)


def _ffn_conv(u, prev, cp):
    cw = cp[0:CONV_F, :]
    cb = cp[CONV_F:CONV_F + 1, :]
    y = pltpu.roll(u, 2, axis=0) * cw[0:1, :] + pltpu.roll(u, 1, axis=0) * cw[1:2, :] + u * cw[2:3, :] + cb
    head = jnp.concatenate([prev, u[0:SUBLANES, :]], axis=0)
    y_head = (head[SUBLANES - 2:2 * SUBLANES - 2, :] * cw[0:1, :] + head[SUBLANES - 1:2 * SUBLANES - 1, :] * cw[1:2, :]
              + head[SUBLANES:, :] * cw[2:3, :] + cb)
    return jnp.concatenate([y_head, y[SUBLANES:, :]], axis=0)


def _ffn_seq_kernel(x_ref, gpre_ref, wgv_ref, cp_ref, wd_ref, gpost_ref,
                    *rest, tiles_per_seq, cast_next, n_chunks):
    if cast_next:
        wupn_ref, wdn_ref, o_ref, nu_ref, wupn_out, wdn_out, hn_ref, carry_ref = rest
        wupn_out[...] = wupn_ref[...].astype(BF16)
        wdn_out[...] = wdn_ref[...].astype(BF16)
    else:
        o_ref, nu_ref, hn_ref, carry_ref = rest
    i = pl.program_id(0)
    f = pl.program_id(1)
    nf = n_chunks
    tm = x_ref.shape[0]

    @pl.when(f == 0)
    def _():
        hn_ref[...] = _rms(x_ref[...], gpre_ref[...]).astype(BF16)
        o_ref[...] = jnp.zeros_like(o_ref)

    @pl.when(i % tiles_per_seq == 0)
    def _():
        carry_ref[f] = jnp.zeros(carry_ref.shape[1:], F32)

    u = jnp.dot(hn_ref[...], wgv_ref[...], preferred_element_type=F32)
    tf = u.shape[1] // 2
    ug, uv = u[:, :tf], u[:, tf:]
    cg = _ffn_conv(ug, carry_ref[f, 0], cp_ref[f])
    cv = _ffn_conv(uv, carry_ref[f, 1], cp_ref[f + nf])
    carry_ref[f, 0] = ug[tm - SUBLANES:tm, :]
    carry_ref[f, 1] = uv[tm - SUBLANES:tm, :]
    nu_ref[0] = ug[tm - (CONV_F - 1):tm, :]
    nu_ref[1] = uv[tm - (CONV_F - 1):tm, :]
    act = (_gelu_tanh(cg) * cv).astype(BF16)
    o_ref[...] += jnp.dot(act, wd_ref[...], preferred_element_type=F32)

    @pl.when(f == nf - 1)
    def _():
        o_ref[...] = x_ref[...] + _rms(o_ref[...], gpost_ref[...])


def ffn_seq(x, seq_len, gpre, wup, cp, wdown, gpost, next_f32=None):
    m, d = x.shape
    tm = min(TM, seq_len)
    tps = seq_len // tm
    nb = m // seq_len
    nf = D_FF // TF
    nsteps = (m // tm) * nf
    cast_next = next_f32 is not None
    kern = functools.partial(_ffn_seq_kernel, tiles_per_seq=tps, cast_next=cast_next, n_chunks=nf)
    in_specs, args = _unzip([
        (pl.BlockSpec((tm, d), lambda i, f: (i, 0)), x),
        _layer(gpre, (1, d), lambda i, f: (0, 0)),
        (pl.BlockSpec((d, 2 * TF), lambda i, f: (0, f)), wup),
        _layer(cp, cp[0].shape[1:], lambda i, f: (0, 0, 0)),
        (pl.BlockSpec((TF, d), lambda i, f: (f, 0)), wdown),
        _layer(gpost, (1, d), lambda i, f: (0, 0))])
    out_specs = [pl.BlockSpec((tm, d), lambda i, f: (i, 0)),
                 pl.BlockSpec((None, 2, CONV_F - 1, TF), lambda i, f: (i, 0, 0, f))]
    out_shape = [jax.ShapeDtypeStruct((m, d), F32),
                 jax.ShapeDtypeStruct((m // tm, 2, CONV_F - 1, D_FF), F32)]
    if cast_next:
        cw, rem_c = divmod(2 * D_FF, nsteps)
        rw, rem_r = divmod(D_FF, nsteps)
        assert rem_c == 0 and rem_r == 0 and cw % LANES == 0 and rw % (2 * SUBLANES) == 0, (nsteps, cw, rw)
        up_in, up_arr = _layer(next_f32[0], (d, cw), lambda i, f: (0, i * nf + f))
        dn_in, dn_arr = _layer(next_f32[1], (rw, d), lambda i, f: (i * nf + f, 0))
        in_specs += [up_in, dn_in]
        args += [up_arr, dn_arr]
        per, half = TF // cw, D_FF // cw
        assert TF % cw == 0

        def up_dst(i, f):
            s = i * nf + f
            return 0, ((s % half) // per) * (2 * per) + (s // half) * per + s % per

        out_specs += [pl.BlockSpec((d, cw), up_dst),
                      pl.BlockSpec((rw, d), lambda i, f: (i * nf + f, 0))]
        out_shape += [jax.ShapeDtypeStruct((d, 2 * D_FF), BF16), jax.ShapeDtypeStruct((D_FF, d), BF16)]
    outs = pl.pallas_call(
        kern,
        grid=(m // tm, nf),
        in_specs=in_specs,
        out_specs=out_specs,
        out_shape=out_shape,
        scratch_shapes=[pltpu.VMEM((tm, d), BF16),
                        pltpu.VMEM((nf, 2, SUBLANES, TF), F32)],
        compiler_params=_cparams(2),
        name="ffn_seq",
    )(*args)
    y, nu = outs[:2]
    last = nu.reshape(nb, tps, 2, CONV_F - 1, D_FF)[:, tps - 1]
    last = jnp.swapaxes(last, 1, 2).reshape(nb, CONV_F - 1, 2 * D_FF)
    return y, last, tuple(outs[2:])


def _ffn_step_kernel(x_ref, gpre_ref, wg_ref, wv_ref, cwg_ref, cwv_ref, cbg_ref, cbv_ref, wd_ref, gpost_ref,
                     pg_ref, pv_ref,
                     o_ref, ng_ref, nv_ref, hn_ref, acc_ref):
    f = pl.program_id(0)
    nf = pl.num_programs(0)

    @pl.when(f == 0)
    def _():
        hn_ref[...] = _rms(x_ref[...], gpre_ref[...]).astype(BF16)
        acc_ref[...] = jnp.zeros_like(acc_ref)

    hn = hn_ref[...]
    ug = jnp.dot(hn, wg_ref[...], preferred_element_type=F32)
    uv = jnp.dot(hn, wv_ref[...], preferred_element_type=F32)
    p0g, p1g = pg_ref[:, 0, :], pg_ref[:, 1, :]
    p0v, p1v = pv_ref[:, 0, :], pv_ref[:, 1, :]
    ng_ref[:, 0, :] = p1g
    ng_ref[:, 1, :] = ug
    nv_ref[:, 0, :] = p1v
    nv_ref[:, 1, :] = uv
    cwg = cwg_ref[...]
    cwv = cwv_ref[...]
    cg = p0g * cwg[0:1, :] + p1g * cwg[1:2, :] + ug * cwg[2:3, :] + cbg_ref[...]
    cv = p0v * cwv[0:1, :] + p1v * cwv[1:2, :] + uv * cwv[2:3, :] + cbv_ref[...]
    act = (_gelu_tanh(cg) * cv).astype(BF16)
    acc_ref[...] += jnp.dot(act, wd_ref[...], preferred_element_type=F32)

    @pl.when(f == nf - 1)
    def _():
        o_ref[...] = x_ref[...] + _rms(acc_ref[...], gpost_ref[...])


def ffn_step(x, past, gpre, wup, convw, convb, wdown, gpost):
    n, d = x.shape
    nf = D_FF // TF
    phalf = lambda c: _layer(past, (n, CONV_F - 1, TF), lambda f: (0, 0, f + c * nf))
    in_specs, args = _unzip([
        (pl.BlockSpec((n, d), lambda f: (0, 0)), x),
        _layer(gpre, (1, d), lambda f: (0, 0)),
        (pl.BlockSpec((d, TF), lambda f: (0, 2 * f)), wup),
        (pl.BlockSpec((d, TF), lambda f: (0, 2 * f + 1)), wup),
        _layer(convw, (CONV_F, TF), lambda f: (0, f)),
        _layer(convw, (CONV_F, TF), lambda f: (0, f + nf)),
        _layer(convb, (1, TF), lambda f: (0, f)),
        _layer(convb, (1, TF), lambda f: (0, f + nf)),
        (pl.BlockSpec((TF, d), lambda f: (f, 0)), wdown),
        _layer(gpost, (1, d), lambda f: (0, 0)),
        phalf(0), phalf(1)])
    y, ng, nv = pl.pallas_call(
        _ffn_step_kernel,
        grid=(nf,),
        in_specs=in_specs,
        out_specs=[pl.BlockSpec((n, d), lambda f: (0, 0)),
                   pl.BlockSpec((n, CONV_F - 1, TF), lambda f: (0, 0, f)),
                   pl.BlockSpec((n, CONV_F - 1, TF), lambda f: (0, 0, f))],
        out_shape=[jax.ShapeDtypeStruct((n, d), F32),
                   jax.ShapeDtypeStruct((n, CONV_F - 1, D_FF), F32),
                   jax.ShapeDtypeStruct((n, CONV_F - 1, D_FF), F32)],
        scratch_shapes=[pltpu.VMEM((n, d), BF16), pltpu.VMEM((n, d), F32)],
        compiler_params=_cparams(1),
        name="ffn_step",
    )(*args)
    return y, jnp.concatenate([ng, nv], axis=-1)


def _rope_tables(pos0, t):
    half = HD_A // 2
    pos = (pos0 + jnp.arange(t)).astype(F32)
    inv = ROPE_THETA ** (-jnp.arange(half, dtype=F32) / half)
    ang = pos[:, None] * inv[None, :]
    cos, sin = jnp.cos(ang), jnp.sin(ang)
    zero = jnp.zeros_like(sin)
    cos_h = jnp.concatenate([cos, cos], axis=-1)
    sin_up = jnp.concatenate([-sin, zero], axis=-1)
    sin_dn = jnp.concatenate([zero, sin], axis=-1)
    two = lambda a: jnp.concatenate([a, a], axis=-1)
    return two(cos_h), two(sin_up), two(sin_dn)


def _rope(x, cos, sin_up, sin_dn):
    w = x.shape[1]
    reps = w // LANES
    tile = lambda a: jnp.concatenate([a] * reps, axis=1) if reps > 1 else a
    x_up = pltpu.roll(x, w - HD_A // 2, axis=1)
    x_dn = pltpu.roll(x, HD_A // 2, axis=1)
    return x * tile(cos) + x_up * tile(sin_up) + x_dn * tile(sin_dn)


def _dup_half(a, half):
    lane = lax.broadcasted_iota(jnp.int32, a.shape, 1)
    sw = pltpu.roll(a, HD_A, axis=1)
    lo = lane < HD_A
    return jnp.where(lo, a, sw) if half == 0 else jnp.where(lo, sw, a)


def _sink_softmax(s, sk_rows):
    m = jnp.maximum(jnp.max(s, axis=-1, keepdims=True), sk_rows)
    p = jnp.exp(s - m)
    return p, jnp.sum(p, axis=-1, keepdims=True) + jnp.exp(sk_rows - m)


def _swa_prompt_kernel(sinks_ref, q_ref, kc_ref, kp_ref, vc_ref, vp_ref,
                       cc_ref, suc_ref, sdc_ref, cp_ref, sup_ref, sdp_ref, *rest, cast_interleave):
    nc = len(cast_interleave)
    att_ref, knew_ref, vnew_ref = rest[nc:nc + 3]
    _cast_slabs(rest[:nc], rest[nc + 3:], cast_interleave)
    c = pl.program_id(1)
    nb = pl.num_programs(1)
    bq = q_ref.shape[0]
    q = _rope(q_ref[...], cc_ref[...], suc_ref[...], sdc_ref[...])
    kc = _rope(kc_ref[...], cc_ref[...], suc_ref[...], sdc_ref[...])
    kp = _rope(kp_ref[...], cp_ref[...], sup_ref[...], sdp_ref[...])
    vc = vc_ref[...]
    vp = vp_ref[...]

    @pl.when(c == nb - 1)
    def _():
        knew_ref[...] = kc
        vnew_ref[...] = vc

    rows = GROUP_A * bq
    r = lax.broadcasted_iota(jnp.int32, (rows, 2 * bq), 0) % bq
    col = lax.broadcasted_iota(jnp.int32, (rows, 2 * bq), 1)
    valid = (col >= r) & (col <= r + WINDOW) & ((c > 0) | (col >= bq))
    lane = lax.broadcasted_iota(jnp.int32, (bq, LANES), 1)
    lo = lane < HD_A
    row_head = lax.broadcasted_iota(jnp.int32, (rows, 1), 0) // bq

    for h in range(N_KV_A):
        t, half = divmod(h, 2)
        ksl = slice(t * LANES, (t + 1) * LANES)
        kd = _dup_half(jnp.concatenate([kp[:, ksl], kc[:, ksl]], axis=0), half).astype(BF16)
        vd = _dup_half(jnp.concatenate([vp[:, ksl], vc[:, ksl]], axis=0), half).astype(BF16)
        parts = []
        for p in range(2):
            q2 = q[:, (2 * h + p) * LANES:(2 * h + p + 1) * LANES]
            parts.append(jnp.where(lo, q2, 0.0))
            parts.append(jnp.where(lo, 0.0, q2))
        qs = (jnp.concatenate(parts, axis=0) * (HD_A ** -0.5)).astype(BF16)
        s = lax.dot_general(qs, kd, (((1,), (1,)), ((), ())), preferred_element_type=F32)
        s = jnp.where(valid, s, -jnp.inf)
        sk = jnp.zeros((rows, 1), F32)
        for g in range(GROUP_A):
            sk = jnp.where(row_head == g, sinks_ref[h * GROUP_A + g], sk)
        pr, den = _sink_softmax(s, sk)
        o = jnp.dot(pr.astype(BF16), vd, preferred_element_type=F32) / den
        for p in range(2):
            o_lo = o[(2 * p) * bq:(2 * p + 1) * bq, :]
            o_hi = o[(2 * p + 1) * bq:(2 * p + 2) * bq, :]
            att_ref[:, (2 * h + p) * LANES:(2 * h + p + 1) * LANES] = jnp.where(lo, o_lo, o_hi).astype(BF16)


def swa_prompt(proj, sinks, nbatch, seq_len, cast_f32=(), cast_interleave=()):
    nb = seq_len // BLOCK_A
    p3 = proj.reshape(nbatch, seq_len, IN_A)
    kcol = (QA + 2 * D_RNN) // KA
    cos, sup, sdn = _rope_tables(0, seq_len)
    prev = lambda c: jnp.maximum(c - 1, 0)
    tab_c = pl.BlockSpec((BLOCK_A, LANES), lambda b, c: (c, 0))
    tab_p = pl.BlockSpec((BLOCK_A, LANES), lambda b, c: (prev(c), 0))
    in_specs = [pl.BlockSpec(memory_space=pltpu.SMEM),
                pl.BlockSpec((None, BLOCK_A, QA), lambda b, c: (b, c, 0)),
                pl.BlockSpec((None, BLOCK_A, KA), lambda b, c: (b, c, kcol)),
                pl.BlockSpec((None, BLOCK_A, KA), lambda b, c: (b, prev(c), kcol)),
                pl.BlockSpec((None, BLOCK_A, KA), lambda b, c: (b, c, kcol + 1)),
                pl.BlockSpec((None, BLOCK_A, KA), lambda b, c: (b, prev(c), kcol + 1)),
                tab_c, tab_c, tab_c, tab_p, tab_p, tab_p]
    args = [sinks, p3, p3, p3, p3, p3, cos, sup, sdn, cos, sup, sdn]
    out_specs = [pl.BlockSpec((None, BLOCK_A, QA), lambda b, c: (b, c, 0)),
                 pl.BlockSpec((None, BLOCK_A, KA), lambda b, c: (b, 0, 0)),
                 pl.BlockSpec((None, BLOCK_A, KA), lambda b, c: (b, 0, 0))]
    out_shape = [jax.ShapeDtypeStruct((nbatch, seq_len, QA), BF16),
                 jax.ShapeDtypeStruct((nbatch, BLOCK_A, KA), F32),
                 jax.ShapeDtypeStruct((nbatch, BLOCK_A, KA), F32)]
    assert len(cast_f32) == len(cast_interleave)
    _add_cast_job(cast_f32, nbatch * nb, lambda b, c: b * nb + c, in_specs, args, out_specs, out_shape)
    outs = pl.pallas_call(
        functools.partial(_swa_prompt_kernel, cast_interleave=tuple(cast_interleave)),
        grid=(nbatch, nb),
        in_specs=in_specs,
        out_specs=out_specs,
        out_shape=out_shape,
        compiler_params=_cparams(2),
        name="swa_prompt",
    )(*args)
    att, knew, vnew = outs[:3]
    return att.reshape(nbatch * seq_len, QA), knew, vnew, tuple(outs[3:])


def _swa_sample_kernel(sinks_ref, q_ref, k_ref, v_ref, kc_ref, vc_ref, cos_ref, sup_ref, sdn_ref,
                       att_ref, knew_ref, vnew_ref):
    w = kc_ref.shape[1]
    lane = lax.broadcasted_iota(jnp.int32, (1, LANES), 1)
    lo = lane < HD_A
    row = lax.broadcasted_iota(jnp.int32, (SUBLANES, 1), 0)
    scale = HD_A ** -0.5
    todo = []
    for z in range(q_ref.shape[0]):
        q = _rope(q_ref[z], cos_ref[...], sup_ref[...], sdn_ref[...])
        k = _rope(k_ref[z], cos_ref[...], sup_ref[...], sdn_ref[...])
        v = v_ref[z]
        knew_ref[z, 0:w - 1, :] = kc_ref[z, 1:w, :]
        knew_ref[z, w - 1:w, :] = k
        vnew_ref[z, 0:w - 1, :] = vc_ref[z, 1:w, :]
        vnew_ref[z, w - 1:w, :] = v
        for h in range(N_KV_A):
            t, half = divmod(h, 2)
            ksl = slice(t * LANES, (t + 1) * LANES)
            kd = _dup_half(kc_ref[z, :, ksl], half)
            kd_new = _dup_half(k[:, ksl], half)
            parts = []
            for p in range(2):
                q2 = q[:, (2 * h + p) * LANES:(2 * h + p + 1) * LANES]
                parts.append(jnp.where(lo, q2, 0.0))
                parts.append(jnp.where(lo, 0.0, q2))
            parts.append(jnp.zeros((SUBLANES - GROUP_A, LANES), F32))
            qs = jnp.concatenate(parts, axis=0)
            s = lax.dot_general(qs.astype(BF16), kd.astype(BF16), (((1,), (1,)), ((), ())),
                                preferred_element_type=F32) * scale
            s_new = jnp.sum(qs * kd_new, axis=-1, keepdims=True) * scale
            todo.append((z, h, s, s_new, v))
    probs = []
    for z, h, s, s_new, v in todo:
        sk = jnp.zeros((SUBLANES, 1), F32)
        for g in range(GROUP_A):
            sk = jnp.where(row == g, sinks_ref[h * GROUP_A + g], sk)
        m = jnp.maximum(jnp.maximum(jnp.max(s, axis=-1, keepdims=True), s_new), sk)
        pr = jnp.exp(s - m)
        p_new = jnp.exp(s_new - m)
        den = jnp.sum(pr, axis=-1, keepdims=True) + p_new + jnp.exp(sk - m)
        probs.append((z, h, (pr / den).astype(BF16), p_new / den, v))
    for z, h, pr, p_new, v in probs:
        t, half = divmod(h, 2)
        ksl = slice(t * LANES, (t + 1) * LANES)
        vd = _dup_half(vc_ref[z, :, ksl], half)
        vd_new = _dup_half(v[:, ksl], half)
        o = jnp.dot(pr, vd.astype(BF16), preferred_element_type=F32) + p_new * vd_new
        for p in range(2):
            att_ref[z, :, (2 * h + p) * LANES:(2 * h + p + 1) * LANES] = jnp.where(
                lo, o[2 * p:2 * p + 1, :], o[2 * p + 1:2 * p + 2, :]).astype(BF16)


def swa_sample(proj, cache_k, cache_v, sinks):
    n = proj.shape[0]
    nl, _, w = cache_k[0].shape[:3]
    p3 = proj.reshape(n, 1, IN_A)
    kcol = (QA + 2 * D_RNN) // KA
    ck_spec, ck = _layer((cache_k[0].reshape(nl, n, w, KA), cache_k[1]), (SAMPLES_PER_STEP, w, KA),
                         lambda z: (z, 0, 0))
    cv_spec, cv = _layer((cache_v[0].reshape(nl, n, w, KA), cache_v[1]), (SAMPLES_PER_STEP, w, KA),
                         lambda z: (z, 0, 0))
    cos, sup, sdn = _rope_tables(PAST_LEN, 1)
    tab = pl.BlockSpec((1, LANES), lambda z: (0, 0))
    gs = SAMPLES_PER_STEP
    assert n % gs == 0
    att, knew, vnew = pl.pallas_call(
        _swa_sample_kernel,
        grid=(n // gs,),
        in_specs=[pl.BlockSpec(memory_space=pltpu.SMEM),
                  pl.BlockSpec((gs, 1, QA), lambda z: (z, 0, 0)),
                  pl.BlockSpec((gs, 1, KA), lambda z: (z, 0, kcol)),
                  pl.BlockSpec((gs, 1, KA), lambda z: (z, 0, kcol + 1)),
                  ck_spec, cv_spec,
                  tab, tab, tab],
        out_specs=[pl.BlockSpec((gs, 1, QA), lambda z: (z, 0, 0)),
                   pl.BlockSpec((gs, w, KA), lambda z: (z, 0, 0)),
                   pl.BlockSpec((gs, w, KA), lambda z: (z, 0, 0))],
        out_shape=[jax.ShapeDtypeStruct((n, 1, QA), BF16),
                   jax.ShapeDtypeStruct((n, w, KA), F32),
                   jax.ShapeDtypeStruct((n, w, KA), F32)],
        compiler_params=_cparams(1),
        name="swa_sample",
    )(sinks, p3, p3, p3, ck, cv, cos, sup, sdn)
    return att.reshape(n, QA), knew, vnew


def _lru_gates(xc, wa_ref, wx_ref, ba_ref, bx_ref, lam_ref):
    rs, is_ = [], []
    for c in range(D_RNN // LRU_GROUP):
        xg = xc[:, c * LRU_GROUP:(c + 1) * LRU_GROUP].astype(BF16)
        rs.append(jnp.dot(xg, wa_ref[c], preferred_element_type=F32))
        is_.append(jnp.dot(xg, wx_ref[c], preferred_element_type=F32))
    r = _sigmoid(jnp.concatenate(rs, axis=1) + ba_ref[...])
    i = _sigmoid(jnp.concatenate(is_, axis=1) + bx_ref[...])
    log_a = LRU_C * r * _log_sigmoid(lam_ref[...])
    a = jnp.exp(log_a)
    s = -jnp.tanh(log_a) * (a * a + 1.0)
    b = jnp.where(s > 0.0, s * lax.rsqrt(s), 0.0) * (i * xc)
    return a, b


def _lru_seq_kernel(xr_ref, gr_ref, cw_ref, cb_ref, wa_ref, wx_ref, ba_ref, bx_ref, lam_ref, *rest, side_job):
    if side_job:
        wt_ref, gt_ref, rec_ref, hlast_ref, cnew_ref, w_out, g_out, xbuf_ref, a_ref, b_ref, h_ref, hc_ref = rest
        w_out[...] = wt_ref[...].T.astype(BF16)
        gt = gt_ref[...]
        pad = jnp.zeros((GATE_PAD - gt.shape[0], gt.shape[1]), F32)
        g_out[...] = jnp.concatenate([gt, pad], axis=0).T.astype(BF16)
    else:
        rec_ref, hlast_ref, cnew_ref, xbuf_ref, a_ref, b_ref, h_ref, hc_ref = rest
    t = pl.program_id(1)
    tt = xr_ref.shape[0]

    @pl.when(t == 0)
    def _():
        xbuf_ref[0:SUBLANES, :] = jnp.zeros((SUBLANES, D_RNN), F32)
        hc_ref[...] = jnp.zeros_like(hc_ref)

    xr = xr_ref[...]
    xbuf_ref[SUBLANES:SUBLANES + tt, :] = xr
    cw = cw_ref[...]
    xc = xr * cw[CONV_B - 1:CONV_B, :] + cb_ref[...]
    for j in range(CONV_B - 1):
        off = SUBLANES - (CONV_B - 1) + j
        xc = xc + xbuf_ref[off:off + tt, :] * cw[j:j + 1, :]
    xbuf_ref[0:SUBLANES, :] = xr[tt - SUBLANES:tt, :]
    cnew_ref[...] = xr[tt - (CONV_B - 1):tt, :]

    a, b = _lru_gates(xc, wa_ref, wx_ref, ba_ref, bx_ref, lam_ref)
    a_ref[...] = a
    b_ref[...] = b
    row = lax.broadcasted_iota(jnp.int32, (SUBLANES, D_RNN), 0)

    def body(g, h):
        base = pl.multiple_of(g * SUBLANES, SUBLANES)
        a8 = a_ref[pl.ds(base, SUBLANES), :]
        b8 = b_ref[pl.ds(base, SUBLANES), :]
        for s in (1, 2, 4):
            a_sh = pltpu.roll(a8, s, axis=0)
            b_sh = pltpu.roll(b8, s, axis=0)
            keep = row >= s
            b8 = jnp.where(keep, a8 * b_sh + b8, b8)
            a8 = jnp.where(keep, a8 * a_sh, a8)
        h8 = b8 + a8 * h
        h_ref[pl.ds(base, SUBLANES), :] = h8
        return h8[SUBLANES - 1:SUBLANES, :]

    h_last = lax.fori_loop(0, tt // SUBLANES, body, hc_ref[...])
    hc_ref[...] = h_last
    hlast_ref[...] = h_last
    rec_ref[...] = (h_ref[...] * _gelu_tanh(gr_ref[...])).astype(BF16)


def _lru_blockdiag(w):
    per = LRU_GROUP // BS_B
    w4 = w.reshape(D_RNN // LRU_GROUP, per, BS_B, BS_B)
    eye = jnp.eye(per, dtype=w.dtype)
    return jnp.einsum('cpij,pq->cpiqj', w4, eye).reshape(D_RNN // LRU_GROUP, LRU_GROUP, LRU_GROUP).astype(BF16)


def lru_seq(proj, nbatch, seq_len, cw, cb, wa_bd, wx_bd, ba, bx, lam, wt_job=None):
    p3 = proj.reshape(nbatch, seq_len, IN_A)
    tt = min(LRU_TT, seq_len)
    nt = seq_len // tt
    vec = lambda o: _layer(o, (1, D_RNN), lambda b, t: (0, 0))
    wspec = lambda o: _layer(o, o[0].shape[1:], lambda b, t: (0, 0, 0))
    in_specs, args = _unzip([
        (pl.BlockSpec((None, tt, D_RNN), lambda b, t: (b, t, QA // D_RNN)), p3),
        (pl.BlockSpec((None, tt, D_RNN), lambda b, t: (b, t, QA // D_RNN + 1)), p3),
        _layer(cw, (CONV_B, D_RNN), lambda b, t: (0, 0)),
        vec(cb), wspec(wa_bd), wspec(wx_bd), vec(ba), vec(bx), vec(lam)])
    out_specs = [pl.BlockSpec((None, tt, D_RNN), lambda b, t: (b, t, 0)),
                 pl.BlockSpec((None, 1, D_RNN), lambda b, t: (b, 0, 0)),
                 pl.BlockSpec((None, CONV_B - 1, D_RNN), lambda b, t: (b, 0, 0))]
    out_shape = [jax.ShapeDtypeStruct((nbatch, seq_len, D_RNN), BF16),
                 jax.ShapeDtypeStruct((nbatch, 1, D_RNN), F32),
                 jax.ShapeDtypeStruct((nbatch, CONV_B - 1, D_RNN), F32)]
    if wt_job is not None:
        wt, n_main = wt_job
        nl, n_all, kin = wt.shape
        spl, rem = divmod(nbatch * nt, nl)
        rp, rem2 = divmod(n_main, spl)
        ng = n_all - n_main
        assert rem == 0 and rem2 == 0 and rp % LANES == 0 and n_main % ng == 0 and ng % SUBLANES == 0
        step = lambda b, t: b * nt + t
        in_specs += [pl.BlockSpec((None, rp, kin), lambda b, t: (step(b, t) // spl, step(b, t) % spl, 0)),
                     pl.BlockSpec((None, ng, kin), lambda b, t: (step(b, t) // spl, n_main // ng, 0))]
        args += [wt, wt]
        out_specs += [pl.BlockSpec((None, kin, rp), lambda b, t: (step(b, t) // spl, 0, step(b, t) % spl)),
                      pl.BlockSpec((None, kin, GATE_PAD), lambda b, t: (step(b, t) // spl, 0, 0))]
        out_shape += [jax.ShapeDtypeStruct((nl, kin, n_main), BF16), jax.ShapeDtypeStruct((nl, kin, GATE_PAD), BF16)]
    outs = pl.pallas_call(
        functools.partial(_lru_seq_kernel, side_job=wt_job is not None),
        grid=(nbatch, nt),
        in_specs=in_specs,
        out_specs=out_specs,
        out_shape=out_shape,
        scratch_shapes=[pltpu.VMEM((tt + SUBLANES, D_RNN), F32),
                        pltpu.VMEM((tt, D_RNN), F32),
                        pltpu.VMEM((tt, D_RNN), F32),
                        pltpu.VMEM((tt, D_RNN), F32),
                        pltpu.VMEM((1, D_RNN), F32)],
        compiler_params=_cparams(2),
        name="lru_seq",
    )(*args)
    rec, hlast, cnew = outs[:3]
    return rec.reshape(nbatch * seq_len, D_RNN), hlast.reshape(nbatch, D_RNN), cnew, tuple(outs[3:])


def _lru_step_kernel(xr_ref, gr_ref, past_ref, h0_ref, cw_ref, cb_ref,
                     wa_ref, wx_ref, ba_ref, bx_ref, lam_ref, rec_ref, h_ref, cnew_ref):
    cw = cw_ref[...]
    xr = xr_ref[...]
    xc = xr * cw[CONV_B - 1:CONV_B, :] + cb_ref[...]
    for r in range(CONV_B - 1):
        p = past_ref[:, r, :]
        xc = xc + p * cw[r:r + 1, :]
        if r > 0:
            cnew_ref[:, r - 1, :] = p
    cnew_ref[:, CONV_B - 2, :] = xr
    a, b = _lru_gates(xc, wa_ref, wx_ref, ba_ref, bx_ref, lam_ref)
    h = b + a * h0_ref[...]
    h_ref[...] = h
    rec_ref[...] = (h * _gelu_tanh(gr_ref[...])).astype(BF16)


def lru_step(proj, h0, conv0, cw, cb, wa_bd, wx_bd, ba, bx, lam):
    n = proj.shape[0]
    col = lambda j: pl.BlockSpec((n, D_RNN), lambda i: (0, j))
    vec = lambda o: _layer(o, (1, D_RNN), lambda i: (0, 0))
    wspec = lambda o: _layer(o, o[0].shape[1:], lambda i: (0, 0, 0))
    hist = pl.BlockSpec((n, CONV_B - 1, D_RNN), lambda i: (0, 0, 0))
    in_specs, args = _unzip([
        (col(QA // D_RNN), proj), (col(QA // D_RNN + 1), proj),
        _layer(conv0, (n, CONV_B - 1, D_RNN), lambda i: (0, 0, 0)),
        _layer(h0, (n, D_RNN), lambda i: (0, 0)),
        _layer(cw, (CONV_B, D_RNN), lambda i: (0, 0)),
        vec(cb), wspec(wa_bd), wspec(wx_bd), vec(ba), vec(bx), vec(lam)])
    rec, h, conv_new = pl.pallas_call(
        _lru_step_kernel,
        grid=(1,),
        in_specs=in_specs,
        out_specs=[col(0), col(0), hist],
        out_shape=[jax.ShapeDtypeStruct((n, D_RNN), BF16), jax.ShapeDtypeStruct((n, D_RNN), F32),
                   jax.ShapeDtypeStruct((n, CONV_B - 1, D_RNN), F32)],
        compiler_params=_cparams(1),
        name="lru_step",
    )(*args)
    return rec, h, conv_new


def _cumsum_lanes(x):
    n = x.shape[1]
    lane = lax.broadcasted_iota(jnp.int32, x.shape, 1)
    s = 1
    while s < n:
        x = x + jnp.where(lane >= s, pltpu.roll(x, s, axis=1), 0.0)
        s *= 2
    return x


def _cummax_lanes(x):
    n = x.shape[1]
    lane = lax.broadcasted_iota(jnp.int32, x.shape, 1)
    s = 1
    while s < n:
        x = jnp.maximum(x, jnp.where(lane >= s, pltpu.roll(x, s, axis=1), -jnp.inf))
        s *= 2
    return x


def _mlstm_seq_kernel(q_ref, k_ref, v_ref, o_ref, gate_ref, gbias_ref, bo_ref, gn_ref,
                      out_ref, c_ref, n_ref, m_ref, cx_ref):
    ci = pl.program_id(1)
    nh = N_HEADS_C
    ln = q_ref.shape[0]

    @pl.when(ci == 0)
    def _():
        cx_ref[...] = jnp.zeros_like(cx_ref)
        m_ref[...] = jnp.zeros_like(m_ref)

    gate_t = (gate_ref[...] + gbias_ref[...]).T
    i_rows = gate_t[0:nh, :]
    b_rows = _cumsum_lanes(_log_sigmoid(gate_t[nh:2 * nh, :]))
    m_prev = m_ref[:, 0:1]
    r_rows = i_rows - b_rows
    mt_rows = b_rows + jnp.maximum(m_prev, _cummax_lanes(r_rows))
    m_last = mt_rows[:, ln - 1:ln]
    b_last = b_rows[:, ln - 1:ln]
    decay = jnp.exp(b_last + m_prev - m_last)
    per_token = jnp.concatenate([
        b_rows - mt_rows,
        jnp.exp(b_rows + m_prev - mt_rows),
        jnp.exp(-mt_rows),
        jnp.exp(b_last - b_rows + i_rows - m_last),
        jnp.zeros((LANES - 4 * nh, ln), F32)], axis=0)
    cols = per_token.T
    m_ref[...] = jnp.broadcast_to(m_last, m_ref.shape)

    ri = lax.broadcasted_iota(jnp.int32, (ln, ln), 0)
    cj = lax.broadcasted_iota(jnp.int32, (ln, ln), 1)
    causal = cj <= ri
    ones_l = jnp.ones((ln, LANES), BF16)
    ones_v = jnp.ones((DV_C, LANES), BF16)
    two = lambda a: jnp.concatenate([a] * (DV_C // LANES), axis=1)

    for h in range(nh):
        qterm = cols[:, h:h + 1]
        inter_b = jnp.broadcast_to(cols[:, nh + h:nh + h + 1], (ln, LANES))
        emt_b = jnp.broadcast_to(cols[:, 2 * nh + h:2 * nh + h + 1], (ln, LANES))
        w_col = cols[:, 3 * nh + h:3 * nh + h + 1]
        qb = q_ref[:, h * DK_C:(h + 1) * DK_C].astype(BF16)
        kh = k_ref[:, h * DK_C:(h + 1) * DK_C] * (DK_C ** -0.5)
        vx = jnp.concatenate([v_ref[:, h * DV_C:(h + 1) * DV_C].astype(BF16), ones_l], axis=1)

        s = lax.dot_general(qb, kh.astype(BF16), (((1,), (1,)), ((), ())), preferred_element_type=F32)
        amat = (jnp.exp(jnp.where(causal, qterm + r_rows[h:h + 1, :], -jnp.inf)) * s).astype(BF16)
        cx = cx_ref[h]
        qc = jnp.dot(qb, cx.astype(BF16), preferred_element_type=F32)
        av = jnp.dot(amat, vx, preferred_element_type=F32)
        num = two(inter_b) * qc[:, :DV_C] + av[:, :DV_C]
        den_b = inter_b * qc[:, DV_C:] + av[:, DV_C:]
        hh = num * two(1.0 / jnp.maximum(jnp.abs(den_b), emt_b))

        kw = (kh * w_col).astype(BF16)
        cx_ref[h] = decay[h:h + 1, :] * cx + lax.dot_general(kw, vx, (((0,), (0,)), ((), ())),
                                                             preferred_element_type=F32)

        vsl = slice(h * DV_C, (h + 1) * DV_C)
        ssq_b = jnp.dot((hh * hh).astype(BF16), ones_v, preferred_element_type=F32)
        hn = hh * two(lax.rsqrt(ssq_b * (1.0 / DV_C) + EPS)) * gn_ref[:, vsl]
        og = _sigmoid(o_ref[:, vsl] + bo_ref[:, vsl])
        out_ref[:, vsl] = (og * hn).astype(BF16)

    @pl.when(ci == pl.num_programs(1) - 1)
    def _():
        for h in range(nh):
            cx = cx_ref[h]
            c_ref[h] = cx[:, :DV_C]
            n_ref[h:h + 1, :] = cx[:, DV_C:].T[0:1, :]


def mlstm_seq(proj, gates, nbatch, seq_len, gbias, b_o, g_norm):
    ln = min(MLSTM_L, seq_len)
    p3 = proj.reshape(nbatch, seq_len, IN_C_MAIN)
    g3 = gates.reshape(nbatch, seq_len, GATE_PAD)
    in_specs, args = _unzip([
        (pl.BlockSpec((None, ln, QC), lambda b, c: (b, c, 0)), p3),
        (pl.BlockSpec((None, ln, QC), lambda b, c: (b, c, 1)), p3),
        (pl.BlockSpec((None, ln, VC), lambda b, c: (b, c, 2 * QC // VC)), p3),
        (pl.BlockSpec((None, ln, VC), lambda b, c: (b, c, 2 * QC // VC + 1)), p3),
        (pl.BlockSpec((None, ln, GATE_PAD), lambda b, c: (b, c, 0)), g3),
        _layer(gbias, (1, GATE_PAD), lambda b, c: (0, 0)),
        _layer(b_o, (1, VC), lambda b, c: (0, 0)),
        _layer(g_norm, (1, VC), lambda b, c: (0, 0))])
    out, c, n, m = pl.pallas_call(
        _mlstm_seq_kernel,
        grid=(nbatch, seq_len // ln),
        in_specs=in_specs,
        out_specs=[pl.BlockSpec((None, ln, VC), lambda b, c: (b, c, 0)),
                   pl.BlockSpec((None, N_HEADS_C, DK_C, DV_C), lambda b, c: (b, 0, 0, 0)),
                   pl.BlockSpec((None, N_HEADS_C, DK_C), lambda b, c: (b, 0, 0)),
                   pl.BlockSpec((None, N_HEADS_C, LANES), lambda b, c: (b, 0, 0))],
        out_shape=[jax.ShapeDtypeStruct((nbatch, seq_len, VC), BF16),
                   jax.ShapeDtypeStruct((nbatch, N_HEADS_C, DK_C, DV_C), F32),
                   jax.ShapeDtypeStruct((nbatch, N_HEADS_C, DK_C), F32),
                   jax.ShapeDtypeStruct((nbatch, N_HEADS_C, LANES), F32)],
        scratch_shapes=[pltpu.VMEM((N_HEADS_C, DK_C, DV_C + LANES), F32)],
        compiler_params=_cparams(2),
        name="mlstm_seq",
    )(*args)
    return out.reshape(nbatch * seq_len, VC), c, n, m[:, :, 0]


def _mlstm_step_kernel(q_ref, k_ref, v_ref, o_ref, gate_ref, gbias_ref, bo_ref, gn_ref, c0_ref, n0_ref, m0_ref,
                       out_ref, c_ref, n_ref, m_ref):
    q8 = q_ref[...]
    k8 = k_ref[...] * (DK_C ** -0.5)
    v8 = v_ref[...]
    gate = gate_ref[...] + gbias_ref[...]
    i8 = gate[:, 0:1]
    lf8 = _log_sigmoid(gate[:, 1:2])
    g8 = lf8 + m0_ref[...]
    mt = jnp.maximum(g8, i8)
    inter = jnp.exp(g8 - mt)
    wl = jnp.exp(i8 - mt)
    a8 = wl * jnp.sum(q8 * k8, axis=1, keepdims=True)
    row = lax.broadcasted_iota(jnp.int32, (N_HEADS_C, 1), 0)
    qb = q8.astype(BF16)
    kw = k8 * wl
    qc = jnp.zeros((N_HEADS_C, DV_C), F32)
    for h in range(N_HEADS_C):
        qc = qc + jnp.where(row == h, jnp.dot(qb, c0_ref[h].astype(BF16), preferred_element_type=F32), 0.0)
    vb = v8.astype(BF16)
    for h in range(N_HEADS_C):
        kz = jnp.where(row == h, kw, 0.0).astype(BF16)
        c_ref[h] = inter[h:h + 1, :] * c0_ref[h] + lax.dot_general(kz, vb, (((0,), (0,)), ((), ())),
                                                                   preferred_element_type=F32)
    n0 = n0_ref[...]
    num = inter * qc + a8 * v8
    den = inter * jnp.sum(q8 * n0, axis=1, keepdims=True) + a8
    hh = num / jnp.maximum(jnp.abs(den), jnp.exp(-mt))
    n_ref[...] = inter * n0 + kw
    m_ref[...] = mt
    hn = _rms(hh, gn_ref[...])
    og = _sigmoid(o_ref[...] + bo_ref[...])
    out_ref[...] = (og * hn).astype(BF16)


def mlstm_step(proj, gates, gbias, b_o, g_norm, c0, n0, m0):
    n = proj.shape[0]
    h = N_HEADS_C
    q = proj[:, :QC].reshape(n, h, DK_C)
    k = proj[:, QC:2 * QC].reshape(n, h, DK_C)
    v = proj[:, 2 * QC:2 * QC + VC].reshape(n, h, DV_C)
    o = proj[:, 2 * QC + VC:].reshape(n, h, DV_C)
    gcols = jnp.swapaxes(gates[:, :2 * h].reshape(n, 2, h), 1, 2)
    gb = jnp.swapaxes(gbias[0][gbias[1], 0, :2 * h].reshape(2, h), 0, 1)
    nl = c0[0].shape[0]
    per = lambda *tail: pl.BlockSpec((None,) + tail, lambda z: (z,) + (0,) * len(tail))
    lper = lambda o_, *tail: _layer(o_, (None,) + tail, lambda z: (z,) + (0,) * len(tail))
    lshared = lambda o_, *shape: _layer(o_, shape, lambda z: (0,) * len(shape))
    in_specs, args = _unzip([
        (per(h, DK_C), q), (per(h, DK_C), k), (per(h, DV_C), v), (per(h, DV_C), o), (per(h, 2), gcols),
        (pl.BlockSpec((h, 2), lambda z: (0, 0)), gb),
        lshared((b_o[0].reshape(nl, h, DV_C), b_o[1]), h, DV_C),
        lshared((g_norm[0].reshape(nl, h, DV_C), g_norm[1]), h, DV_C),
        lper(c0, h, DK_C, DV_C), lper(n0, h, DK_C), lper((m0[0].reshape(nl, n, h, 1), m0[1]), h, 1)])
    out, c, nn, m = pl.pallas_call(
        _mlstm_step_kernel,
        grid=(n,),
        in_specs=in_specs,
        out_specs=[per(h, DV_C), per(h, DK_C, DV_C), per(h, DK_C), per(h, 1)],
        out_shape=[jax.ShapeDtypeStruct((n, h, DV_C), BF16),
                   jax.ShapeDtypeStruct((n, h, DK_C, DV_C), F32),
                   jax.ShapeDtypeStruct((n, h, DK_C), F32),
                   jax.ShapeDtypeStruct((n, h, 1), F32)],
        compiler_params=_cparams(1),
        name="mlstm_step",
    )(*args)
    return out.reshape(n, VC), c, nn, m.reshape(n, h)


def _prep_weights(w):
    a_in = w['a_w_in']
    q, k, v, xr, gr = jnp.split(a_in, [QA, QA + KA, QA + 2 * KA, QA + 2 * KA + D_RNN], axis=-1)
    nc = w['c_w_in'].shape[0]
    gbias = jnp.concatenate([w['c_b_i'], w['c_b_f'], jnp.zeros((nc, GATE_PAD - 2 * N_HEADS_C), F32)], axis=-1)
    return {
        'a_w_in': jnp.concatenate([q, xr, gr, k, v], axis=-1).astype(BF16),
        'a_w_out': None,
        'c_w_in': None,
        'c_w_gate': None,
        'c_gbias': gbias.reshape(nc, 1, GATE_PAD),
        'c_w_out': None,
        'ffn_w_up': [],
        'ffn_w_down': [],
        'ffn_cp': _ffn_conv_params(w['ffn_conv_w'], w['ffn_conv_b']),
        'lru_wa': jax.vmap(_lru_blockdiag)(w['a_lru_wa']),
        'lru_wx': jax.vmap(_lru_blockdiag)(w['a_lru_wx']),
    }


def _trunk(x, seq_len, state, w, pw):
    m = x.shape[0]
    nbatch = m // seq_len
    outs = {'k': [], 'v': [], 'h': [], 'conv': [], 'C': [], 'n': [], 'm': [], 'f': []}

    def vec(name, idx):
        a = w[name]
        return a.reshape(a.shape[0], 1, a.shape[1]), idx

    for l in range(DEPTH):
        j = l // 2
        if l % 2 == 0:
            proj = norm_matmul(x, vec('norm_mix_pre', l), (pw['a_w_in'], j), IN_A, _proj_tn(m, IN_A))
            lru_args = ((w['a_conv_w'], j), vec('a_conv_b', j), (pw['lru_wa'], j), (pw['lru_wx'], j),
                        vec('a_lru_ba', j), vec('a_lru_bx', j), vec('a_lru_lambda', j))
            if state is None:
                first = ((w['ffn_w_up'], 0), (w['ffn_w_down'], 0), (w['a_w_out'].reshape(-1, D_MODEL), None),
                         (w['c_w_out'].reshape(-1, D_MODEL), None)) if l == 0 else ()
                att, k_new, v_new, cast = swa_prompt(proj, w['a_sinks'][j], nbatch, seq_len, cast_f32=first,
                                                     cast_interleave=(True, False, False, False) if l == 0 else ())
                if cast:
                    pw['ffn_w_up'].append(cast[0])
                    pw['ffn_w_down'].append(cast[1])
                    pw['a_w_out'] = cast[2].reshape(w['a_w_out'].shape)
                    pw['c_w_out'] = cast[3].reshape(w['c_w_out'].shape)
                job = (jnp.swapaxes(w['c_w_in'], 1, 2), IN_C_MAIN) if l == 0 else None
                rec, h_new, c_new, cast = lru_seq(proj, nbatch, seq_len, *lru_args, wt_job=job)
                if cast:
                    pw['c_w_in'], pw['c_w_gate'] = cast
            else:
                att, k_new, v_new = swa_sample(proj, (state['cache_k'], j), (state['cache_v'], j), w['a_sinks'][j])
                rec, h_new, c_new = lru_step(proj, (state['lru_h'], j), (state['lru_conv'], j), *lru_args)
            outs['k'].append(k_new.reshape(nbatch, -1, N_KV_A, HD_A))
            outs['v'].append(v_new.reshape(nbatch, -1, N_KV_A, HD_A))
            outs['h'].append(h_new)
            outs['conv'].append(c_new)
            x = out_proj([att, rec], (pw['a_w_out'], j), x, vec('norm_mix_post', l))
        else:
            proj, gates = norm_matmul(x, vec('norm_mix_pre', l), (pw['c_w_in'], j), IN_C_MAIN,
                                      _proj_tn(m, IN_C_MAIN), w_side=(pw['c_w_gate'], j))
            mix_args = ((pw['c_gbias'], j), vec('c_b_o', j), vec('c_norm', j))
            if state is None:
                hg, c_new, n_new, m_new = mlstm_seq(proj, gates, nbatch, seq_len, *mix_args)
            else:
                hg, c_new, n_new, m_new = mlstm_step(proj, gates, *mix_args, (state['mlstm_C'], j),
                                                     (state['mlstm_n'], j), (state['mlstm_m'], j))
            outs['C'].append(c_new)
            outs['n'].append(n_new)
            outs['m'].append(m_new)
            x = out_proj([hg], (pw['c_w_out'], j), x, vec('norm_mix_post', l))
        if state is None:
            nxt = ((w['ffn_w_up'], l + 1), (w['ffn_w_down'], l + 1)) if l + 1 < DEPTH else None
            x, f_new, cast = ffn_seq(x, seq_len, vec('norm_ffn_pre', l), pw['ffn_w_up'][l], (pw['ffn_cp'], l),
                                     pw['ffn_w_down'][l], vec('norm_ffn_post', l), next_f32=nxt)
            if cast:
                pw['ffn_w_up'].append(cast[0])
                pw['ffn_w_down'].append(cast[1])
        else:
            x, f_new = ffn_step(x, (state['ffn_conv'], l), vec('norm_ffn_pre', l), pw['ffn_w_up'][l],
                                (w['ffn_conv_w'], l), vec('ffn_conv_b', l), pw['ffn_w_down'][l],
                                vec('norm_ffn_post', l))
        outs['f'].append(f_new)
    return x, {name: jnp.stack(vals) for name, vals in outs.items()}


def kernel(x_prompt, x_sample, cache_k, cache_v, state_lru_h, state_lru_conv, state_mlstm_C, state_mlstm_n, state_mlstm_m, state_ffn_conv, norm_mix_pre, norm_mix_post, norm_ffn_pre, norm_ffn_post, a_w_in, a_sinks, a_conv_w, a_conv_b, a_lru_wa, a_lru_ba, a_lru_wx, a_lru_bx, a_lru_lambda, a_w_out, c_w_in, c_b_i, c_b_f, c_b_o, c_norm, c_w_out, ffn_w_up, ffn_conv_w, ffn_conv_b, ffn_w_down):
    w = {
        'norm_mix_pre': norm_mix_pre, 'norm_mix_post': norm_mix_post,
        'norm_ffn_pre': norm_ffn_pre, 'norm_ffn_post': norm_ffn_post,
        'a_w_in': a_w_in, 'a_sinks': a_sinks, 'a_conv_w': a_conv_w, 'a_conv_b': a_conv_b,
        'a_lru_wa': a_lru_wa, 'a_lru_ba': a_lru_ba, 'a_lru_wx': a_lru_wx, 'a_lru_bx': a_lru_bx,
        'a_lru_lambda': a_lru_lambda, 'a_w_out': a_w_out,
        'c_w_in': c_w_in, 'c_b_i': c_b_i, 'c_b_f': c_b_f, 'c_b_o': c_b_o, 'c_norm': c_norm, 'c_w_out': c_w_out,
        'ffn_w_up': ffn_w_up, 'ffn_conv_w': ffn_conv_w, 'ffn_conv_b': ffn_conv_b, 'ffn_w_down': ffn_w_down,
    }
    pw = _prep_weights(w)
    bp, sp, d = x_prompt.shape
    bs, ss, _ = x_sample.shape
    assert ss == 1, "the sample group advances one token per step"
    y_p, st_p = _trunk(x_prompt.reshape(bp * sp, d), sp, None, w, pw)
    state = {'cache_k': cache_k, 'cache_v': cache_v, 'lru_h': state_lru_h, 'lru_conv': state_lru_conv,
             'mlstm_C': state_mlstm_C, 'mlstm_n': state_mlstm_n, 'mlstm_m': state_mlstm_m,
             'ffn_conv': state_ffn_conv}
    y_s, st_s = _trunk(x_sample.reshape(bs * ss, d), 1, state, w, pw)
    return (y_p.reshape(bp, sp, d), y_s.reshape(bs, ss, d),
            st_p['k'], st_s['k'], st_p['v'], st_s['v'], st_p['h'], st_s['h'],
            st_p['conv'], st_s['conv'], st_p['C'], st_s['C'], st_p['n'], st_s['n'],
            st_p['m'], st_s['m'], st_p['f'], st_s['f'])
```
